```python
import math
import numpy as np
import jax
import jax.numpy as jnp
from jax import lax

D_MODEL = 1024
BATCH = 4
SEQ = 8192
DEPTH = 2

GRID_W = 64
CTX_LEN = 256
NORM_EPS = 1e-6
NEG_INF = -1e30

SSD_D_INNER = 1024
SSD_HEAD_DIM = 64
SSD_HEADS = SSD_D_INNER // SSD_HEAD_DIM
SSD_GROUPS = 2
SSD_STATE = 128
SSD_CONV = 5
SSD_CHUNK = 128
SSD_XBC = SSD_D_INNER + 2 * SSD_GROUPS * SSD_STATE

NA_HEADS = 8
NA_HEAD_DIM = 64
NA_WIDTH = NA_HEADS * NA_HEAD_DIM
NA_WIN_ROWS = 8
NA_WIN_COLS = 16
NA_QCOLS = 16
NA_KCOLS = NA_QCOLS + NA_WIN_COLS

CONF_WIDTH = 512
CONF_KERNEL = 31

N_BRANCH = 3

D_FF = 2816
N_EXPERTS = 8
TOP_K = 2
D_FF_EXPERT = 2816

IN_WIDTHS = (SSD_D_INNER, SSD_XBC, SSD_HEADS, SSD_HEADS, NA_WIDTH, NA_WIDTH, NA_WIDTH, 2 * CONF_WIDTH, N_BRANCH * D_MODEL)
IN_COLS = sum(IN_WIDTHS)

kernel_name = 'hybrid_ssd_natten_conformer_moe_dit'


def rms_norm(x, g):
    x32 = x.astype(jnp.float32)
    y = x32 * lax.rsqrt(jnp.mean(x32 * x32, axis=-1, keepdims=True) + NORM_EPS)
    return (y * g.astype(jnp.float32)).astype(x.dtype)


def layer_norm(x, g, b):
    x32 = x.astype(jnp.float32)
    xc = x32 - jnp.mean(x32, axis=-1, keepdims=True)
    var = jnp.mean(xc * xc, axis=-1, keepdims=True)
    return (xc * lax.rsqrt(var + NORM_EPS) * g.astype(jnp.float32) + b.astype(jnp.float32)).astype(x.dtype)


def modulate(x, shift, scale):
    return x * (1 + scale) + shift


def dw_conv(u, w, bias):
    k = w.shape[0]
    y = lax.conv_general_dilated(u, w[:, None, :].astype(u.dtype), window_strides=(1,),
                                 padding=[(k // 2, k // 2)], dimension_numbers=('NWC', 'WIO', 'NWC'),
                                 feature_group_count=u.shape[-1])
    return y + bias.astype(u.dtype)


def ssd_scan(xs, dt, a, bm, cm, h0):
    nb, seq_len, n_heads, hp = xs.shape
    n_groups, n_state = bm.shape[2], bm.shape[3]
    hg = n_heads // n_groups
    q = SSD_CHUNK
    nc = seq_len // q
    f32 = jnp.float32
    xd = (xs.astype(f32) * dt[..., None]).reshape(nb, nc, q, n_groups, hg, hp)
    bm = bm.astype(f32).reshape(nb, nc, q, n_groups, n_state)
    cm = cm.astype(f32).reshape(nb, nc, q, n_groups, n_state)
    cs = jnp.cumsum(dt.reshape(nb, nc, q, n_groups, hg) * a.reshape(n_groups, hg), axis=2)
    cs_t = jnp.moveaxis(cs, 2, -1)
    lower = jnp.tril(jnp.ones((q, q), dtype=bool))
    decay = jnp.exp(jnp.where(lower, cs_t[..., :, None] - cs_t[..., None, :], -jnp.inf))
    scores = jnp.einsum('bclgn,bcsgn->bcgls', cm, bm)
    y_diag = jnp.einsum('bcgls,bcgels,bcsgep->bclgep', scores, decay, xd)
    decay_end = jnp.exp(cs[:, :, -1:] - cs)
    states = jnp.einsum('bcsgn,bcsge,bcsgep->bcgepn', bm, decay_end, xd)
    chunk_decay = jnp.exp(cs[:, :, -1])

    def step(h, inp):
        s_c, d_c = inp
        return h * d_c[..., None, None] + s_c, h

    h_last, h_in = lax.scan(step, h0.reshape(nb, n_groups, hg, hp, n_state),
                            (jnp.moveaxis(states, 1, 0), jnp.moveaxis(chunk_decay, 1, 0)))
    h_in = jnp.moveaxis(h_in, 0, 1)
    y_off = jnp.einsum('bclgn,bcgepn,bclge->bclgep', cm, h_in, jnp.exp(cs))
    y = (y_diag + y_off).reshape(nb, seq_len, n_heads, hp)
    return y, h_last.reshape(nb, n_heads, hp, n_state)


def ssd_prepare(xbc, conv_w, conv_b):
    u = jax.nn.silu(dw_conv(xbc, conv_w, conv_b))
    nb, seq_len = u.shape[:2]
    xs, bm, cm = jnp.split(u, [SSD_D_INNER, SSD_D_INNER + SSD_GROUPS * SSD_STATE], axis=-1)
    return (xs.reshape(nb, seq_len, SSD_HEADS, SSD_HEAD_DIM),
            bm.reshape(nb, seq_len, SSD_GROUPS, SSD_STATE),
            cm.reshape(nb, seq_len, SSD_GROUPS, SSD_STATE))


def ssd_direction(xs, bm, cm, dt_raw, a_log, dt_bias, h0, reverse):
    if reverse:
        xs, bm, cm, dt_raw = (jnp.flip(t, axis=1) for t in (xs, bm, cm, dt_raw))
    dt = jax.nn.softplus(dt_raw.astype(jnp.float32) + dt_bias.astype(jnp.float32))
    a = -jnp.exp(a_log.astype(jnp.float32))
    y, h = ssd_scan(xs, dt, a, bm, cm, h0)
    if reverse:
        y = jnp.flip(y, axis=1)
    return y, h


def ssd_output(y_f, y_b, xs, z, d_skip, norm_g, w_out):
    nb, seq_len = z.shape[:2]
    y = y_f + y_b + d_skip.astype(jnp.float32)[:, None] * xs.astype(jnp.float32)
    y = y.reshape(nb, seq_len, SSD_D_INNER).astype(z.dtype) * jax.nn.silu(z)
    return rms_norm(y, norm_g) @ w_out


def na_latent(q, k, v, k_ctx, v_ctx, rpb):
    nb, seq_len = q.shape[:2]
    rows = seq_len // GRID_W
    wr = min(NA_WIN_ROWS, rows)
    n_cb = GRID_W // NA_QCOLS
    qg = q.reshape(nb, rows, n_cb, NA_QCOLS, NA_HEADS, NA_HEAD_DIM)
    kg = k.reshape(nb, rows, GRID_W, NA_HEADS, NA_HEAD_DIM)
    vg = v.reshape(nb, rows, GRID_W, NA_HEADS, NA_HEAD_DIM)
    qcol = jnp.arange(GRID_W).reshape(n_cb, NA_QCOLS)
    col_start = jnp.clip(qcol - NA_WIN_COLS // 2, 0, GRID_W - NA_WIN_COLS)
    kcol = (jnp.clip(jnp.arange(n_cb) * NA_QCOLS - NA_WIN_COLS // 2, 0, GRID_W - NA_KCOLS)[:, None]
            + jnp.arange(NA_KCOLS))
    rel = kcol[:, None, :] - col_start[:, :, None]
    col_mask = (rel >= 0) & (rel < NA_WIN_COLS)
    dcol_idx = jnp.clip(kcol[:, None, :] - qcol[:, :, None] + NA_WIN_COLS - 1, 0, 2 * NA_WIN_COLS - 2)
    row_start = jnp.clip(jnp.arange(rows) - wr // 2, 0, rows - wr)
    scale = NA_HEAD_DIM ** -0.5
    n_win = wr * NA_KCOLS

    def one_row(inp):
        q_r, r, rs = inp
        k_r = lax.dynamic_slice_in_dim(kg, rs, wr, axis=1)[:, :, kcol]
        v_r = lax.dynamic_slice_in_dim(vg, rs, wr, axis=1)[:, :, kcol]
        drow_idx = rs + jnp.arange(wr) - r + NA_WIN_ROWS - 1
        bias = rpb[:, drow_idx[None, None, :, None], dcol_idx[:, :, None, :]]
        s_win = jnp.einsum('bmqhd,bwmkhd->bhmqwk', q_r, k_r, preferred_element_type=jnp.float32) * scale
        s_win = jnp.where(col_mask[:, :, None, :], s_win + bias.astype(jnp.float32), NEG_INF)
        s_ctx = jnp.einsum('bmqhd,bjhd->bhmqj', q_r, k_ctx, preferred_element_type=jnp.float32) * scale
        s = jnp.concatenate([s_win.reshape(s_win.shape[:4] + (n_win,)), s_ctx], axis=-1)
        p = jax.nn.softmax(s, axis=-1).astype(v.dtype)
        p_win = p[..., :n_win].reshape(s_win.shape)
        p_ctx = p[..., n_win:]
        return (jnp.einsum('bhmqwk,bwmkhd->bmqhd', p_win, v_r)
                + jnp.einsum('bhmqj,bjhd->bmqhd', p_ctx, v_ctx))

    out = lax.map(one_row, (jnp.moveaxis(qg, 1, 0), jnp.arange(rows), row_start))
    return jnp.moveaxis(out, 0, 1).reshape(nb, seq_len, NA_WIDTH)


def na_context(q, k, v):
    s = jnp.einsum('bihd,bjhd->bhij', q, k, preferred_element_type=jnp.float32) * NA_HEAD_DIM ** -0.5
    p = jax.nn.softmax(s, axis=-1).astype(v.dtype)
    o = jnp.einsum('bhij,bjhd->bihd', p, v)
    return o.reshape(o.shape[0], o.shape[1], NA_WIDTH)


def conformer_branch(glu_in, conv_w, conv_b, ln_g, ln_b, w_out):
    a, g = jnp.split(glu_in, 2, axis=-1)
    u = dw_conv(a * jax.nn.sigmoid(g), conv_w, conv_b)
    return jax.nn.silu(layer_norm(u, ln_g, ln_b)) @ w_out


def merge_branches(gate_logits, branches):
    gates = jnp.split(jax.nn.sigmoid(gate_logits.astype(jnp.float32)).astype(gate_logits.dtype), N_BRANCH, axis=-1)
    out = gates[0] * branches[0]
    for g, y in zip(gates[1:], branches[1:]):
        out = out + g * y
    return out


def heads(t):
    return t.reshape(t.shape[0], t.shape[1], NA_HEADS, NA_HEAD_DIM)


def token_mixer(h_lat, h_ctx, lp, need_ctx):
    wb = jnp.split(lp['w_in'], np.cumsum(IN_WIDTHS)[:-1].tolist(), axis=1)
    w_z, w_xbc, w_dtf, w_dtb, w_q, w_k, w_v, w_glu, w_gate = wb
    xs_c, bm_c, cm_c = ssd_prepare(h_ctx @ w_xbc, lp['ssd_conv_w'], lp['ssd_conv_b'])
    xs_l, bm_l, cm_l = ssd_prepare(h_lat @ w_xbc, lp['ssd_conv_w'], lp['ssd_conv_b'])
    h0 = jnp.zeros((h_ctx.shape[0], SSD_HEADS, SSD_HEAD_DIM, SSD_STATE), jnp.float32)
    a_log, dt_bias = lp['ssd_a_log'], lp['ssd_dt_bias']
    y_cf, st_f = ssd_direction(xs_c, bm_c, cm_c, h_ctx @ w_dtf, a_log[0], dt_bias[0], h0, False)
    y_cb, st_b = ssd_direction(xs_c, bm_c, cm_c, h_ctx @ w_dtb, a_log[1], dt_bias[1], h0, True)
    y_lf, _ = ssd_direction(xs_l, bm_l, cm_l, h_lat @ w_dtf, a_log[0], dt_bias[0], st_f, False)
    y_lb, _ = ssd_direction(xs_l, bm_l, cm_l, h_lat @ w_dtb, a_log[1], dt_bias[1], st_b, True)
    ssd_l = ssd_output(y_lf, y_lb, xs_l, h_lat @ w_z, lp['ssd_d'], lp['ssd_norm'], lp['ssd_out'])
    k_c = heads(h_ctx @ w_k)
    v_c = heads(h_ctx @ w_v)
    na_l = na_latent(heads(h_lat @ w_q), heads(h_lat @ w_k), heads(h_lat @ w_v), k_c, v_c, lp['na_rpb']) @ lp['na_out']
    conf_l = conformer_branch(h_lat @ w_glu, lp['conf_conv_w'], lp['conf_conv_b'], lp['conf_ln_g'], lp['conf_ln_b'], lp['conf_out'])
    out_l = merge_branches(h_lat @ w_gate, (ssd_l, na_l, conf_l)) @ lp['w_o']
    if not need_ctx:
        return out_l, None
    ssd_c = ssd_output(y_cf, y_cb, xs_c, h_ctx @ w_z, lp['ssd_d'], lp['ssd_norm'], lp['ssd_out'])
    na_c = na_context(heads(h_ctx @ w_q), k_c, v_c) @ lp['na_out']
    conf_c = conformer_branch(h_ctx @ w_glu, lp['conf_conv_w'], lp['conf_conv_b'], lp['conf_ln_g'], lp['conf_ln_b'], lp['conf_out'])
    out_c = merge_branches(h_ctx @ w_gate, (ssd_c, na_c, conf_c)) @ lp['w_o']
    return out_l, out_c


def swiglu(h, w_gate, w_up, w_down):
    return (jax.nn.silu(h @ w_gate) * (h @ w_up)) @ w_down


def moe_swiglu(h, w_router, w_gate, w_up, w_down):
    logits = (h @ w_router).astype(jnp.float32)
    top_v, top_i = lax.top_k(logits, TOP_K)
    top_w = jax.nn.softmax(top_v, axis=-1)
    combine = jnp.sum(jax.nn.one_hot(top_i, N_EXPERTS, dtype=jnp.float32) * top_w[..., None], axis=-2).astype(h.dtype)
    out = combine[..., 0:1] * swiglu(h, w_gate[0], w_up[0], w_down[0])
    for e in range(1, N_EXPERTS):
        out = out + combine[..., e:e + 1] * swiglu(h, w_gate[e], w_up[e], w_down[e])
    return out


def setup_inputs(seed: int = 0) -> dict:
    key = jax.random.key(seed)
    keys = list(jax.random.split(key, 40))
    f32 = jnp.float32
    n_dense = (DEPTH + 1) // 2
    n_moe = DEPTH // 2

    def nrm(shape, scale=1.0):
        return jax.random.normal(keys.pop(), shape, f32) * scale

    def gain(shape):
        return 1.0 + nrm(shape, 0.05)

    def unif(shape, lo, hi):
        return jax.random.uniform(keys.pop(), shape, f32, lo, hi)

    dt0 = jnp.exp(unif((DEPTH, 2, SSD_HEADS), math.log(1e-3), math.log(1e-1)))
    return {
        'x': nrm((BATCH, SEQ, D_MODEL)),
        'c': nrm((BATCH, D_MODEL)),
        'ctx': nrm((BATCH, CTX_LEN, D_MODEL)),
        'c_ctx': nrm((D_MODEL,)),
        'ada_w': nrm((DEPTH, D_MODEL, 6 * D_MODEL), 0.5 * D_MODEL ** -0.5),
        'ada_b': nrm((DEPTH, 6 * D_MODEL), 0.02),
        'norm_mix': gain((DEPTH, D_MODEL)),
        'norm_ffn': gain((DEPTH, D_MODEL)),
        'w_in': nrm((DEPTH, D_MODEL, IN_COLS), D_MODEL ** -0.5),
        'ssd_conv_w': nrm((DEPTH, SSD_CONV, SSD_XBC), SSD_CONV ** -0.5),
        'ssd_conv_b': nrm((DEPTH, SSD_XBC), 0.02),
        'ssd_a_log': jnp.log(unif((DEPTH, 2, SSD_HEADS), 1.0, 16.0)),
        'ssd_dt_bias': dt0 + jnp.log(-jnp.expm1(-dt0)),
        'ssd_d': gain((DEPTH, SSD_HEADS)),
        'ssd_norm': gain((DEPTH, SSD_D_INNER)),
        'ssd_out': nrm((DEPTH, SSD_D_INNER, D_MODEL), SSD_D_INNER ** -0.5),
        'na_rpb': nrm((DEPTH, NA_HEADS, 2 * NA_WIN_ROWS - 1, 2 * NA_WIN_COLS - 1), 0.1),
        'na_out': nrm((DEPTH, NA_WIDTH, D_MODEL), NA_WIDTH ** -0.5),
        'conf_conv_w': nrm((DEPTH, CONF_KERNEL, CONF_WIDTH), CONF_KERNEL ** -0.5),
        'conf_conv_b': nrm((DEPTH, CONF_WIDTH), 0.02),
        'conf_ln_g': gain((DEPTH, CONF_WIDTH)),
        'conf_ln_b': nrm((DEPTH, CONF_WIDTH), 0.02),
        'conf_out': nrm((DEPTH, CONF_WIDTH, D_MODEL), CONF_WIDTH ** -0.5),
        'w_o': nrm((DEPTH, D_MODEL, D_MODEL), D_MODEL ** -0.5),
        'ffn_gate': nrm((n_dense, D_MODEL, D_FF), D_MODEL ** -0.5),
        'ffn_up': nrm((n_dense, D_MODEL, D_FF), D_MODEL ** -0.5),
        'ffn_down': nrm((n_dense, D_FF, D_MODEL), D_FF ** -0.5),
        'moe_router': nrm((n_moe, D_MODEL, N_EXPERTS), D_MODEL ** -0.5),
        'moe_gate': nrm((n_moe, N_EXPERTS, D_MODEL, D_FF_EXPERT), D_MODEL ** -0.5),
        'moe_up': nrm((n_moe, N_EXPERTS, D_MODEL, D_FF_EXPERT), D_MODEL ** -0.5),
        'moe_down': nrm((n_moe, N_EXPERTS, D_FF_EXPERT, D_MODEL), D_FF_EXPERT ** -0.5),
        'final_norm': gain((D_MODEL,)),
    }


def reference(x, c, ctx, c_ctx, ada_w, ada_b, norm_mix, norm_ffn, w_in, ssd_conv_w, ssd_conv_b,
              ssd_a_log, ssd_dt_bias, ssd_d, ssd_norm, ssd_out, na_rpb, na_out, conf_conv_w, conf_conv_b,
              conf_ln_g, conf_ln_b, conf_out, w_o, ffn_gate, ffn_up, ffn_down, moe_router, moe_gate,
              moe_up, moe_down, final_norm):
    s_c = jax.nn.silu(c)
    s_cc = jax.nn.silu(c_ctx)
    for layer in range(DEPTH):
        need_ctx = layer < DEPTH - 1
        mod_l = jnp.split((s_c @ ada_w[layer] + ada_b[layer])[:, None, :], 6, axis=-1)
        mod_c = jnp.split((s_cc @ ada_w[layer] + ada_b[layer])[None, None, :], 6, axis=-1)
        lp = {
            'w_in': w_in[layer], 'ssd_conv_w': ssd_conv_w[layer], 'ssd_conv_b': ssd_conv_b[layer],
            'ssd_a_log': ssd_a_log[layer], 'ssd_dt_bias': ssd_dt_bias[layer], 'ssd_d': ssd_d[layer],
            'ssd_norm': ssd_norm[layer], 'ssd_out': ssd_out[layer], 'na_rpb': na_rpb[layer],
            'na_out': na_out[layer], 'conf_conv_w': conf_conv_w[layer], 'conf_conv_b': conf_conv_b[layer],
            'conf_ln_g': conf_ln_g[layer], 'conf_ln_b': conf_ln_b[layer], 'conf_out': conf_out[layer],
            'w_o': w_o[layer],
        }
        h_l = modulate(rms_norm(x, norm_mix[layer]), mod_l[0], mod_l[1])
        h_c = modulate(rms_norm(ctx, norm_mix[layer]), mod_c[0], mod_c[1])
        y_l, y_c = token_mixer(h_l, h_c, lp, need_ctx)
        x = x + mod_l[2] * y_l
        h_l = modulate(rms_norm(x, norm_ffn[layer]), mod_l[3], mod_l[4])
        if need_ctx:
            ctx = ctx + mod_c[2] * y_c
            h_c = modulate(rms_norm(ctx, norm_ffn[layer]), mod_c[3], mod_c[4])
            h_all = jnp.concatenate([h_c, h_l], axis=1)
            n_ctx = ctx.shape[1]
        else:
            h_all = h_l
            n_ctx = 0
        i = layer // 2
        if layer % 2 == 0:
            f_all = swiglu(h_all, ffn_gate[i], ffn_up[i], ffn_down[i])
        else:
            f_all = moe_swiglu(h_all, moe_router[i], moe_gate[i], moe_up[i], moe_down[i])
        x = x + mod_l[5] * f_all[:, n_ctx:]
        if need_ctx:
            ctx = ctx + mod_c[5] * f_all[:, :n_ctx]
    return rms_norm(x, final_norm)
```

```python
import functools

import numpy as np
import jax
import jax.numpy as jnp
from jax import lax
from jax.experimental import pallas as pl
from jax.experimental.pallas import tpu as pltpu

F32 = jnp.float32
BF16 = jnp.bfloat16

NORM_EPS = 1e-6
NEG_INF = -1e30
GRID_W = 64

SSD_HEADS = 16
SSD_HEAD_DIM = 64
SSD_GROUPS = 2
SSD_STATE = 128
SSD_CHUNK = 128
SSD_D_INNER = SSD_HEADS * SSD_HEAD_DIM
SSD_XBC = SSD_D_INNER + 2 * SSD_GROUPS * SSD_STATE
SSD_CONV = 5

NA_HEADS = 8
NA_HEAD_DIM = 64
NA_WIDTH = NA_HEADS * NA_HEAD_DIM
NA_WIN_ROWS = 8
NA_WIN_COLS = 16

CONF_WIDTH = 512
CONF_KERNEL = 31
N_EXPERTS = 8

LANES = 128
VMEM_LIMIT = 56 << 20


def _cparams(*sem):
    return pltpu.CompilerParams(dimension_semantics=sem, vmem_limit_bytes=VMEM_LIMIT)


def _sigmoid(x):
    return 1.0 / (1.0 + jnp.exp(-x))


def _silu(x):
    return x * _sigmoid(x)


def _softplus(x):
    return jnp.maximum(x, 0.0) + jnp.log1p(jnp.exp(-jnp.abs(x)))


def _rms(x, g):
    return x * lax.rsqrt(jnp.mean(x * x, axis=-1, keepdims=True) + NORM_EPS) * g


def _dot(a, b):
    return jnp.dot(a, b, preferred_element_type=F32)


def _dot_nt(a, b):
    return lax.dot_general(a, b, (((1,), (1,)), ((), ())), preferred_element_type=F32)


def _split3(a):
    hi = a.astype(BF16)
    r1 = a - hi.astype(F32)
    mid = r1.astype(BF16)
    lo = (r1 - mid.astype(F32)).astype(BF16)
    return hi, mid, lo


def _const_spec(shape):
    nd = len(shape)
    return pl.BlockSpec(shape, lambda *_: (0,) * nd)


def _ada_body(c_ref, w_ref, b_ref, o_ref):
    s = _silu(c_ref[...]).astype(BF16)
    o_ref[...] = _dot(s, w_ref[...].astype(BF16)) + b_ref[...]


def _ada(cc, ada_w, ada_b):
    depth, d, n = ada_w.shape
    tn = 1536
    return pl.pallas_call(
        _ada_body,
        grid=(depth, n // tn),
        in_specs=[pl.BlockSpec((8, d), lambda l, j: (0, 0)),
                  pl.BlockSpec((None, d, tn), lambda l, j: (l, 0, j)),
                  pl.BlockSpec((None, 1, tn), lambda l, j: (l, 0, j))],
        out_specs=pl.BlockSpec((None, 8, tn), lambda l, j: (l, 0, j)),
        out_shape=jax.ShapeDtypeStruct((depth, 8, n), F32),
        compiler_params=_cparams("arbitrary", "arbitrary"),
        name="ada",
    )(cc, ada_w, ada_b.reshape(depth, 1, n))


_SEG_Z = (0, 1024)
_SEG_XBC = (1024, 2560)
_SEG_Q = (2560, 3072)
_SEG_K = (3072, 3584)
_SEG_V = (3584, 4096)
_SEG_GLU = (4096, 5120)


def _inproj_body(x_ref, sh_ref, sc_ref, g_ref, w_ref, wdt_ref, wdtt_ref, bl_ref, bs_ref,
                 z_ref, xbc_ref, q_ref, k_ref, v_ref, glu_ref, dt_ref, dtt_ref):
    h = (_rms(x_ref[0], g_ref[...]) * (1.0 + sc_ref[0]) + sh_ref[0]).astype(BF16)
    z_ref[0] = _dot(h, w_ref[:, _SEG_Z[0]:_SEG_Z[1]])
    xbc_ref[0] = _dot(h, w_ref[:, _SEG_XBC[0]:_SEG_XBC[1]])
    q_ref[0] = (_dot(h, w_ref[:, _SEG_Q[0]:_SEG_Q[1]]) * (NA_HEAD_DIM ** -0.5)).astype(BF16)
    k_ref[0] = _dot(h, w_ref[:, _SEG_K[0]:_SEG_K[1]]).astype(BF16)
    v_ref[0] = _dot(h, w_ref[:, _SEG_V[0]:_SEG_V[1]]).astype(BF16)
    r = _dot(h, w_ref[:, _SEG_GLU[0]:_SEG_GLU[1]])
    glu_ref[0] = r[:, :CONF_WIDTH] * _sigmoid(r[:, CONF_WIDTH:])
    dt_ref[0] = _softplus(_dot(h, wdt_ref[...]) + bl_ref[...])
    dtt_ref[0] = _softplus(_dot_nt(wdtt_ref[...], h) + bs_ref[...])


def _inproj(x, shift, scale, g, w_main, w_dt, w_dtt, b_lane, b_sub, tm):
    b, l, d = x.shape
    row = lambda n: pl.BlockSpec((1, tm, n), lambda i, j: (i, j, 0))
    mod = pl.BlockSpec((1, 1, d), lambda i, j: (i, 0, 0))
    return pl.pallas_call(
        _inproj_body,
        grid=(b, l // tm),
        in_specs=[row(d), mod, mod, _const_spec((1, d)), _const_spec(w_main.shape),
                  _const_spec(w_dt.shape), _const_spec(w_dtt.shape),
                  _const_spec(b_lane.shape), _const_spec(b_sub.shape)],
        out_specs=[row(1024), row(SSD_XBC), row(NA_WIDTH), row(NA_WIDTH), row(NA_WIDTH),
                   row(CONF_WIDTH), row(LANES),
                   pl.BlockSpec((1, 2 * SSD_HEADS, tm), lambda i, j: (i, 0, j))],
        out_shape=[jax.ShapeDtypeStruct((b, l, 1024), F32),
                   jax.ShapeDtypeStruct((b, l, SSD_XBC), F32),
                   jax.ShapeDtypeStruct((b, l, NA_WIDTH), BF16),
                   jax.ShapeDtypeStruct((b, l, NA_WIDTH), BF16),
                   jax.ShapeDtypeStruct((b, l, NA_WIDTH), BF16),
                   jax.ShapeDtypeStruct((b, l, CONF_WIDTH), F32),
                   jax.ShapeDtypeStruct((b, l, LANES), F32),
                   jax.ShapeDtypeStruct((b, 2 * SSD_HEADS, l), F32)],
        compiler_params=_cparams("parallel", "parallel"),
        name="inproj",
    )(x, shift, scale, g, w_main, w_dt, w_dtt, b_lane, b_sub)


def _dwconv_body(prev_ref, cur_ref, next_ref, w_ref, b_ref, *rest, taps, tl, hb, strip, mode):
    if mode == "ln_silu":
        lg_ref, lb_ref, o_ref, buf = rest
    else:
        o_ref, buf = rest
    i = pl.program_id(1)
    n = pl.num_programs(1)
    buf[0:hb, :] = jnp.where(i > 0, prev_ref[0], 0.0)
    buf[hb:hb + tl, :] = cur_ref[0]
    buf[hb + tl:hb + tl + hb, :] = jnp.where(i < n - 1, next_ref[0], 0.0)
    half = taps // 2
    for r0 in range(0, tl, strip):
        acc = b_ref[...] + w_ref[0:1, :] * buf[r0 + hb - half:r0 + hb - half + strip, :]
        for j in range(1, taps):
            off = r0 + hb - half + j
            acc = acc + w_ref[j:j + 1, :] * buf[off:off + strip, :]
        if mode == "silu":
            o_ref[0, r0:r0 + strip, :] = _silu(acc)
        else:
            mu = jnp.mean(acc, axis=-1, keepdims=True)
            xc = acc - mu
            var = jnp.mean(xc * xc, axis=-1, keepdims=True)
            y = xc * lax.rsqrt(var + NORM_EPS) * lg_ref[...] + lb_ref[...]
            o_ref[0, r0:r0 + strip, :] = _silu(y).astype(o_ref.dtype)


def _dwconv(x, w, bias, tl, hb, strip, mode, ln=None, out_dtype=F32):
    b, l, c = x.shape
    taps = w.shape[0]
    nh = tl // hb
    last = l // hb - 1
    ins = [x, x, x, w, bias.reshape(1, c)]
    specs = [pl.BlockSpec((1, hb, c), lambda i, j: (i, jnp.maximum(j * nh - 1, 0), 0)),
             pl.BlockSpec((1, tl, c), lambda i, j: (i, j, 0)),
             pl.BlockSpec((1, hb, c), lambda i, j: (i, jnp.minimum((j + 1) * nh, last), 0)),
             _const_spec((taps, c)), _const_spec((1, c))]
    if mode == "ln_silu":
        ins += [ln[0].reshape(1, c), ln[1].reshape(1, c)]
        specs += [_const_spec((1, c)), _const_spec((1, c))]
    return pl.pallas_call(
        functools.partial(_dwconv_body, taps=taps, tl=tl, hb=hb, strip=strip, mode=mode),
        grid=(b, l // tl),
        in_specs=specs,
        out_specs=pl.BlockSpec((1, tl, c), lambda i, j: (i, j, 0)),
        out_shape=jax.ShapeDtypeStruct((b, l, c), out_dtype),
        scratch_shapes=[pltpu.VMEM((tl + 2 * hb, c), F32)],
        compiler_params=_cparams("parallel", "parallel"),
        name="dwconv_" + mode,
    )(*ins)


def _ssd_dir(u_ref, dtc_ref, dtt_ref, al_ref, as_ref, st_ref, y_ref, d, reverse):
    q = SSD_CHUNK
    li = lax.broadcasted_iota(jnp.int32, (q, q), 0)
    si = lax.broadcasted_iota(jnp.int32, (q, q), 1)
    lane = lax.broadcasted_iota(jnp.int32, (q, LANES), 1)
    lo_half = lane < SSD_HEAD_DIM
    mask = (si >= li) if reverse else (si <= li)
    tri = jnp.where(mask, 1.0, 0.0).astype(BF16)
    tri_t = jnp.where((li >= si) if reverse else (li <= si), 1.0, 0.0).astype(BF16)

    a_lane = -jnp.exp(al_ref[...])
    a_sub = -jnp.exp(as_ref[d * SSD_HEADS:(d + 1) * SSD_HEADS, :])
    dtc = dtc_ref[0]
    hi, mid, lo = _split3(dtc * a_lane)
    cs = _dot(tri, hi) + _dot(tri, mid) + _dot(tri, lo)
    dtr = dtt_ref[0, d * SSD_HEADS:(d + 1) * SSD_HEADS, :] * a_sub
    hi, mid, lo = _split3(dtr)
    cs_r = _dot(hi, tri_t) + _dot(mid, tri_t) + _dot(lo, tri_t)

    hpg = SSD_HEADS // SSD_GROUPS
    gw = hpg * SSD_HEAD_DIM
    for g in range(SSD_GROUPS):
        bm = u_ref[0, :, SSD_D_INNER + g * SSD_STATE:SSD_D_INNER + (g + 1) * SSD_STATE]
        cm = u_ref[0, :, SSD_D_INNER + (SSD_GROUPS + g) * SSD_STATE:
                   SSD_D_INNER + (SSD_GROUPS + g + 1) * SSD_STATE].astype(BF16)
        bm_t = bm.T.astype(BF16)
        scores = _dot(cm, bm_t)
        st = st_ref[d, g]
        y_off = _dot(cm, st.astype(BF16))
        xdd = []
        tots = []
        for pp in range(hpg // 2):
            p = g * (hpg // 2) + pp
            c0 = 16 * d + 2 * p
            bc0 = jnp.broadcast_to(cs[:, c0:c0 + 1], (q, LANES))
            bc1 = jnp.broadcast_to(cs[:, c0 + 1:c0 + 2], (q, LANES))
            csx = jnp.where(lo_half, bc0, bc1)
            dtx = jnp.where(lo_half, jnp.broadcast_to(dtc[:, c0:c0 + 1], (q, LANES)),
                            jnp.broadcast_to(dtc[:, c0 + 1:c0 + 2], (q, LANES)))
            xd = u_ref[0, :, p * LANES:(p + 1) * LANES] * dtx
            xd_b = xd.astype(BF16)
            dec0 = jnp.exp(jnp.where(mask, bc0 - cs_r[2 * p:2 * p + 1, :], -jnp.inf))
            dec1 = jnp.exp(jnp.where(mask, bc1 - cs_r[2 * p + 1:2 * p + 2, :], -jnp.inf))
            y0 = _dot((scores * dec0).astype(BF16), xd_b)
            y1 = _dot((scores * dec1).astype(BF16), xd_b)
            y = jnp.where(lo_half, y0, y1) + y_off[:, pp * LANES:(pp + 1) * LANES] * jnp.exp(csx)
            y_ref[0, :, p * LANES:(p + 1) * LANES] = y
            totx = csx[0:1, :] if reverse else csx[q - 1:q, :]
            xdd.append((xd * jnp.exp(totx - csx)).astype(BF16))
            tots.append(totx)
        new_states = _dot(bm_t, jnp.concatenate(xdd, axis=1))
        st_ref[d, g] = st * jnp.exp(jnp.concatenate(tots, axis=1)) + new_states


def _ssd_body(uf_ref, ub_ref, dcf_ref, dcb_ref, dtf_ref, dtb_ref, al_ref, as_ref, h0_ref,
              yf_ref, yb_ref, ht_ref, st_ref):
    i = pl.program_id(1)

    @pl.when(i == 0)
    def _():
        st_ref[...] = h0_ref[0]

    _ssd_dir(uf_ref, dcf_ref, dtf_ref, al_ref, as_ref, st_ref, yf_ref, 0, False)
    _ssd_dir(ub_ref, dcb_ref, dtb_ref, al_ref, as_ref, st_ref, yb_ref, 1, True)

    @pl.when(i == pl.num_programs(1) - 1)
    def _():
        ht_ref[0] = st_ref[...]


def _ssd(u, dt, dtt, a_lane, a_sub, h0):
    b, l, _ = u.shape
    q = SSD_CHUNK
    nc = l // q
    fwd = lambda n: pl.BlockSpec((1, q, n), lambda i, j: (i, j, 0))
    bwd = lambda n: pl.BlockSpec((1, q, n), lambda i, j: (i, nc - 1 - j, 0))
    st_shape = (2, SSD_GROUPS, SSD_STATE, SSD_D_INNER // SSD_GROUPS)
    st_spec = pl.BlockSpec((1,) + st_shape, lambda i, j: (i, 0, 0, 0, 0))
    return pl.pallas_call(
        _ssd_body,
        grid=(b, nc),
        in_specs=[fwd(SSD_XBC), bwd(SSD_XBC), fwd(LANES), bwd(LANES),
                  pl.BlockSpec((1, 2 * SSD_HEADS, q), lambda i, j: (i, 0, j)),
                  pl.BlockSpec((1, 2 * SSD_HEADS, q), lambda i, j: (i, 0, nc - 1 - j)),
                  _const_spec((1, LANES)), _const_spec((2 * SSD_HEADS, 1)), st_spec],
        out_specs=[fwd(SSD_D_INNER), bwd(SSD_D_INNER), st_spec],
        out_shape=[jax.ShapeDtypeStruct((b, l, SSD_D_INNER), F32),
                   jax.ShapeDtypeStruct((b, l, SSD_D_INNER), F32),
                   jax.ShapeDtypeStruct((b,) + st_shape, F32)],
        scratch_shapes=[pltpu.VMEM(st_shape, F32)],
        compiler_params=_cparams("parallel", "arbitrary"),
        name="ssd",
    )(u, u, dt, dt, dtt, dtt, a_lane, a_sub, h0)


def _na_body(*refs, window, n_win):
    if window:
        q_ref, k_ref, v_ref, kc_ref, vc_ref, bias_ref, o_ref = refs
    else:
        q_ref, kc_ref, vc_ref, o_ref = refs
    tq = q_ref.shape[1]
    lane = lax.broadcasted_iota(jnp.int32, (tq, LANES), 1)
    lo_half = lane < NA_HEAD_DIM
    if window:
        r = pl.program_id(1)
        rows = pl.num_programs(1)
        wr = n_win // GRID_W
        rs = jnp.clip(r - wr // 2, 0, rows - wr)
        start = pl.multiple_of(rs * GRID_W, GRID_W)
    for hp in range(NA_HEADS // 2):
        sl = slice(hp * LANES, (hp + 1) * LANES)
        q2 = q_ref[0, :, sl]
        kc2 = kc_ref[0, :, sl]
        vc2 = vc_ref[0, :, sl]
        if window:
            k2 = k_ref[0, pl.ds(start, n_win), sl]
            v2 = v_ref[0, pl.ds(start, n_win), sl]
        outs = []
        for par in range(2):
            qm = jnp.where(lo_half if par == 0 else jnp.logical_not(lo_half), q2, jnp.zeros_like(q2))
            s_c = _dot_nt(qm, kc2)
            m = jnp.max(s_c, axis=-1, keepdims=True)
            if window:
                s_w = _dot_nt(qm, k2) + bias_ref[0, 2 * hp + par]
                m = jnp.maximum(m, jnp.max(s_w, axis=-1, keepdims=True))
                p_w = jnp.exp(s_w - m)
            p_c = jnp.exp(s_c - m)
            den = jnp.sum(p_c, axis=-1, keepdims=True)
            o = _dot(p_c.astype(BF16), vc2)
            if window:
                den = den + jnp.sum(p_w, axis=-1, keepdims=True)
                o = o + _dot(p_w.astype(BF16), v2)
            outs.append(o / den)
        o_ref[0, :, sl] = jnp.where(lo_half, outs[0], outs[1]).astype(o_ref.dtype)


def _na_bias_table(rpb, rows):
    wr = min(NA_WIN_ROWS, rows)
    r_all = np.arange(rows)
    rs_all = np.clip(r_all - wr // 2, 0, rows - wr)
    drow_all = rs_all[:, None] + np.arange(wr)[None, :] - r_all[:, None] + NA_WIN_ROWS - 1
    uniq, var_of_row = np.unique(drow_all, axis=0, return_inverse=True)
    qcol = np.arange(GRID_W)
    kcol = np.arange(GRID_W)
    col_start = np.clip(qcol - NA_WIN_COLS // 2, 0, GRID_W - NA_WIN_COLS)
    rel = kcol[None, :] - col_start[:, None]
    col_ok = (rel >= 0) & (rel < NA_WIN_COLS)
    dcol = np.clip(kcol[None, :] - qcol[:, None] + NA_WIN_COLS - 1, 0, 2 * NA_WIN_COLS - 2)
    tab = rpb[:, uniq[:, :, None, None], dcol[None, None, :, :]]
    tab = jnp.where(col_ok[None, None, None], tab.astype(F32), NEG_INF)
    tab = jnp.transpose(tab, (1, 0, 3, 2, 4)).reshape(uniq.shape[0], NA_HEADS, GRID_W, wr * GRID_W)
    return tab, jnp.asarray(var_of_row.reshape(-1), jnp.int32), wr


def _na_latent(q, k, v, kc, vc, rpb):
    b, l, w = q.shape
    rows = l // GRID_W
    tab, var_of_row, wr = _na_bias_table(rpb, rows)
    n_win = wr * GRID_W
    lc = kc.shape[1]
    full = lambda n: pl.BlockSpec((1, n, w), lambda i, r, vr: (i, 0, 0))
    grid_spec = pltpu.PrefetchScalarGridSpec(
        num_scalar_prefetch=1,
        grid=(b, rows),
        in_specs=[pl.BlockSpec((1, GRID_W, w), lambda i, r, vr: (i, r, 0)),
                  full(l), full(l), full(lc), full(lc),
                  pl.BlockSpec((1, NA_HEADS, GRID_W, n_win), lambda i, r, vr: (vr[r], 0, 0, 0))],
        out_specs=pl.BlockSpec((1, GRID_W, w), lambda i, r, vr: (i, r, 0)),
    )

    def body(vr_ref, *refs):
        _na_body(*refs, window=True, n_win=n_win)

    return pl.pallas_call(
        body,
        grid_spec=grid_spec,
        out_shape=jax.ShapeDtypeStruct((b, l, w), BF16),
        compiler_params=_cparams("parallel", "arbitrary"),
        name="na_latent",
    )(var_of_row, q, k, v, kc, vc, tab)


def _na_context(q, kc, vc):
    b, lc, w = q.shape
    tq = GRID_W
    full = pl.BlockSpec((1, lc, w), lambda i, j: (i, 0, 0))
    return pl.pallas_call(
        functools.partial(_na_body, window=False, n_win=0),
        grid=(b, lc // tq),
        in_specs=[pl.BlockSpec((1, tq, w), lambda i, j: (i, j, 0)), full, full],
        out_specs=pl.BlockSpec((1, tq, w), lambda i, j: (i, j, 0)),
        out_shape=jax.ShapeDtypeStruct((b, lc, w), BF16),
        compiler_params=_cparams("parallel", "arbitrary"),
        name="na_context",
    )(q, kc, vc)


def _tail_body(x_ref, mod_ref, gmix_ref, gffn_ref, yf_ref, yb_ref, xs_ref, z_ref, na_ref, cf_ref,
               dskip_ref, snorm_ref, wgate_ref, wssd_ref, wna_ref, wcf_ref, wo_ref, *rest, router):
    if router:
        wr_ref, xo_ref, h2_ref, lg_ref = rest
    else:
        xo_ref, h2_ref = rest
    d = x_ref.shape[2]
    x = x_ref[0]
    mod = mod_ref[0]
    h = (_rms(x, gmix_ref[...]) * (1.0 + mod[1:2]) + mod[0:1]).astype(BF16)
    y = (yf_ref[0] + yb_ref[0] + dskip_ref[...] * xs_ref[0]) * _silu(z_ref[0])
    ssd = _dot(_rms(y, snorm_ref[...]).astype(BF16), wssd_ref[...])
    merged = _sigmoid(_dot(h, wgate_ref[:, 0:d])) * ssd
    merged = merged + _sigmoid(_dot(h, wgate_ref[:, d:2 * d])) * _dot(na_ref[0], wna_ref[...])
    merged = merged + _sigmoid(_dot(h, wgate_ref[:, 2 * d:3 * d])) * _dot(cf_ref[0], wcf_ref[...])
    xn = x + mod[2:3] * _dot(merged.astype(BF16), wo_ref[...])
    xo_ref[0] = xn
    h2 = _rms(xn, gffn_ref[...]) * (1.0 + mod[4:5]) + mod[3:4]
    h2_ref[0] = h2.astype(BF16)
    if router:
        lg_ref[0] = jnp.dot(h2, wr_ref[...], preferred_element_type=F32,
                            precision=lax.Precision.HIGHEST)


def _tail(x, mod, gmix, gffn, yf, yb, u, z, na, cf, dskip, snorm, wgate, wssd, wna, wcf, wo, tm,
          w_router=None):
    b, l, d = x.shape
    row = lambda n: pl.BlockSpec((1, tm, n), lambda i, j: (i, j, 0))
    ins = [x, mod, gmix, gffn, yf, yb, u, z, na, cf, dskip, snorm, wgate, wssd, wna, wcf, wo]
    specs = [row(d), pl.BlockSpec((1, 8, d), lambda i, j: (i, 0, 0)), _const_spec((1, d)),
             _const_spec((1, d)), row(d), row(d), row(SSD_D_INNER), row(d), row(NA_WIDTH),
             row(CONF_WIDTH), _const_spec((1, d)), _const_spec((1, d)), _const_spec(wgate.shape),
             _const_spec(wssd.shape), _const_spec(wna.shape), _const_spec(wcf.shape),
             _const_spec(wo.shape)]
    out_specs = [row(d), row(d)]
    out_shape = [jax.ShapeDtypeStruct((b, l, d), F32), jax.ShapeDtypeStruct((b, l, d), BF16)]
    if w_router is not None:
        ins.append(w_router)
        specs.append(_const_spec(w_router.shape))
        out_specs.append(row(LANES))
        out_shape.append(jax.ShapeDtypeStruct((b, l, LANES), F32))
    return pl.pallas_call(
        functools.partial(_tail_body, router=w_router is not None),
        grid=(b, l // tm),
        in_specs=specs,
        out_specs=out_specs,
        out_shape=out_shape,
        compiler_params=_cparams("parallel", "parallel"),
        name="tail",
    )(*ins)


def _ffn_epilogue(x_ref, mod_ref, fn_ref, acc, o_ref, final):
    xn = x_ref[0] + mod_ref[0][5:6] * acc
    if final:
        xn = _rms(xn, fn_ref[...])
    o_ref[0] = xn


def _ffn_body(h_ref, x_ref, mod_ref, fn_ref, wg_ref, wu_ref, wd_ref, o_ref, acc_ref, *, final):
    k = pl.program_id(2)

    @pl.when(k == 0)
    def _():
        acc_ref[...] = jnp.zeros_like(acc_ref)

    h = h_ref[0]
    a = (_silu(_dot(h, wg_ref[...])) * _dot(h, wu_ref[...])).astype(BF16)
    acc_ref[...] += _dot(a, wd_ref[...])

    @pl.when(k == pl.num_programs(2) - 1)
    def _():
        _ffn_epilogue(x_ref, mod_ref, fn_ref, acc_ref[...], o_ref, final)


def _ffn(h, x, mod, fnorm, wg, wu, wd, tm, tf, final):
    b, l, d = x.shape
    ff = wg.shape[1]
    row = pl.BlockSpec((1, tm, d), lambda i, j, k: (i, j, 0))
    return pl.pallas_call(
        functools.partial(_ffn_body, final=final),
        grid=(b, l // tm, ff // tf),
        in_specs=[row, row, pl.BlockSpec((1, 8, d), lambda i, j, k: (i, 0, 0)),
                  pl.BlockSpec((1, d), lambda i, j, k: (0, 0)),
                  pl.BlockSpec((d, tf), lambda i, j, k: (0, k)),
                  pl.BlockSpec((d, tf), lambda i, j, k: (0, k)),
                  pl.BlockSpec((tf, d), lambda i, j, k: (k, 0))],
        out_specs=row,
        out_shape=jax.ShapeDtypeStruct((b, l, d), F32),
        scratch_shapes=[pltpu.VMEM((tm, d), F32)],
        compiler_params=_cparams("parallel", "parallel", "arbitrary"),
        name="ffn",
    )(h, x, mod, fnorm, wg, wu, wd)


def _top2_combine(logits):
    lane = lax.broadcasted_iota(jnp.int32, logits.shape, 1)
    lg = jnp.where(lane < N_EXPERTS, logits, -jnp.inf)
    v1 = jnp.max(lg, axis=-1, keepdims=True)
    i1 = jnp.min(jnp.where(lg == v1, lane, LANES), axis=-1, keepdims=True)
    lg2 = jnp.where(lane == i1, -jnp.inf, lg)
    v2 = jnp.max(lg2, axis=-1, keepdims=True)
    i2 = jnp.min(jnp.where(lg2 == v2, lane, LANES), axis=-1, keepdims=True)
    e2 = jnp.exp(v2 - v1)
    w1 = 1.0 / (1.0 + e2)
    w2 = e2 / (1.0 + e2)
    return jnp.where(lane == i1, w1, 0.0) + jnp.where(lane == i2, w2, 0.0)


def _moe_body(h_ref, x_ref, mod_ref, fn_ref, lg_ref, wg_ref, wu_ref, wd_ref, o_ref, acc_ref, cw_ref,
              *, final):
    e = pl.program_id(2)
    k = pl.program_id(3)

    @pl.when((e == 0) & (k == 0))
    def _():
        acc_ref[...] = jnp.zeros_like(acc_ref)
        cw_ref[...] = _top2_combine(lg_ref[0])

    h = h_ref[0]
    a = (_silu(_dot(h, wg_ref[...])) * _dot(h, wu_ref[...])).astype(BF16)
    lane = lax.broadcasted_iota(jnp.int32, cw_ref.shape, 1)
    cw = jnp.sum(jnp.where(lane == e, cw_ref[...], 0.0), axis=-1, keepdims=True)
    acc_ref[...] += cw * _dot(a, wd_ref[...])

    @pl.when((e == pl.num_programs(2) - 1) & (k == pl.num_programs(3) - 1))
    def _():
        _ffn_epilogue(x_ref, mod_ref, fn_ref, acc_ref[...], o_ref, final)


def _moe(h, x, mod, fnorm, logits, wg, wu, wd, tm, tf, final):
    b, l, d = x.shape
    ne, _, ff = wg.shape
    row = lambda n: pl.BlockSpec((1, tm, n), lambda i, j, e, k: (i, j, 0))
    return pl.pallas_call(
        functools.partial(_moe_body, final=final),
        grid=(b, l // tm, ne, ff // tf),
        in_specs=[row(d), row(d), pl.BlockSpec((1, 8, d), lambda i, j, e, k: (i, 0, 0)),
                  pl.BlockSpec((1, d), lambda i, j, e, k: (0, 0)), row(LANES),
                  pl.BlockSpec((None, d, tf), lambda i, j, e, k: (e, 0, k)),
                  pl.BlockSpec((None, d, tf), lambda i, j, e, k: (e, 0, k)),
                  pl.BlockSpec((None, tf, d), lambda i, j, e, k: (e, k, 0))],
        out_specs=row(d),
        out_shape=jax.ShapeDtypeStruct((b, l, d), F32),
        scratch_shapes=[pltpu.VMEM((tm, d), F32), pltpu.VMEM((tm, LANES), F32)],
        compiler_params=_cparams("parallel", "parallel", "arbitrary", "arbitrary"),
        name="moe",
    )(h, x, mod, fnorm, logits, wg, wu, wd)


def _row_tile(l, want):
    return want if l % want == 0 else l


def kernel(x, c, ctx, c_ctx, ada_w, ada_b, norm_mix, norm_ffn, w_in, ssd_conv_w, ssd_conv_b, ssd_a_log, ssd_dt_bias, ssd_d, ssd_norm, ssd_out, na_rpb, na_out, conf_conv_w, conf_conv_b, conf_ln_g, conf_ln_b, conf_out, w_o, ffn_gate, ffn_up, ffn_down, moe_router, moe_gate, moe_up, moe_down, final_norm):
    depth = w_in.shape[0]
    b, l, d = x.shape
    lc = ctx.shape[1]
    nh = SSD_HEADS

    cc = jnp.zeros((8, d), F32).at[:b].set(c).at[b].set(c_ctx)
    mods = _ada(cc, ada_w, ada_b)
    fnorm = final_norm.reshape(1, d)

    offs = np.cumsum((0, SSD_D_INNER, SSD_XBC, nh, nh, NA_WIDTH, NA_WIDTH, NA_WIDTH, 2 * CONF_WIDTH, 3 * d))
    seg = lambda w, i: w[:, offs[i]:offs[i + 1]]

    for layer in range(depth):
        need_ctx = layer < depth - 1
        last = layer == depth - 1
        m6 = mods[layer].reshape(8, 6, d)
        mod_l = jnp.zeros((b, 8, d), F32).at[:, :6].set(m6[:b])
        mod_c = jnp.broadcast_to(jnp.zeros((8, d), F32).at[:6].set(m6[b]), (b, 8, d))

        wl = w_in[layer]
        w_main = jnp.concatenate([seg(wl, i) for i in (0, 1, 4, 5, 6, 7)], axis=1).astype(BF16)
        w_dt = jnp.concatenate([seg(wl, 2), seg(wl, 3)], axis=1)
        w_dt_pad = jnp.zeros((d, LANES), F32).at[:, :2 * nh].set(w_dt).astype(BF16)
        w_dtt = w_dt.T.astype(BF16)
        w_gate = seg(wl, 8).astype(BF16)
        dtb = ssd_dt_bias[layer].reshape(2 * nh)
        b_lane = jnp.zeros((1, LANES), F32).at[0, :2 * nh].set(dtb)
        b_sub = dtb.reshape(2 * nh, 1)
        alog = ssd_a_log[layer].reshape(2 * nh)
        a_lane = jnp.zeros((1, LANES), F32).at[0, :2 * nh].set(alog)
        a_sub = alog.reshape(2 * nh, 1)
        dskip = jnp.repeat(ssd_d[layer], SSD_HEAD_DIM).reshape(1, SSD_D_INNER)
        gmix = norm_mix[layer].reshape(1, d)
        gffn = norm_ffn[layer].reshape(1, d)
        snorm = ssd_norm[layer].reshape(1, SSD_D_INNER)
        wssd = ssd_out[layer].astype(BF16)
        wna = na_out[layer].astype(BF16)
        wcf = conf_out[layer].astype(BF16)
        wo = w_o[layer].astype(BF16)

        def mixer_inputs(xx, mod, tm):
            return _inproj(xx, mod[:, 0:1], mod[:, 1:2], gmix, w_main, w_dt_pad, w_dtt, b_lane, b_sub, tm)

        z_c, xbc_c, q_c, k_c, v_c, glu_c, dt_c, dtt_c = mixer_inputs(ctx, mod_c, _row_tile(lc, 256))
        z_l, xbc_l, q_l, k_l, v_l, glu_l, dt_l, dtt_l = mixer_inputs(x, mod_l, _row_tile(l, 512))

        conv5 = lambda t, tl: _dwconv(t, ssd_conv_w[layer], ssd_conv_b[layer], tl, 8, 16, "silu")
        u_c = conv5(xbc_c, _row_tile(lc, 256))
        u_l = conv5(xbc_l, _row_tile(l, 256))
        h0 = jnp.zeros((b, 2, SSD_GROUPS, SSD_STATE, SSD_D_INNER // SSD_GROUPS), F32)
        yf_c, yb_c, st = _ssd(u_c, dt_c, dtt_c, a_lane, a_sub, h0)
        yf_l, yb_l, _ = _ssd(u_l, dt_l, dtt_l, a_lane, a_sub, st)

        na_l = _na_latent(q_l, k_l, v_l, k_c, v_c, na_rpb[layer])

        conv31 = lambda t, tl: _dwconv(t, conf_conv_w[layer], conf_conv_b[layer], tl, 16, 32, "ln_silu",
                                       ln=(conf_ln_g[layer], conf_ln_b[layer]), out_dtype=BF16)
        cf_l = conv31(glu_l, _row_tile(l, 256))

        is_moe = layer % 2 == 1
        w_router = None
        if is_moe:
            w_router = jnp.zeros((d, LANES), F32).at[:, :N_EXPERTS].set(moe_router[layer // 2])
        tail = functools.partial(_tail, gmix=gmix, gffn=gffn, dskip=dskip, snorm=snorm, wgate=w_gate,
                                 wssd=wssd, wna=wna, wcf=wcf, wo=wo)
        outs = tail(x, mod_l, yf=yf_l, yb=yb_l, u=u_l, z=z_l, na=na_l, cf=cf_l, tm=_row_tile(l, 256),
                    w_router=w_router)
        x_mid, h2_l = outs[0], outs[1]
        if need_ctx:
            na_c = _na_context(q_c, k_c, v_c)
            cf_c = conv31(glu_c, _row_tile(lc, 256))
            outs_c = tail(ctx, mod_c, yf=yf_c, yb=yb_c, u=u_c, z=z_c, na=na_c, cf=cf_c,
                          tm=_row_tile(lc, 256), w_router=w_router)
            ctx_mid, h2_c = outs_c[0], outs_c[1]

        i = layer // 2
        if not is_moe:
            wg, wu, wd = (t[i].astype(BF16) for t in (ffn_gate, ffn_up, ffn_down))
            tf = wg.shape[1] // 2
            x = _ffn(h2_l, x_mid, mod_l, fnorm, wg, wu, wd, _row_tile(l, 512), tf, last)
            if need_ctx:
                ctx = _ffn(h2_c, ctx_mid, mod_c, fnorm, wg, wu, wd, _row_tile(lc, 256), tf, False)
        else:
            wg, wu, wd = (t[i].astype(BF16) for t in (moe_gate, moe_up, moe_down))
            tf = wg.shape[2] // 2
            x = _moe(h2_l, x_mid, mod_l, fnorm, outs[2], wg, wu, wd, _row_tile(l, 512), tf, last)
            if need_ctx:
                ctx = _moe(h2_c, ctx_mid, mod_c, fnorm, outs_c[2], wg, wu, wd, _row_tile(lc, 256), tf,
                           False)
    return x
```

```python
import functools

import numpy as np
import jax
import jax.numpy as jnp
from jax import lax
from jax.experimental import pallas as pl
from jax.experimental.pallas import tpu as pltpu

F32 = jnp.float32
BF16 = jnp.bfloat16

NORM_EPS = 1e-6
NEG_INF = -1e30
GRID_W = 64

SSD_HEADS = 16
SSD_HEAD_DIM = 64
SSD_GROUPS = 2
SSD_STATE = 128
SSD_CHUNK = 128
SSD_D_INNER = SSD_HEADS * SSD_HEAD_DIM
SSD_XBC = SSD_D_INNER + 2 * SSD_GROUPS * SSD_STATE
SSD_CONV = 5

NA_HEADS = 8
NA_HEAD_DIM = 64
NA_WIDTH = NA_HEADS * NA_HEAD_DIM
NA_WIN_ROWS = 8
NA_WIN_COLS = 16

CONF_WIDTH = 512
CONF_KERNEL = 31
N_EXPERTS = 8

LANES = 128
VMEM_LIMIT = 56 << 20


def _cparams(*sem):
    return pltpu.CompilerParams(dimension_semantics=sem, vmem_limit_bytes=VMEM_LIMIT)


def _sigmoid(x):
    return 1.0 / (1.0 + jnp.exp(-x))


def _silu(x):
    return x * _sigmoid(x)


def _softplus(x):
    return jnp.maximum(x, 0.0) + jnp.log1p(jnp.exp(-jnp.abs(x)))


def _rms(x, g):
    return x * lax.rsqrt(jnp.mean(x * x, axis=-1, keepdims=True) + NORM_EPS) * g


def _dot(a, b):
    return jnp.dot(a, b, preferred_element_type=F32)


def _dot_nt(a, b):
    return lax.dot_general(a, b, (((1,), (1,)), ((), ())), preferred_element_type=F32)


def _split3(a):
    hi = a.astype(BF16)
    r1 = a - hi.astype(F32)
    mid = r1.astype(BF16)
    lo = (r1 - mid.astype(F32)).astype(BF16)
    return hi, mid, lo


def _const_spec(shape):
    nd = len(shape)
    return pl.BlockSpec(shape, lambda *_: (0,) * nd)


def _ada_body(c_ref, w_ref, b_ref, o_ref):
    s = _silu(c_ref[...]).astype(BF16)
    o_ref[...] = _dot(s, w_ref[...].astype(BF16)) + b_ref[...]


def _ada(cc, ada_w, ada_b):
    depth, d, n = ada_w.shape
    tn = 1536
    return pl.pallas_call(
        _ada_body,
        grid=(depth, n // tn),
        in_specs=[pl.BlockSpec((8, d), lambda l, j: (0, 0)),
                  pl.BlockSpec((None, d, tn), lambda l, j: (l, 0, j)),
                  pl.BlockSpec((None, 1, tn), lambda l, j: (l, 0, j))],
        out_specs=pl.BlockSpec((None, 8, tn), lambda l, j: (l, 0, j)),
        out_shape=jax.ShapeDtypeStruct((depth, 8, n), F32),
        compiler_params=_cparams("arbitrary", "arbitrary"),
        name="ada",
    )(cc, ada_w, ada_b.reshape(depth, 1, n))


_SEG_Z = (0, 1024)
_SEG_XBC = (1024, 2560)
_SEG_Q = (2560, 3072)
_SEG_K = (3072, 3584)
_SEG_V = (3584, 4096)
_SEG_GLU = (4096, 5120)


def _inproj_body(x_ref, sh_ref, sc_ref, g_ref, w_ref, wdt_ref, wdtt_ref, bl_ref, bs_ref,
                 z_ref, xbc_ref, q_ref, k_ref, v_ref, glu_ref, dt_ref, dtt_ref):
    h = (_rms(x_ref[0], g_ref[...]) * (1.0 + sc_ref[0]) + sh_ref[0]).astype(BF16)
    z_ref[0] = _dot(h, w_ref[:, _SEG_Z[0]:_SEG_Z[1]])
    xbc_ref[0] = _dot(h, w_ref[:, _SEG_XBC[0]:_SEG_XBC[1]])
    q_ref[0] = (_dot(h, w_ref[:, _SEG_Q[0]:_SEG_Q[1]]) * (NA_HEAD_DIM ** -0.5)).astype(BF16)
    k_ref[0] = _dot(h, w_ref[:, _SEG_K[0]:_SEG_K[1]]).astype(BF16)
    v_ref[0] = _dot(h, w_ref[:, _SEG_V[0]:_SEG_V[1]]).astype(BF16)
    r = _dot(h, w_ref[:, _SEG_GLU[0]:_SEG_GLU[1]])
    glu_ref[0] = r[:, :CONF_WIDTH] * _sigmoid(r[:, CONF_WIDTH:])
    dt_ref[0] = _softplus(_dot(h, wdt_ref[...]) + bl_ref[...])
    dtt_ref[0] = _softplus(_dot_nt(wdtt_ref[...], h) + bs_ref[...])


def _inproj(x, shift, scale, g, w_main, w_dt, w_dtt, b_lane, b_sub, tm):
    b, l, d = x.shape
    row = lambda n: pl.BlockSpec((1, tm, n), lambda i, j: (i, j, 0))
    mod = pl.BlockSpec((1, 1, d), lambda i, j: (i, 0, 0))
    return pl.pallas_call(
        _inproj_body,
        grid=(b, l // tm),
        in_specs=[row(d), mod, mod, _const_spec((1, d)), _const_spec(w_main.shape),
                  _const_spec(w_dt.shape), _const_spec(w_dtt.shape),
                  _const_spec(b_lane.shape), _const_spec(b_sub.shape)],
        out_specs=[row(1024), row(SSD_XBC), row(NA_WIDTH), row(NA_WIDTH), row(NA_WIDTH),
                   row(CONF_WIDTH), row(LANES),
                   pl.BlockSpec((1, 2 * SSD_HEADS, tm), lambda i, j: (i, 0, j))],
        out_shape=[jax.ShapeDtypeStruct((b, l, 1024), F32),
                   jax.ShapeDtypeStruct((b, l, SSD_XBC), F32),
                   jax.ShapeDtypeStruct((b, l, NA_WIDTH), BF16),
                   jax.ShapeDtypeStruct((b, l, NA_WIDTH), BF16),
                   jax.ShapeDtypeStruct((b, l, NA_WIDTH), BF16),
                   jax.ShapeDtypeStruct((b, l, CONF_WIDTH), F32),
                   jax.ShapeDtypeStruct((b, l, LANES), F32),
                   jax.ShapeDtypeStruct((b, 2 * SSD_HEADS, l), F32)],
        compiler_params=_cparams("parallel", "parallel"),
        name="inproj",
    )(x, shift, scale, g, w_main, w_dt, w_dtt, b_lane, b_sub)


def _dwconv_body(prev_ref, cur_ref, next_ref, w_ref, b_ref, *rest, taps, tl, hb, strip, mode):
    if mode == "ln_silu":
        lg_ref, lb_ref, o_ref, buf = rest
    else:
        o_ref, buf = rest
    i = pl.program_id(1)
    n = pl.num_programs(1)
    buf[0:hb, :] = jnp.where(i > 0, prev_ref[0], 0.0)
    buf[hb:hb + tl, :] = cur_ref[0]
    buf[hb + tl:hb + tl + hb, :] = jnp.where(i < n - 1, next_ref[0], 0.0)
    half = taps // 2
    for r0 in range(0, tl, strip):
        acc = b_ref[...] + w_ref[0:1, :] * buf[r0 + hb - half:r0 + hb - half + strip, :]
        for j in range(1, taps):
            off = r0 + hb - half + j
            acc = acc + w_ref[j:j + 1, :] * buf[off:off + strip, :]
        if mode == "silu":
            o_ref[0, r0:r0 + strip, :] = _silu(acc)
        else:
            mu = jnp.mean(acc, axis=-1, keepdims=True)
            xc = acc - mu
            var = jnp.mean(xc * xc, axis=-1, keepdims=True)
            y = xc * lax.rsqrt(var + NORM_EPS) * lg_ref[...] + lb_ref[...]
            o_ref[0, r0:r0 + strip, :] = _silu(y).astype(o_ref.dtype)


def _dwconv(x, w, bias, tl, hb, strip, mode, ln=None, out_dtype=F32):
    b, l, c = x.shape
    taps = w.shape[0]
    nh = tl // hb
    last = l // hb - 1
    ins = [x, x, x, w, bias.reshape(1, c)]
    specs = [pl.BlockSpec((1, hb, c), lambda i, j: (i, jnp.maximum(j * nh - 1, 0), 0)),
             pl.BlockSpec((1, tl, c), lambda i, j: (i, j, 0)),
             pl.BlockSpec((1, hb, c), lambda i, j: (i, jnp.minimum((j + 1) * nh, last), 0)),
             _const_spec((taps, c)), _const_spec((1, c))]
    if mode == "ln_silu":
        ins += [ln[0].reshape(1, c), ln[1].reshape(1, c)]
        specs += [_const_spec((1, c)), _const_spec((1, c))]
    return pl.pallas_call(
        functools.partial(_dwconv_body, taps=taps, tl=tl, hb=hb, strip=strip, mode=mode),
        grid=(b, l // tl),
        in_specs=specs,
        out_specs=pl.BlockSpec((1, tl, c), lambda i, j: (i, j, 0)),
        out_shape=jax.ShapeDtypeStruct((b, l, c), out_dtype),
        scratch_shapes=[pltpu.VMEM((tl + 2 * hb, c), F32)],
        compiler_params=_cparams("parallel", "parallel"),
        name="dwconv_" + mode,
    )(*ins)


def _ssd_dir(u_ref, dtc_ref, dtt_ref, al_ref, as_ref, st_ref, y_ref, d, reverse):
    q = SSD_CHUNK
    li = lax.broadcasted_iota(jnp.int32, (q, q), 0)
    si = lax.broadcasted_iota(jnp.int32, (q, q), 1)
    lane = lax.broadcasted_iota(jnp.int32, (q, LANES), 1)
    lo_half = lane < SSD_HEAD_DIM
    mask = (si >= li) if reverse else (si <= li)
    tri = jnp.where(mask, 1.0, 0.0).astype(BF16)
    tri_t = jnp.where((li >= si) if reverse else (li <= si), 1.0, 0.0).astype(BF16)

    a_lane = -jnp.exp(al_ref[...])
    a_sub = -jnp.exp(as_ref[d * SSD_HEADS:(d + 1) * SSD_HEADS, :])
    dtc = dtc_ref[0]
    hi, mid, lo = _split3(dtc * a_lane)
    cs = _dot(tri, hi) + _dot(tri, mid) + _dot(tri, lo)
    dtr = dtt_ref[0, d * SSD_HEADS:(d + 1) * SSD_HEADS, :] * a_sub
    hi, mid, lo = _split3(dtr)
    cs_r = _dot(hi, tri_t) + _dot(mid, tri_t) + _dot(lo, tri_t)

    hpg = SSD_HEADS // SSD_GROUPS
    gw = hpg * SSD_HEAD_DIM
    for g in range(SSD_GROUPS):
        bm = u_ref[0, :, SSD_D_INNER + g * SSD_STATE:SSD_D_INNER + (g + 1) * SSD_STATE]
        cm = u_ref[0, :, SSD_D_INNER + (SSD_GROUPS + g) * SSD_STATE:
                   SSD_D_INNER + (SSD_GROUPS + g + 1) * SSD_STATE].astype(BF16)
        bm_t = bm.T.astype(BF16)
        scores = _dot(cm, bm_t)
        st = st_ref[d, g]
        y_off = _dot(cm, st.astype(BF16))
        xdd = []
        tots = []
        for pp in range(hpg // 2):
            p = g * (hpg // 2) + pp
            c0 = 16 * d + 2 * p
            bc0 = jnp.broadcast_to(cs[:, c0:c0 + 1], (q, LANES))
            bc1 = jnp.broadcast_to(cs[:, c0 + 1:c0 + 2], (q, LANES))
            csx = jnp.where(lo_half, bc0, bc1)
            dtx = jnp.where(lo_half, jnp.broadcast_to(dtc[:, c0:c0 + 1], (q, LANES)),
                            jnp.broadcast_to(dtc[:, c0 + 1:c0 + 2], (q, LANES)))
            xd = u_ref[0, :, p * LANES:(p + 1) * LANES] * dtx
            xd_b = xd.astype(BF16)
            dec0 = jnp.exp(jnp.where(mask, bc0 - cs_r[2 * p:2 * p + 1, :], -jnp.inf))
            dec1 = jnp.exp(jnp.where(mask, bc1 - cs_r[2 * p + 1:2 * p + 2, :], -jnp.inf))
            y0 = _dot((scores * dec0).astype(BF16), xd_b)
            y1 = _dot((scores * dec1).astype(BF16), xd_b)
            y = jnp.where(lo_half, y0, y1) + y_off[:, pp * LANES:(pp + 1) * LANES] * jnp.exp(csx)
            y_ref[0, :, p * LANES:(p + 1) * LANES] = y
            totx = csx[0:1, :] if reverse else csx[q - 1:q, :]
            xdd.append((xd * jnp.exp(totx - csx)).astype(BF16))
            tots.append(totx)
        new_states = _dot(bm_t, jnp.concatenate(xdd, axis=1))
        st_ref[d, g] = st * jnp.exp(jnp.concatenate(tots, axis=1)) + new_states


def _ssd_body(uf_ref, ub_ref, dcf_ref, dcb_ref, dtf_ref, dtb_ref, al_ref, as_ref, h0_ref,
              yf_ref, yb_ref, ht_ref, st_ref):
    i = pl.program_id(1)

    @pl.when(i == 0)
    def _():
        st_ref[...] = h0_ref[0]

    _ssd_dir(uf_ref, dcf_ref, dtf_ref, al_ref, as_ref, st_ref, yf_ref, 0, False)
    _ssd_dir(ub_ref, dcb_ref, dtb_ref, al_ref, as_ref, st_ref, yb_ref, 1, True)

    @pl.when(i == pl.num_programs(1) - 1)
    def _():
        ht_ref[0] = st_ref[...]


def _ssd(u, dt, dtt, a_lane, a_sub, h0):
    b, l, _ = u.shape
    q = SSD_CHUNK
    nc = l // q
    fwd = lambda n: pl.BlockSpec((1, q, n), lambda i, j: (i, j, 0))
    bwd = lambda n: pl.BlockSpec((1, q, n), lambda i, j: (i, nc - 1 - j, 0))
    st_shape = (2, SSD_GROUPS, SSD_STATE, SSD_D_INNER // SSD_GROUPS)
    st_spec = pl.BlockSpec((1,) + st_shape, lambda i, j: (i, 0, 0, 0, 0))
    return pl.pallas_call(
        _ssd_body,
        grid=(b, nc),
        in_specs=[fwd(SSD_XBC), bwd(SSD_XBC), fwd(LANES), bwd(LANES),
                  pl.BlockSpec((1, 2 * SSD_HEADS, q), lambda i, j: (i, 0, j)),
                  pl.BlockSpec((1, 2 * SSD_HEADS, q), lambda i, j: (i, 0, nc - 1 - j)),
                  _const_spec((1, LANES)), _const_spec((2 * SSD_HEADS, 1)), st_spec],
        out_specs=[fwd(SSD_D_INNER), bwd(SSD_D_INNER), st_spec],
        out_shape=[jax.ShapeDtypeStruct((b, l, SSD_D_INNER), F32),
                   jax.ShapeDtypeStruct((b, l, SSD_D_INNER), F32),
                   jax.ShapeDtypeStruct((b,) + st_shape, F32)],
        scratch_shapes=[pltpu.VMEM(st_shape, F32)],
        compiler_params=_cparams("parallel", "arbitrary"),
        name="ssd",
    )(u, u, dt, dt, dtt, dtt, a_lane, a_sub, h0)


def _na_body(*refs, window, n_win):
    if window:
        q_ref, k_ref, v_ref, kc_ref, vc_ref, bias_ref, o_ref = refs
    else:
        q_ref, kc_ref, vc_ref, o_ref = refs
    tq = q_ref.shape[1]
    lane = lax.broadcasted_iota(jnp.int32, (tq, LANES), 1)
    lo_half = lane < NA_HEAD_DIM
    if window:
        r = pl.program_id(1)
        rows = pl.num_programs(1)
        wr = n_win // GRID_W
        rs = jnp.clip(r - wr // 2, 0, rows - wr)
        start = pl.multiple_of(rs * GRID_W, GRID_W)
    scores = []
    for h in range(NA_HEADS):
        sl = slice((h // 2) * LANES, (h // 2 + 1) * LANES)
        q2 = q_ref[0, :, sl]
        qm = jnp.where(lo_half if h % 2 == 0 else jnp.logical_not(lo_half), q2, jnp.zeros_like(q2))
        s = _dot_nt(qm, kc_ref[0, :, sl])
        if window:
            s_w = _dot_nt(qm, k_ref[0, pl.ds(start, n_win), sl]) + bias_ref[0, h]
            s = jnp.concatenate([s_w, s], axis=1)
        scores.append(s)
    s_all = jnp.concatenate(scores, axis=0)
    p_all = jnp.exp(s_all - jnp.max(s_all, axis=-1, keepdims=True))
    inv = 1.0 / jnp.sum(p_all, axis=-1, keepdims=True)
    p_all = p_all.astype(BF16)
    for hp in range(NA_HEADS // 2):
        sl = slice(hp * LANES, (hp + 1) * LANES)
        outs = []
        for par in range(2):
            h = 2 * hp + par
            p = p_all[h * tq:(h + 1) * tq]
            o = _dot(p[:, n_win:], vc_ref[0, :, sl])
            if window:
                o = o + _dot(p[:, :n_win], v_ref[0, pl.ds(start, n_win), sl])
            outs.append(o * inv[h * tq:(h + 1) * tq])
        o_ref[0, :, sl] = jnp.where(lo_half, outs[0], outs[1]).astype(o_ref.dtype)


def _na_bias_table(rpb, rows):
    wr = min(NA_WIN_ROWS, rows)
    r_all = np.arange(rows)
    rs_all = np.clip(r_all - wr // 2, 0, rows - wr)
    drow_all = rs_all[:, None] + np.arange(wr)[None, :] - r_all[:, None] + NA_WIN_ROWS - 1
    uniq, var_of_row = np.unique(drow_all, axis=0, return_inverse=True)
    qcol = np.arange(GRID_W)
    kcol = np.arange(GRID_W)
    col_start = np.clip(qcol - NA_WIN_COLS // 2, 0, GRID_W - NA_WIN_COLS)
    rel = kcol[None, :] - col_start[:, None]
    col_ok = (rel >= 0) & (rel < NA_WIN_COLS)
    dcol = np.clip(kcol[None, :] - qcol[:, None] + NA_WIN_COLS - 1, 0, 2 * NA_WIN_COLS - 2)
    n_dcol = 2 * NA_WIN_COLS - 1
    onehot = jnp.asarray(dcol[None, :, :] == np.arange(n_dcol)[:, None, None], F32)
    sel = rpb.astype(F32)[:, uniq.reshape(-1), :]
    tab = jnp.einsum("hrc,cqk->hqrk", sel, onehot, precision=lax.Precision.HIGHEST)
    tab = tab.reshape(NA_HEADS, GRID_W, uniq.shape[0], wr, GRID_W)
    tab = jnp.where(col_ok[None, :, None, None, :], tab, NEG_INF)
    tab = jnp.transpose(tab, (2, 0, 1, 3, 4)).reshape(uniq.shape[0], NA_HEADS, GRID_W, wr * GRID_W)
    return tab, jnp.asarray(var_of_row.reshape(-1), jnp.int32), wr


def _na_latent(q, k, v, kc, vc, rpb):
    b, l, w = q.shape
    rows = l // GRID_W
    tab, var_of_row, wr = _na_bias_table(rpb, rows)
    n_win = wr * GRID_W
    lc = kc.shape[1]
    full = lambda n: pl.BlockSpec((1, n, w), lambda i, r, vr: (i, 0, 0))
    grid_spec = pltpu.PrefetchScalarGridSpec(
        num_scalar_prefetch=1,
        grid=(b, rows),
        in_specs=[pl.BlockSpec((1, GRID_W, w), lambda i, r, vr: (i, r, 0)),
                  full(l), full(l), full(lc), full(lc),
                  pl.BlockSpec((1, NA_HEADS, GRID_W, n_win), lambda i, r, vr: (vr[r], 0, 0, 0))],
        out_specs=pl.BlockSpec((1, GRID_W, w), lambda i, r, vr: (i, r, 0)),
    )

    def body(vr_ref, *refs):
        _na_body(*refs, window=True, n_win=n_win)

    return pl.pallas_call(
        body,
        grid_spec=grid_spec,
        out_shape=jax.ShapeDtypeStruct((b, l, w), BF16),
        compiler_params=_cparams("parallel", "arbitrary"),
        name="na_latent",
    )(var_of_row, q, k, v, kc, vc, tab)


def _na_context(q, kc, vc):
    b, lc, w = q.shape
    tq = GRID_W
    full = pl.BlockSpec((1, lc, w), lambda i, j: (i, 0, 0))
    return pl.pallas_call(
        functools.partial(_na_body, window=False, n_win=0),
        grid=(b, lc // tq),
        in_specs=[pl.BlockSpec((1, tq, w), lambda i, j: (i, j, 0)), full, full],
        out_specs=pl.BlockSpec((1, tq, w), lambda i, j: (i, j, 0)),
        out_shape=jax.ShapeDtypeStruct((b, lc, w), BF16),
        compiler_params=_cparams("parallel", "arbitrary"),
        name="na_context",
    )(q, kc, vc)


def _tail_body(x_ref, mod_ref, gmix_ref, gffn_ref, yf_ref, yb_ref, xs_ref, z_ref, na_ref, cf_ref,
               dskip_ref, snorm_ref, wgate_ref, wssd_ref, wna_ref, wcf_ref, wo_ref, *rest, router):
    if router:
        wr_ref, xo_ref, h2_ref, lg_ref = rest
    else:
        xo_ref, h2_ref = rest
    d = x_ref.shape[2]
    x = x_ref[0]
    mod = mod_ref[0]
    h = (_rms(x, gmix_ref[...]) * (1.0 + mod[1:2]) + mod[0:1]).astype(BF16)
    y = (yf_ref[0] + yb_ref[0] + dskip_ref[...] * xs_ref[0]) * _silu(z_ref[0])
    ssd = _dot(_rms(y, snorm_ref[...]).astype(BF16), wssd_ref[...])
    merged = _sigmoid(_dot(h, wgate_ref[:, 0:d])) * ssd
    merged = merged + _sigmoid(_dot(h, wgate_ref[:, d:2 * d])) * _dot(na_ref[0], wna_ref[...])
    merged = merged + _sigmoid(_dot(h, wgate_ref[:, 2 * d:3 * d])) * _dot(cf_ref[0], wcf_ref[...])
    xn = x + mod[2:3] * _dot(merged.astype(BF16), wo_ref[...])
    xo_ref[0] = xn
    h2 = _rms(xn, gffn_ref[...]) * (1.0 + mod[4:5]) + mod[3:4]
    h2_ref[0] = h2.astype(BF16)
    if router:
        hi = h2.astype(BF16)
        mid = (h2 - hi.astype(F32)).astype(BF16)
        r = _dot(hi, wr_ref[...])
        lg_ref[0] = r[:, :LANES] + r[:, LANES:] + _dot(mid, wr_ref[:, :LANES])


def _tail(x, mod, gmix, gffn, yf, yb, u, z, na, cf, dskip, snorm, wgate, wssd, wna, wcf, wo, tm,
          w_router=None):
    b, l, d = x.shape
    row = lambda n: pl.BlockSpec((1, tm, n), lambda i, j: (i, j, 0))
    ins = [x, mod, gmix, gffn, yf, yb, u, z, na, cf, dskip, snorm, wgate, wssd, wna, wcf, wo]
    specs = [row(d), pl.BlockSpec((1, 8, d), lambda i, j: (i, 0, 0)), _const_spec((1, d)),
             _const_spec((1, d)), row(d), row(d), row(SSD_D_INNER), row(d), row(NA_WIDTH),
             row(CONF_WIDTH), _const_spec((1, d)), _const_spec((1, d)), _const_spec(wgate.shape),
             _const_spec(wssd.shape), _const_spec(wna.shape), _const_spec(wcf.shape),
             _const_spec(wo.shape)]
    out_specs = [row(d), row(d)]
    out_shape = [jax.ShapeDtypeStruct((b, l, d), F32), jax.ShapeDtypeStruct((b, l, d), BF16)]
    if w_router is not None:
        ins.append(w_router)
        specs.append(_const_spec(w_router.shape))
        out_specs.append(row(LANES))
        out_shape.append(jax.ShapeDtypeStruct((b, l, LANES), F32))
    return pl.pallas_call(
        functools.partial(_tail_body, router=w_router is not None),
        grid=(b, l // tm),
        in_specs=specs,
        out_specs=out_specs,
        out_shape=out_shape,
        compiler_params=_cparams("parallel", "parallel"),
        name="tail",
    )(*ins)


def _ffn_epilogue(x_ref, mod_ref, fn_ref, acc, o_ref, final):
    xn = x_ref[0] + mod_ref[0][5:6] * acc
    if final:
        xn = _rms(xn, fn_ref[...])
    o_ref[0] = xn


def _ffn_body(h_ref, x_ref, mod_ref, fn_ref, wg_ref, wu_ref, wd_ref, o_ref, acc_ref, *, final):
    k = pl.program_id(2)

    @pl.when(k == 0)
    def _():
        acc_ref[...] = jnp.zeros_like(acc_ref)

    h = h_ref[0]
    a = (_silu(_dot(h, wg_ref[...])) * _dot(h, wu_ref[...])).astype(BF16)
    acc_ref[...] += _dot(a, wd_ref[...])

    @pl.when(k == pl.num_programs(2) - 1)
    def _():
        _ffn_epilogue(x_ref, mod_ref, fn_ref, acc_ref[...], o_ref, final)


def _ffn(h, x, mod, fnorm, wg, wu, wd, tm, tf, final):
    b, l, d = x.shape
    ff = wg.shape[1]
    row = pl.BlockSpec((1, tm, d), lambda i, j, k: (i, j, 0))
    return pl.pallas_call(
        functools.partial(_ffn_body, final=final),
        grid=(b, l // tm, ff // tf),
        in_specs=[row, row, pl.BlockSpec((1, 8, d), lambda i, j, k: (i, 0, 0)),
                  pl.BlockSpec((1, d), lambda i, j, k: (0, 0)),
                  pl.BlockSpec((d, tf), lambda i, j, k: (0, k)),
                  pl.BlockSpec((d, tf), lambda i, j, k: (0, k)),
                  pl.BlockSpec((tf, d), lambda i, j, k: (k, 0))],
        out_specs=row,
        out_shape=jax.ShapeDtypeStruct((b, l, d), F32),
        scratch_shapes=[pltpu.VMEM((tm, d), F32)],
        compiler_params=_cparams("parallel", "parallel", "arbitrary"),
        name="ffn",
    )(h, x, mod, fnorm, wg, wu, wd)


def _top2_combine(logits):
    lane = lax.broadcasted_iota(jnp.int32, logits.shape, 1)
    lg = jnp.where(lane < N_EXPERTS, logits, -jnp.inf)
    v1 = jnp.max(lg, axis=-1, keepdims=True)
    i1 = jnp.min(jnp.where(lg == v1, lane, LANES), axis=-1, keepdims=True)
    lg2 = jnp.where(lane == i1, -jnp.inf, lg)
    v2 = jnp.max(lg2, axis=-1, keepdims=True)
    i2 = jnp.min(jnp.where(lg2 == v2, lane, LANES), axis=-1, keepdims=True)
    e2 = jnp.exp(v2 - v1)
    w1 = 1.0 / (1.0 + e2)
    w2 = e2 / (1.0 + e2)
    return jnp.where(lane == i1, w1, 0.0) + jnp.where(lane == i2, w2, 0.0)


def _moe_body(h_ref, x_ref, mod_ref, fn_ref, lg_ref, wg_ref, wu_ref, wd_ref, o_ref, acc_ref, cw_ref,
              *, final):
    e = pl.program_id(2)
    k = pl.program_id(3)

    @pl.when((e == 0) & (k == 0))
    def _():
        acc_ref[...] = jnp.zeros_like(acc_ref)
        cw_ref[...] = _top2_combine(lg_ref[0])

    h = h_ref[0]
    a = (_silu(_dot(h, wg_ref[...])) * _dot(h, wu_ref[...])).astype(BF16)
    lane = lax.broadcasted_iota(jnp.int32, cw_ref.shape, 1)
    cw = jnp.sum(jnp.where(lane == e, cw_ref[...], 0.0), axis=-1, keepdims=True)
    acc_ref[...] += cw * _dot(a, wd_ref[...])

    @pl.when((e == pl.num_programs(2) - 1) & (k == pl.num_programs(3) - 1))
    def _():
        _ffn_epilogue(x_ref, mod_ref, fn_ref, acc_ref[...], o_ref, final)


def _moe(h, x, mod, fnorm, logits, wg, wu, wd, tm, tf, final):
    b, l, d = x.shape
    ne, _, ff = wg.shape
    row = lambda n: pl.BlockSpec((1, tm, n), lambda i, j, e, k: (i, j, 0))
    return pl.pallas_call(
        functools.partial(_moe_body, final=final),
        grid=(b, l // tm, ne, ff // tf),
        in_specs=[row(d), row(d), pl.BlockSpec((1, 8, d), lambda i, j, e, k: (i, 0, 0)),
                  pl.BlockSpec((1, d), lambda i, j, e, k: (0, 0)), row(LANES),
                  pl.BlockSpec((None, d, tf), lambda i, j, e, k: (e, 0, k)),
                  pl.BlockSpec((None, d, tf), lambda i, j, e, k: (e, 0, k)),
                  pl.BlockSpec((None, tf, d), lambda i, j, e, k: (e, k, 0))],
        out_specs=row(d),
        out_shape=jax.ShapeDtypeStruct((b, l, d), F32),
        scratch_shapes=[pltpu.VMEM((tm, d), F32), pltpu.VMEM((tm, LANES), F32)],
        compiler_params=_cparams("parallel", "parallel", "arbitrary", "arbitrary"),
        name="moe",
    )(h, x, mod, fnorm, logits, wg, wu, wd)


def _row_tile(l, want):
    return want if l % want == 0 else l


def kernel(x, c, ctx, c_ctx, ada_w, ada_b, norm_mix, norm_ffn, w_in, ssd_conv_w, ssd_conv_b, ssd_a_log, ssd_dt_bias, ssd_d, ssd_norm, ssd_out, na_rpb, na_out, conf_conv_w, conf_conv_b, conf_ln_g, conf_ln_b, conf_out, w_o, ffn_gate, ffn_up, ffn_down, moe_router, moe_gate, moe_up, moe_down, final_norm):
    depth = w_in.shape[0]
    b, l, d = x.shape
    lc = ctx.shape[1]
    nh = SSD_HEADS

    cc = jnp.zeros((8, d), F32).at[:b].set(c).at[b].set(c_ctx)
    mods = _ada(cc, ada_w, ada_b)
    fnorm = final_norm.reshape(1, d)

    offs = np.cumsum((0, SSD_D_INNER, SSD_XBC, nh, nh, NA_WIDTH, NA_WIDTH, NA_WIDTH, 2 * CONF_WIDTH, 3 * d))
    seg = lambda w, i: w[:, offs[i]:offs[i + 1]]

    for layer in range(depth):
        need_ctx = layer < depth - 1
        last = layer == depth - 1
        m6 = mods[layer].reshape(8, 6, d)
        mod_l = jnp.zeros((b, 8, d), F32).at[:, :6].set(m6[:b])
        mod_c = jnp.broadcast_to(jnp.zeros((8, d), F32).at[:6].set(m6[b]), (b, 8, d))

        wl = w_in[layer]
        w_main = jnp.concatenate([seg(wl, i) for i in (0, 1, 4, 5, 6, 7)], axis=1).astype(BF16)
        w_dt = jnp.concatenate([seg(wl, 2), seg(wl, 3)], axis=1)
        w_dt_pad = jnp.zeros((d, LANES), F32).at[:, :2 * nh].set(w_dt).astype(BF16)
        w_dtt = w_dt.T.astype(BF16)
        w_gate = seg(wl, 8).astype(BF16)
        dtb = ssd_dt_bias[layer].reshape(2 * nh)
        b_lane = jnp.zeros((1, LANES), F32).at[0, :2 * nh].set(dtb)
        b_sub = dtb.reshape(2 * nh, 1)
        alog = ssd_a_log[layer].reshape(2 * nh)
        a_lane = jnp.zeros((1, LANES), F32).at[0, :2 * nh].set(alog)
        a_sub = alog.reshape(2 * nh, 1)
        dskip = jnp.repeat(ssd_d[layer], SSD_HEAD_DIM).reshape(1, SSD_D_INNER)
        gmix = norm_mix[layer].reshape(1, d)
        gffn = norm_ffn[layer].reshape(1, d)
        snorm = ssd_norm[layer].reshape(1, SSD_D_INNER)
        wssd = ssd_out[layer].astype(BF16)
        wna = na_out[layer].astype(BF16)
        wcf = conf_out[layer].astype(BF16)
        wo = w_o[layer].astype(BF16)

        def mixer_inputs(xx, mod, tm):
            return _inproj(xx, mod[:, 0:1], mod[:, 1:2], gmix, w_main, w_dt_pad, w_dtt, b_lane, b_sub, tm)

        z_c, xbc_c, q_c, k_c, v_c, glu_c, dt_c, dtt_c = mixer_inputs(ctx, mod_c, _row_tile(lc, 256))
        z_l, xbc_l, q_l, k_l, v_l, glu_l, dt_l, dtt_l = mixer_inputs(x, mod_l, _row_tile(l, 512))

        conv5 = lambda t, tl: _dwconv(t, ssd_conv_w[layer], ssd_conv_b[layer], tl, 8, 16, "silu")
        u_c = conv5(xbc_c, _row_tile(lc, 256))
        u_l = conv5(xbc_l, _row_tile(l, 256))
        h0 = jnp.zeros((b, 2, SSD_GROUPS, SSD_STATE, SSD_D_INNER // SSD_GROUPS), F32)
        yf_c, yb_c, st = _ssd(u_c, dt_c, dtt_c, a_lane, a_sub, h0)
        yf_l, yb_l, _ = _ssd(u_l, dt_l, dtt_l, a_lane, a_sub, st)

        na_l = _na_latent(q_l, k_l, v_l, k_c, v_c, na_rpb[layer])

        conv31 = lambda t, tl: _dwconv(t, conf_conv_w[layer], conf_conv_b[layer], tl, 16, 32, "ln_silu",
                                       ln=(conf_ln_g[layer], conf_ln_b[layer]), out_dtype=BF16)
        cf_l = conv31(glu_l, _row_tile(l, 256))

        is_moe = layer % 2 == 1
        w_router = None
        if is_moe:
            wr_f = jnp.zeros((d, LANES), F32).at[:, :N_EXPERTS].set(moe_router[layer // 2])
            wr_hi = wr_f.astype(BF16)
            w_router = jnp.concatenate([wr_hi, (wr_f - wr_hi.astype(F32)).astype(BF16)], axis=1)
        tail = functools.partial(_tail, gmix=gmix, gffn=gffn, dskip=dskip, snorm=snorm, wgate=w_gate,
                                 wssd=wssd, wna=wna, wcf=wcf, wo=wo)
        outs = tail(x, mod_l, yf=yf_l, yb=yb_l, u=u_l, z=z_l, na=na_l, cf=cf_l, tm=_row_tile(l, 256),
                    w_router=w_router)
        x_mid, h2_l = outs[0], outs[1]
        if need_ctx:
            na_c = _na_context(q_c, k_c, v_c)
            cf_c = conv31(glu_c, _row_tile(lc, 256))
            outs_c = tail(ctx, mod_c, yf=yf_c, yb=yb_c, u=u_c, z=z_c, na=na_c, cf=cf_c,
                          tm=_row_tile(lc, 256), w_router=w_router)
            ctx_mid, h2_c = outs_c[0], outs_c[1]

        i = layer // 2
        if not is_moe:
            wg, wu, wd = (t[i].astype(BF16) for t in (ffn_gate, ffn_up, ffn_down))
            tf = wg.shape[1] // 2
            x = _ffn(h2_l, x_mid, mod_l, fnorm, wg, wu, wd, _row_tile(l, 512), tf, last)
            if need_ctx:
                ctx = _ffn(h2_c, ctx_mid, mod_c, fnorm, wg, wu, wd, _row_tile(lc, 256), tf, False)
        else:
            wg, wu, wd = (t[i].astype(BF16) for t in (moe_gate, moe_up, moe_down))
            tf = wg.shape[2] // 2
            x = _moe(h2_l, x_mid, mod_l, fnorm, outs[2], wg, wu, wd, _row_tile(l, 512), tf, last)
            if need_ctx:
                ctx = _moe(h2_c, ctx_mid, mod_c, fnorm, outs_c[2], wg, wu, wd, _row_tile(lc, 256), tf,
                           False)
    return x
```

```python
import functools

import numpy as np
import jax
import jax.numpy as jnp
from jax import lax
from jax.experimental import pallas as pl
from jax.experimental.pallas import tpu as pltpu

F32 = jnp.float32
BF16 = jnp.bfloat16

NORM_EPS = 1e-6
NEG_INF = -1e30
GRID_W = 64

SSD_HEADS = 16
SSD_HEAD_DIM = 64
SSD_GROUPS = 2
SSD_STATE = 128
SSD_CHUNK = 128
SSD_D_INNER = SSD_HEADS * SSD_HEAD_DIM
SSD_XBC = SSD_D_INNER + 2 * SSD_GROUPS * SSD_STATE
SSD_CONV = 5

NA_HEADS = 8
NA_HEAD_DIM = 64
NA_WIDTH = NA_HEADS * NA_HEAD_DIM
NA_WIN_ROWS = 8
NA_WIN_COLS = 16

CONF_WIDTH = 512
CONF_KERNEL = 31
N_EXPERTS = 8

LANES = 128
SUBLANES = 8
VMEM_LIMIT = 56 << 20


def _cparams(*sem):
    return pltpu.CompilerParams(dimension_semantics=sem, vmem_limit_bytes=VMEM_LIMIT)


def _sigmoid(x):
    return 1.0 / (1.0 + jnp.exp(-x))


def _silu(x):
    return x * _sigmoid(x)


def _softplus(x):
    return jnp.maximum(x, 0.0) + jnp.log1p(jnp.exp(-jnp.abs(x)))


def _rms(x, g):
    return x * lax.rsqrt(jnp.mean(x * x, axis=-1, keepdims=True) + NORM_EPS) * g


def _dot(a, b):
    return jnp.dot(a, b, preferred_element_type=F32)


def _dot_nt(a, b):
    return lax.dot_general(a, b, (((1,), (1,)), ((), ())), preferred_element_type=F32)


def _split3(a):
    hi = a.astype(BF16)
    r1 = a - hi.astype(F32)
    mid = r1.astype(BF16)
    lo = (r1 - mid.astype(F32)).astype(BF16)
    return hi, mid, lo


def _const_spec(shape):
    nd = len(shape)
    return pl.BlockSpec(shape, lambda *_: (0,) * nd)


def _ada_body(c_ref, w_ref, b_ref, o_ref):
    s = _silu(c_ref[...]).astype(BF16)
    o_ref[...] = _dot(s, w_ref[...].astype(BF16)) + b_ref[...]


def _ada(cc, ada_w, ada_b):
    depth, d, n = ada_w.shape
    tn = 1536
    return pl.pallas_call(
        _ada_body,
        grid=(depth, n // tn),
        in_specs=[pl.BlockSpec((8, d), lambda l, j: (0, 0)),
                  pl.BlockSpec((None, d, tn), lambda l, j: (l, 0, j)),
                  pl.BlockSpec((None, 1, tn), lambda l, j: (l, 0, j))],
        out_specs=pl.BlockSpec((None, 8, tn), lambda l, j: (l, 0, j)),
        out_shape=jax.ShapeDtypeStruct((depth, 8, n), F32),
        compiler_params=_cparams("arbitrary", "arbitrary"),
        name="ada",
    )(cc, ada_w, ada_b.reshape(depth, 1, n))


_SEG_Z = (0, 1024)
_SEG_XBC = (1024, 2560)
_SEG_Q = (2560, 3072)
_SEG_K = (3072, 3584)
_SEG_V = (3584, 4096)
_SEG_GLU = (4096, 5120)


def _inproj_body(x_ref, sh_ref, sc_ref, g_ref, w_ref, wdt_ref, wdtt_ref, bl_ref, bs_ref,
                 z_ref, xbc_ref, q_ref, k_ref, v_ref, glu_ref, dt_ref, dtt_ref):
    h = (_rms(x_ref[0], g_ref[...]) * (1.0 + sc_ref[0]) + sh_ref[0]).astype(BF16)
    z_ref[0] = _dot(h, w_ref[:, _SEG_Z[0]:_SEG_Z[1]])
    xbc_ref[0] = _dot(h, w_ref[:, _SEG_XBC[0]:_SEG_XBC[1]])
    q_ref[0] = (_dot(h, w_ref[:, _SEG_Q[0]:_SEG_Q[1]]) * (NA_HEAD_DIM ** -0.5)).astype(BF16)
    k_ref[0] = _dot(h, w_ref[:, _SEG_K[0]:_SEG_K[1]]).astype(BF16)
    v_ref[0] = _dot(h, w_ref[:, _SEG_V[0]:_SEG_V[1]]).astype(BF16)
    r = _dot(h, w_ref[:, _SEG_GLU[0]:_SEG_GLU[1]])
    glu_ref[0] = r[:, :CONF_WIDTH] * _sigmoid(r[:, CONF_WIDTH:])
    dt_ref[0] = _softplus(_dot(h, wdt_ref[...]) + bl_ref[...])
    dtt_ref[0] = _softplus(_dot_nt(wdtt_ref[...], h) + bs_ref[...])


def _inproj(x, shift, scale, g, w_main, w_dt, w_dtt, b_lane, b_sub, tm):
    b, l, d = x.shape
    row = lambda n: pl.BlockSpec((1, tm, n), lambda i, j: (i, j, 0))
    mod = pl.BlockSpec((1, 1, d), lambda i, j: (i, 0, 0))
    return pl.pallas_call(
        _inproj_body,
        grid=(b, l // tm),
        in_specs=[row(d), mod, mod, _const_spec((1, d)), _const_spec(w_main.shape),
                  _const_spec(w_dt.shape), _const_spec(w_dtt.shape),
                  _const_spec(b_lane.shape), _const_spec(b_sub.shape)],
        out_specs=[row(1024), row(SSD_XBC), row(NA_WIDTH), row(NA_WIDTH), row(NA_WIDTH),
                   row(CONF_WIDTH), row(LANES),
                   pl.BlockSpec((1, 2 * SSD_HEADS, tm), lambda i, j: (i, 0, j))],
        out_shape=[jax.ShapeDtypeStruct((b, l, 1024), F32),
                   jax.ShapeDtypeStruct((b, l, SSD_XBC), F32),
                   jax.ShapeDtypeStruct((b, l, NA_WIDTH), BF16),
                   jax.ShapeDtypeStruct((b, l, NA_WIDTH), BF16),
                   jax.ShapeDtypeStruct((b, l, NA_WIDTH), BF16),
                   jax.ShapeDtypeStruct((b, l, CONF_WIDTH), F32),
                   jax.ShapeDtypeStruct((b, l, LANES), F32),
                   jax.ShapeDtypeStruct((b, 2 * SSD_HEADS, l), F32)],
        compiler_params=_cparams("parallel", "parallel"),
        name="inproj",
    )(x, shift, scale, g, w_main, w_dt, w_dtt, b_lane, b_sub)


def _dwconv_body(prev_ref, cur_ref, next_ref, w_ref, b_ref, *rest, taps, tl, hb, strip, mode):
    if mode == "ln_silu":
        lg_ref, lb_ref, o_ref, buf, *sh = rest
    else:
        o_ref, buf, *sh = rest
    i = pl.program_id(1)
    n = pl.num_programs(1)
    buf[0:hb, :] = jnp.where(i > 0, prev_ref[0], 0.0)
    buf[hb:hb + tl, :] = cur_ref[0]
    buf[hb + tl:hb + tl + hb, :] = jnp.where(i < n - 1, next_ref[0], 0.0)
    shifted = sh[0] if sh else None
    if shifted is not None:
        for s in range(1, SUBLANES):
            shifted[s - 1] = buf[s:s + shifted.shape[1], :]

    def window(off):
        if shifted is not None and off % SUBLANES:
            a = off - off % SUBLANES
            return shifted[off % SUBLANES - 1, a:a + strip, :]
        return buf[off:off + strip, :]

    def tap_weight(j):
        wj = w_ref[j * SUBLANES:(j + 1) * SUBLANES, :]
        return jnp.concatenate([wj] * (strip // SUBLANES), axis=0)

    half = taps // 2
    for r0 in range(0, tl, strip):
        acc = b_ref[...] + tap_weight(0) * window(r0 + hb - half)
        for j in range(1, taps):
            acc = acc + tap_weight(j) * window(r0 + hb - half + j)
        if mode == "silu":
            o_ref[0, r0:r0 + strip, :] = _silu(acc)
        else:
            mu = jnp.mean(acc, axis=-1, keepdims=True)
            xc = acc - mu
            var = jnp.mean(xc * xc, axis=-1, keepdims=True)
            y = xc * lax.rsqrt(var + NORM_EPS) * lg_ref[...] + lb_ref[...]
            o_ref[0, r0:r0 + strip, :] = _silu(y).astype(o_ref.dtype)


def _dwconv(x, w, bias, tl, hb, strip, mode, ln=None, out_dtype=F32):
    b, l, c = x.shape
    taps = w.shape[0]
    nh = tl // hb
    last = l // hb - 1
    ins = [x, x, x, jnp.repeat(w, SUBLANES, axis=0), bias.reshape(1, c)]
    specs = [pl.BlockSpec((1, hb, c), lambda i, j: (i, jnp.maximum(j * nh - 1, 0), 0)),
             pl.BlockSpec((1, tl, c), lambda i, j: (i, j, 0)),
             pl.BlockSpec((1, hb, c), lambda i, j: (i, jnp.minimum((j + 1) * nh, last), 0)),
             _const_spec((taps * SUBLANES, c)), _const_spec((1, c))]
    if mode == "ln_silu":
        ins += [ln[0].reshape(1, c), ln[1].reshape(1, c)]
        specs += [_const_spec((1, c)), _const_spec((1, c))]
    return pl.pallas_call(
        functools.partial(_dwconv_body, taps=taps, tl=tl, hb=hb, strip=strip, mode=mode),
        grid=(b, l // tl),
        in_specs=specs,
        out_specs=pl.BlockSpec((1, tl, c), lambda i, j: (i, j, 0)),
        out_shape=jax.ShapeDtypeStruct((b, l, c), out_dtype),
        scratch_shapes=[pltpu.VMEM((tl + 2 * hb, c), F32)] + (
            [pltpu.VMEM((SUBLANES - 1, tl + 2 * hb - SUBLANES, c), F32)] if taps > SUBLANES else []),
        compiler_params=_cparams("parallel", "parallel"),
        name="dwconv_" + mode,
    )(*ins)


def _ssd_dir(u_ref, dtc_ref, dtt_ref, al_ref, as_ref, st_ref, y_ref, d, reverse):
    q = SSD_CHUNK
    li = lax.broadcasted_iota(jnp.int32, (q, q), 0)
    si = lax.broadcasted_iota(jnp.int32, (q, q), 1)
    lane = lax.broadcasted_iota(jnp.int32, (q, LANES), 1)
    lo_half = lane < SSD_HEAD_DIM
    mask = (si >= li) if reverse else (si <= li)
    tri = jnp.where(mask, 1.0, 0.0).astype(BF16)
    tri_t = jnp.where((li >= si) if reverse else (li <= si), 1.0, 0.0).astype(BF16)

    a_lane = -jnp.exp(al_ref[...])
    a_sub = -jnp.exp(as_ref[d * SSD_HEADS:(d + 1) * SSD_HEADS, :])
    dtc = dtc_ref[0]
    hi, mid, lo = _split3(dtc * a_lane)
    cs = _dot(tri, hi) + _dot(tri, mid) + _dot(tri, lo)
    dtr = dtt_ref[0, d * SSD_HEADS:(d + 1) * SSD_HEADS, :] * a_sub
    hi, mid, lo = _split3(dtr)
    cs_r = _dot(hi, tri_t) + _dot(mid, tri_t) + _dot(lo, tri_t)

    hpg = SSD_HEADS // SSD_GROUPS
    gw = hpg * SSD_HEAD_DIM
    for g in range(SSD_GROUPS):
        bm = u_ref[0, :, SSD_D_INNER + g * SSD_STATE:SSD_D_INNER + (g + 1) * SSD_STATE]
        cm = u_ref[0, :, SSD_D_INNER + (SSD_GROUPS + g) * SSD_STATE:
                   SSD_D_INNER + (SSD_GROUPS + g + 1) * SSD_STATE].astype(BF16)
        bm_t = bm.T.astype(BF16)
        scores = _dot(cm, bm_t)
        st = st_ref[d, g]
        y_off = _dot(cm, st.astype(BF16))
        xdd = []
        tots = []
        for pp in range(hpg // 2):
            p = g * (hpg // 2) + pp
            c0 = 16 * d + 2 * p
            bc0 = jnp.broadcast_to(cs[:, c0:c0 + 1], (q, LANES))
            bc1 = jnp.broadcast_to(cs[:, c0 + 1:c0 + 2], (q, LANES))
            csx = jnp.where(lo_half, bc0, bc1)
            dtx = jnp.where(lo_half, jnp.broadcast_to(dtc[:, c0:c0 + 1], (q, LANES)),
                            jnp.broadcast_to(dtc[:, c0 + 1:c0 + 2], (q, LANES)))
            xd = u_ref[0, :, p * LANES:(p + 1) * LANES] * dtx
            xd_b = xd.astype(BF16)
            dec0 = jnp.exp(jnp.where(mask, bc0 - cs_r[2 * p:2 * p + 1, :], -jnp.inf))
            dec1 = jnp.exp(jnp.where(mask, bc1 - cs_r[2 * p + 1:2 * p + 2, :], -jnp.inf))
            y0 = _dot((scores * dec0).astype(BF16), xd_b)
            y1 = _dot((scores * dec1).astype(BF16), xd_b)
            y = jnp.where(lo_half, y0, y1) + y_off[:, pp * LANES:(pp + 1) * LANES] * jnp.exp(csx)
            y_ref[0, :, p * LANES:(p + 1) * LANES] = y
            totx = csx[0:1, :] if reverse else csx[q - 1:q, :]
            xdd.append((xd * jnp.exp(totx - csx)).astype(BF16))
            tots.append(totx)
        new_states = _dot(bm_t, jnp.concatenate(xdd, axis=1))
        st_ref[d, g] = st * jnp.exp(jnp.concatenate(tots, axis=1)) + new_states


def _ssd_body(uf_ref, ub_ref, dcf_ref, dcb_ref, dtf_ref, dtb_ref, al_ref, as_ref, h0_ref,
              yf_ref, yb_ref, ht_ref, st_ref):
    i = pl.program_id(1)

    @pl.when(i == 0)
    def _():
        st_ref[...] = h0_ref[0]

    _ssd_dir(uf_ref, dcf_ref, dtf_ref, al_ref, as_ref, st_ref, yf_ref, 0, False)
    _ssd_dir(ub_ref, dcb_ref, dtb_ref, al_ref, as_ref, st_ref, yb_ref, 1, True)

    @pl.when(i == pl.num_programs(1) - 1)
    def _():
        ht_ref[0] = st_ref[...]


def _ssd(u, dt, dtt, a_lane, a_sub, h0):
    b, l, _ = u.shape
    q = SSD_CHUNK
    nc = l // q
    fwd = lambda n: pl.BlockSpec((1, q, n), lambda i, j: (i, j, 0))
    bwd = lambda n: pl.BlockSpec((1, q, n), lambda i, j: (i, nc - 1 - j, 0))
    st_shape = (2, SSD_GROUPS, SSD_STATE, SSD_D_INNER // SSD_GROUPS)
    st_spec = pl.BlockSpec((1,) + st_shape, lambda i, j: (i, 0, 0, 0, 0))
    return pl.pallas_call(
        _ssd_body,
        grid=(b, nc),
        in_specs=[fwd(SSD_XBC), bwd(SSD_XBC), fwd(LANES), bwd(LANES),
                  pl.BlockSpec((1, 2 * SSD_HEADS, q), lambda i, j: (i, 0, j)),
                  pl.BlockSpec((1, 2 * SSD_HEADS, q), lambda i, j: (i, 0, nc - 1 - j)),
                  _const_spec((1, LANES)), _const_spec((2 * SSD_HEADS, 1)), st_spec],
        out_specs=[fwd(SSD_D_INNER), bwd(SSD_D_INNER), st_spec],
        out_shape=[jax.ShapeDtypeStruct((b, l, SSD_D_INNER), F32),
                   jax.ShapeDtypeStruct((b, l, SSD_D_INNER), F32),
                   jax.ShapeDtypeStruct((b,) + st_shape, F32)],
        scratch_shapes=[pltpu.VMEM(st_shape, F32)],
        compiler_params=_cparams("parallel", "arbitrary"),
        name="ssd",
    )(u, u, dt, dt, dtt, dtt, a_lane, a_sub, h0)


def _na_body(*refs, window, n_win):
    if window:
        q_ref, k_ref, v_ref, kc_ref, vc_ref, bias_ref, o_ref = refs
    else:
        q_ref, kc_ref, vc_ref, o_ref = refs
    tq = q_ref.shape[1]
    lane = lax.broadcasted_iota(jnp.int32, (tq, LANES), 1)
    lo_half = lane < NA_HEAD_DIM
    if window:
        r = pl.program_id(1)
        rows = pl.num_programs(1)
        wr = n_win // GRID_W
        rs = jnp.clip(r - wr // 2, 0, rows - wr)
        start = pl.multiple_of(rs * GRID_W, GRID_W)
    scores = []
    for h in range(NA_HEADS):
        sl = slice((h // 2) * LANES, (h // 2 + 1) * LANES)
        q2 = q_ref[0, :, sl]
        qm = jnp.where(lo_half if h % 2 == 0 else jnp.logical_not(lo_half), q2, jnp.zeros_like(q2))
        s = _dot_nt(qm, kc_ref[0, :, sl])
        if window:
            s_w = _dot_nt(qm, k_ref[0, pl.ds(start, n_win), sl]) + bias_ref[0, h]
            s = jnp.concatenate([s_w, s], axis=1)
        scores.append(s)
    s_all = jnp.concatenate(scores, axis=0)
    p_all = jnp.exp(s_all - jnp.max(s_all, axis=-1, keepdims=True))
    inv = 1.0 / jnp.sum(p_all, axis=-1, keepdims=True)
    p_all = p_all.astype(BF16)
    for hp in range(NA_HEADS // 2):
        sl = slice(hp * LANES, (hp + 1) * LANES)
        outs = []
        for par in range(2):
            h = 2 * hp + par
            p = p_all[h * tq:(h + 1) * tq]
            o = _dot(p[:, n_win:], vc_ref[0, :, sl])
            if window:
                o = o + _dot(p[:, :n_win], v_ref[0, pl.ds(start, n_win), sl])
            outs.append(o * inv[h * tq:(h + 1) * tq])
        o_ref[0, :, sl] = jnp.where(lo_half, outs[0], outs[1]).astype(o_ref.dtype)


def _na_bias_table(rpb, rows):
    wr = min(NA_WIN_ROWS, rows)
    r_all = np.arange(rows)
    rs_all = np.clip(r_all - wr // 2, 0, rows - wr)
    drow_all = rs_all[:, None] + np.arange(wr)[None, :] - r_all[:, None] + NA_WIN_ROWS - 1
    uniq, var_of_row = np.unique(drow_all, axis=0, return_inverse=True)
    qcol = np.arange(GRID_W)
    kcol = np.arange(GRID_W)
    col_start = np.clip(qcol - NA_WIN_COLS // 2, 0, GRID_W - NA_WIN_COLS)
    rel = kcol[None, :] - col_start[:, None]
    col_ok = (rel >= 0) & (rel < NA_WIN_COLS)
    dcol = np.clip(kcol[None, :] - qcol[:, None] + NA_WIN_COLS - 1, 0, 2 * NA_WIN_COLS - 2)
    n_dcol = 2 * NA_WIN_COLS - 1
    onehot = jnp.asarray(dcol[None, :, :] == np.arange(n_dcol)[:, None, None], F32)
    sel = rpb.astype(F32)[:, uniq.reshape(-1), :]
    tab = jnp.einsum("hrc,cqk->hqrk", sel, onehot, precision=lax.Precision.HIGHEST)
    tab = tab.reshape(NA_HEADS, GRID_W, uniq.shape[0], wr, GRID_W)
    tab = jnp.where(col_ok[None, :, None, None, :], tab, NEG_INF)
    tab = jnp.transpose(tab, (2, 0, 1, 3, 4)).reshape(uniq.shape[0], NA_HEADS, GRID_W, wr * GRID_W)
    return tab, jnp.asarray(var_of_row.reshape(-1), jnp.int32), wr


def _na_latent(q, k, v, kc, vc, rpb):
    b, l, w = q.shape
    rows = l // GRID_W
    tab, var_of_row, wr = _na_bias_table(rpb, rows)
    n_win = wr * GRID_W
    lc = kc.shape[1]
    full = lambda n: pl.BlockSpec((1, n, w), lambda i, r, vr: (i, 0, 0))
    grid_spec = pltpu.PrefetchScalarGridSpec(
        num_scalar_prefetch=1,
        grid=(b, rows),
        in_specs=[pl.BlockSpec((1, GRID_W, w), lambda i, r, vr: (i, r, 0)),
                  full(l), full(l), full(lc), full(lc),
                  pl.BlockSpec((1, NA_HEADS, GRID_W, n_win), lambda i, r, vr: (vr[r], 0, 0, 0))],
        out_specs=pl.BlockSpec((1, GRID_W, w), lambda i, r, vr: (i, r, 0)),
    )

    def body(vr_ref, *refs):
        _na_body(*refs, window=True, n_win=n_win)

    return pl.pallas_call(
        body,
        grid_spec=grid_spec,
        out_shape=jax.ShapeDtypeStruct((b, l, w), BF16),
        compiler_params=_cparams("parallel", "arbitrary"),
        name="na_latent",
    )(var_of_row, q, k, v, kc, vc, tab)


def _na_context(q, kc, vc):
    b, lc, w = q.shape
    tq = GRID_W
    full = pl.BlockSpec((1, lc, w), lambda i, j: (i, 0, 0))
    return pl.pallas_call(
        functools.partial(_na_body, window=False, n_win=0),
        grid=(b, lc // tq),
        in_specs=[pl.BlockSpec((1, tq, w), lambda i, j: (i, j, 0)), full, full],
        out_specs=pl.BlockSpec((1, tq, w), lambda i, j: (i, j, 0)),
        out_shape=jax.ShapeDtypeStruct((b, lc, w), BF16),
        compiler_params=_cparams("parallel", "arbitrary"),
        name="na_context",
    )(q, kc, vc)


def _tail_body(x_ref, mod_ref, gmix_ref, gffn_ref, yf_ref, yb_ref, xs_ref, z_ref, na_ref, cf_ref,
               dskip_ref, snorm_ref, wgate_ref, wssd_ref, wna_ref, wcf_ref, wo_ref, *rest, router):
    if router:
        wr_ref, xo_ref, h2_ref, lg_ref = rest
    else:
        xo_ref, h2_ref = rest
    d = x_ref.shape[2]
    x = x_ref[0]
    mod = mod_ref[0]
    h = (_rms(x, gmix_ref[...]) * (1.0 + mod[1:2]) + mod[0:1]).astype(BF16)
    y = (yf_ref[0] + yb_ref[0] + dskip_ref[...] * xs_ref[0]) * _silu(z_ref[0])
    ssd = _dot(_rms(y, snorm_ref[...]).astype(BF16), wssd_ref[...])
    merged = _sigmoid(_dot(h, wgate_ref[:, 0:d])) * ssd
    merged = merged + _sigmoid(_dot(h, wgate_ref[:, d:2 * d])) * _dot(na_ref[0], wna_ref[...])
    merged = merged + _sigmoid(_dot(h, wgate_ref[:, 2 * d:3 * d])) * _dot(cf_ref[0], wcf_ref[...])
    xn = x + mod[2:3] * _dot(merged.astype(BF16), wo_ref[...])
    xo_ref[0] = xn
    h2 = _rms(xn, gffn_ref[...]) * (1.0 + mod[4:5]) + mod[3:4]
    h2_ref[0] = h2.astype(h2_ref.dtype)
    if router:
        hi = h2.astype(BF16)
        mid = (h2 - hi.astype(F32)).astype(BF16)
        r = _dot(hi, wr_ref[...])
        lg_ref[0] = _top2_route(r[:, :LANES] + r[:, LANES:] + _dot(mid, wr_ref[:, :LANES]))


def _tail(x, mod, gmix, gffn, yf, yb, u, z, na, cf, dskip, snorm, wgate, wssd, wna, wcf, wo, tm,
          w_router=None):
    b, l, d = x.shape
    row = lambda n: pl.BlockSpec((1, tm, n), lambda i, j: (i, j, 0))
    ins = [x, mod, gmix, gffn, yf, yb, u, z, na, cf, dskip, snorm, wgate, wssd, wna, wcf, wo]
    specs = [row(d), pl.BlockSpec((1, 8, d), lambda i, j: (i, 0, 0)), _const_spec((1, d)),
             _const_spec((1, d)), row(d), row(d), row(SSD_D_INNER), row(d), row(NA_WIDTH),
             row(CONF_WIDTH), _const_spec((1, d)), _const_spec((1, d)), _const_spec(wgate.shape),
             _const_spec(wssd.shape), _const_spec(wna.shape), _const_spec(wcf.shape),
             _const_spec(wo.shape)]
    out_specs = [row(d), row(d)]
    h2_dtype = BF16 if w_router is None else F32
    out_shape = [jax.ShapeDtypeStruct((b, l, d), F32), jax.ShapeDtypeStruct((b, l, d), h2_dtype)]
    if w_router is not None:
        ins.append(w_router)
        specs.append(_const_spec(w_router.shape))
        out_specs.append(row(LANES))
        out_shape.append(jax.ShapeDtypeStruct((b, l, LANES), F32))
    return pl.pallas_call(
        functools.partial(_tail_body, router=w_router is not None),
        grid=(b, l // tm),
        in_specs=specs,
        out_specs=out_specs,
        out_shape=out_shape,
        compiler_params=_cparams("parallel", "parallel"),
        name="tail",
    )(*ins)


def _ffn_epilogue(x_ref, mod_ref, fn_ref, acc, o_ref, final):
    xn = x_ref[0] + mod_ref[0][5:6] * acc
    if final:
        xn = _rms(xn, fn_ref[...])
    o_ref[0] = xn


def _ffn_body(h_ref, x_ref, mod_ref, fn_ref, wg_ref, wu_ref, wd_ref, o_ref, acc_ref, *, final):
    k = pl.program_id(2)

    @pl.when(k == 0)
    def _():
        acc_ref[...] = jnp.zeros_like(acc_ref)

    h = h_ref[0]
    a = (_silu(_dot(h, wg_ref[...])) * _dot(h, wu_ref[...])).astype(BF16)
    acc_ref[...] += _dot(a, wd_ref[...])

    @pl.when(k == pl.num_programs(2) - 1)
    def _():
        _ffn_epilogue(x_ref, mod_ref, fn_ref, acc_ref[...], o_ref, final)


def _ffn(h, x, mod, fnorm, wg, wu, wd, tm, tf, final):
    b, l, d = x.shape
    ff = wg.shape[1]
    row = pl.BlockSpec((1, tm, d), lambda i, j, k: (i, j, 0))
    return pl.pallas_call(
        functools.partial(_ffn_body, final=final),
        grid=(b, l // tm, ff // tf),
        in_specs=[row, row, pl.BlockSpec((1, 8, d), lambda i, j, k: (i, 0, 0)),
                  pl.BlockSpec((1, d), lambda i, j, k: (0, 0)),
                  pl.BlockSpec((d, tf), lambda i, j, k: (0, k)),
                  pl.BlockSpec((d, tf), lambda i, j, k: (0, k)),
                  pl.BlockSpec((tf, d), lambda i, j, k: (k, 0))],
        out_specs=row,
        out_shape=jax.ShapeDtypeStruct((b, l, d), F32),
        scratch_shapes=[pltpu.VMEM((tm, d), F32)],
        compiler_params=_cparams("parallel", "parallel", "arbitrary"),
        name="ffn",
    )(h, x, mod, fnorm, wg, wu, wd)


def _top2_route(logits):
    lane = lax.broadcasted_iota(jnp.int32, logits.shape, 1)
    lg = jnp.where(lane < N_EXPERTS, logits, -jnp.inf)
    v1 = jnp.max(lg, axis=-1, keepdims=True)
    i1 = jnp.min(jnp.where(lg == v1, lane, LANES), axis=-1, keepdims=True)
    lg2 = jnp.where(lane == i1, -jnp.inf, lg)
    v2 = jnp.max(lg2, axis=-1, keepdims=True)
    i2 = jnp.min(jnp.where(lg2 == v2, lane, LANES), axis=-1, keepdims=True)
    e2 = jnp.exp(v2 - v1)
    w1 = 1.0 / (1.0 + e2)
    w2 = e2 / (1.0 + e2)
    rec = jnp.where(lane == 0, i1.astype(F32), 0.0) + jnp.where(lane == 1, i2.astype(F32), 0.0)
    return rec + jnp.where(lane == 2, w1, 0.0) + jnp.where(lane == 3, w2, 0.0)


def _row_gather_start(ids_ref, k, src_hbm, dst, sem, n):
    def body(r, carry):
        pltpu.make_async_copy(src_hbm.at[pl.ds(ids_ref[0, k, r], 1), :], dst.at[pl.ds(r, 1), :], sem).start()
        return carry
    lax.fori_loop(0, n, body, 0)


def _gmm_body(be_ref, nu_ref, ids_ref, idsn_ref, h_hbm, wg_ref, wu_ref, wd_ref, y_ref, buf, sem,
              *, tm, n_chunk):
    j = pl.program_id(0)
    n_used = nu_ref[0]
    slot = j % 2

    def wait(s):
        pltpu.make_async_copy(h_hbm.at[pl.ds(0, tm), :], buf.at[s], sem.at[s]).wait()

    @pl.when(j == 0)
    def _():
        _row_gather_start(ids_ref, 0, h_hbm, buf.at[0], sem.at[0], tm)

    @pl.when(j < n_used)
    def _():
        wait(slot)

        @pl.when(j + 1 < n_used)
        def _():
            _row_gather_start(idsn_ref, 0, h_hbm, buf.at[1 - slot], sem.at[1 - slot], tm)

        h = buf[slot].astype(BF16)
        ff = wg_ref.shape[1]
        tf = ff // n_chunk
        acc = None
        for c in range(n_chunk):
            cs = slice(c * tf, (c + 1) * tf)
            a = (_silu(_dot(h, wg_ref[:, cs])) * _dot(h, wu_ref[:, cs])).astype(BF16)
            part = _dot(a, wd_ref[cs, :])
            acc = part if acc is None else acc + part
        y_ref[...] = acc

    @pl.when(j >= n_used)
    def _():
        y_ref[...] = jnp.zeros_like(y_ref)


def _gmm(h_flat, block_expert, n_used, src_tok, wg, wu, wd, tm):
    t, d = h_flat.shape
    ne, _, ff = wg.shape
    n_blocks = src_tok.shape[0]
    ids = lambda shift: pl.BlockSpec((1, 1, tm), lambda j, be, nu: (jnp.minimum(j + shift, n_blocks - 1), 0, 0),
                                     memory_space=pltpu.SMEM)
    wspec = lambda r, c: pl.BlockSpec((None, r, c), lambda j, be, nu: (be[j], 0, 0))
    grid_spec = pltpu.PrefetchScalarGridSpec(
        num_scalar_prefetch=2,
        grid=(n_blocks,),
        in_specs=[ids(0), ids(1), pl.BlockSpec(memory_space=pl.ANY),
                  wspec(d, ff), wspec(d, ff), wspec(ff, d)],
        out_specs=pl.BlockSpec((tm, d), lambda j, be, nu: (j, 0)),
        scratch_shapes=[pltpu.VMEM((2, tm, d), F32), pltpu.SemaphoreType.DMA((2,))],
    )
    return pl.pallas_call(
        functools.partial(_gmm_body, tm=tm, n_chunk=4),
        grid_spec=grid_spec,
        out_shape=jax.ShapeDtypeStruct((n_blocks * tm, d), F32),
        compiler_params=_cparams("arbitrary"),
        name="moe_gmm",
    )(block_expert, n_used, src_tok, src_tok, h_flat, wg, wu, wd)


def _combine_body(pos_ref, posn_ref, rt_ref, x_ref, mod_ref, fn_ref, y_hbm, o_ref, buf, sem, *, tc, final):
    nl = pl.num_programs(1)
    j = pl.program_id(0) * nl + pl.program_id(1)
    n = pl.num_programs(0) * nl
    slot = j % 2

    def start(ids_ref, s):
        for k in range(2):
            _row_gather_start(ids_ref, k, y_hbm, buf.at[s, pl.ds(k * tc, tc), :], sem.at[s], tc)

    @pl.when(j == 0)
    def _():
        start(pos_ref, 0)

    pltpu.make_async_copy(y_hbm.at[pl.ds(0, 2 * tc), :], buf.at[slot], sem.at[slot]).wait()

    @pl.when(j + 1 < n)
    def _():
        start(posn_ref, 1 - slot)

    rt = rt_ref[0]
    f = rt[:, 2:3] * buf[slot, 0:tc, :] + rt[:, 3:4] * buf[slot, tc:2 * tc, :]
    _ffn_epilogue(x_ref, mod_ref, fn_ref, f, o_ref, final)


def _combine(route, x, mod, fnorm, pos, y_sorted, tc, final):
    b, l, d = x.shape
    nl = l // tc
    n = b * nl
    ids = lambda shift: pl.BlockSpec((1, 2, tc), lambda i, j: (jnp.minimum(i * nl + j + shift, n - 1), 0, 0),
                                     memory_space=pltpu.SMEM)
    row = lambda w: pl.BlockSpec((1, tc, w), lambda i, j: (i, j, 0))
    return pl.pallas_call(
        functools.partial(_combine_body, tc=tc, final=final),
        grid=(b, nl),
        in_specs=[ids(0), ids(1), row(LANES), row(d), pl.BlockSpec((1, 8, d), lambda i, j: (i, 0, 0)),
                  _const_spec((1, d)), pl.BlockSpec(memory_space=pl.ANY)],
        out_specs=row(d),
        out_shape=jax.ShapeDtypeStruct((b, l, d), F32),
        scratch_shapes=[pltpu.VMEM((2, 2 * tc, d), F32), pltpu.SemaphoreType.DMA((2,))],
        compiler_params=_cparams("arbitrary", "arbitrary"),
        name="moe_combine",
    )(pos, pos, route, x, mod, fnorm, y_sorted)


def _moe_sparse(h2, route, x, mod, fnorm, wg, wu, wd, tm, tc, final):
    b, l, d = x.shape
    t = b * l
    ne = wg.shape[0]
    rt = route.reshape(t, LANES)
    experts = jnp.concatenate([rt[:, 0], rt[:, 1]]).astype(jnp.int32)
    onehot = (experts[:, None] == jnp.arange(ne, dtype=jnp.int32)[None, :]).astype(jnp.int32)
    csum = jnp.cumsum(onehot, axis=0)
    rank = jnp.sum((csum - onehot) * onehot, axis=1)
    counts = csum[-1]
    padded = ((counts + tm - 1) // tm) * tm
    ends = jnp.cumsum(padded)
    pos = jnp.sum(onehot * (ends - padded)[None, :], axis=1) + rank
    n_blocks = (2 * t) // tm + ne
    tok = jnp.tile(jnp.arange(t, dtype=jnp.int32), 2)
    src_tok = jnp.zeros((n_blocks * tm,), jnp.int32).at[pos].set(tok).reshape(n_blocks, 1, tm)
    block_start = jnp.arange(n_blocks, dtype=jnp.int32) * tm
    block_expert = jnp.minimum(jnp.sum(block_start[:, None] >= ends[None, :], axis=1), ne - 1).astype(jnp.int32)
    n_used = (ends[-1:] // tm).astype(jnp.int32)
    y_sorted = _gmm(h2.reshape(t, d), block_expert, n_used, src_tok, wg, wu, wd, tm)
    pos2 = jnp.transpose(pos.astype(jnp.int32).reshape(2, t // tc, tc), (1, 0, 2))
    return _combine(route, x, mod, fnorm, pos2, y_sorted, tc, final)


def _row_tile(l, want):
    return want if l % want == 0 else l


def kernel(x, c, ctx, c_ctx, ada_w, ada_b, norm_mix, norm_ffn, w_in, ssd_conv_w, ssd_conv_b, ssd_a_log, ssd_dt_bias, ssd_d, ssd_norm, ssd_out, na_rpb, na_out, conf_conv_w, conf_conv_b, conf_ln_g, conf_ln_b, conf_out, w_o, ffn_gate, ffn_up, ffn_down, moe_router, moe_gate, moe_up, moe_down, final_norm):
    depth = w_in.shape[0]
    b, l, d = x.shape
    lc = ctx.shape[1]
    nh = SSD_HEADS

    cc = jnp.zeros((8, d), F32).at[:b].set(c).at[b].set(c_ctx)
    mods = _ada(cc, ada_w, ada_b)
    fnorm = final_norm.reshape(1, d)

    offs = np.cumsum((0, SSD_D_INNER, SSD_XBC, nh, nh, NA_WIDTH, NA_WIDTH, NA_WIDTH, 2 * CONF_WIDTH, 3 * d))
    seg = lambda w, i: w[:, offs[i]:offs[i + 1]]

    for layer in range(depth):
        need_ctx = layer < depth - 1
        last = layer == depth - 1
        m6 = mods[layer].reshape(8, 6, d)
        mod_l = jnp.zeros((b, 8, d), F32).at[:, :6].set(m6[:b])
        mod_c = jnp.broadcast_to(jnp.zeros((8, d), F32).at[:6].set(m6[b]), (b, 8, d))

        wl = w_in[layer]
        w_main = jnp.concatenate([seg(wl, i) for i in (0, 1, 4, 5, 6, 7)], axis=1).astype(BF16)
        w_dt = jnp.concatenate([seg(wl, 2), seg(wl, 3)], axis=1)
        w_dt_pad = jnp.zeros((d, LANES), F32).at[:, :2 * nh].set(w_dt).astype(BF16)
        w_dtt = w_dt.T.astype(BF16)
        w_gate = seg(wl, 8).astype(BF16)
        dtb = ssd_dt_bias[layer].reshape(2 * nh)
        b_lane = jnp.zeros((1, LANES), F32).at[0, :2 * nh].set(dtb)
        b_sub = dtb.reshape(2 * nh, 1)
        alog = ssd_a_log[layer].reshape(2 * nh)
        a_lane = jnp.zeros((1, LANES), F32).at[0, :2 * nh].set(alog)
        a_sub = alog.reshape(2 * nh, 1)
        dskip = jnp.repeat(ssd_d[layer], SSD_HEAD_DIM).reshape(1, SSD_D_INNER)
        gmix = norm_mix[layer].reshape(1, d)
        gffn = norm_ffn[layer].reshape(1, d)
        snorm = ssd_norm[layer].reshape(1, SSD_D_INNER)
        wssd = ssd_out[layer].astype(BF16)
        wna = na_out[layer].astype(BF16)
        wcf = conf_out[layer].astype(BF16)
        wo = w_o[layer].astype(BF16)

        def mixer_inputs(xx, mod, tm):
            return _inproj(xx, mod[:, 0:1], mod[:, 1:2], gmix, w_main, w_dt_pad, w_dtt, b_lane, b_sub, tm)

        z_c, xbc_c, q_c, k_c, v_c, glu_c, dt_c, dtt_c = mixer_inputs(ctx, mod_c, _row_tile(lc, 256))
        z_l, xbc_l, q_l, k_l, v_l, glu_l, dt_l, dtt_l = mixer_inputs(x, mod_l, _row_tile(l, 512))

        conv5 = lambda t, tl: _dwconv(t, ssd_conv_w[layer], ssd_conv_b[layer], tl, 8, 16, "silu")
        u_c = conv5(xbc_c, _row_tile(lc, 256))
        u_l = conv5(xbc_l, _row_tile(l, 256))
        h0 = jnp.zeros((b, 2, SSD_GROUPS, SSD_STATE, SSD_D_INNER // SSD_GROUPS), F32)
        yf_c, yb_c, st = _ssd(u_c, dt_c, dtt_c, a_lane, a_sub, h0)
        yf_l, yb_l, _ = _ssd(u_l, dt_l, dtt_l, a_lane, a_sub, st)

        na_l = _na_latent(q_l, k_l, v_l, k_c, v_c, na_rpb[layer])

        conv31 = lambda t, tl: _dwconv(t, conf_conv_w[layer], conf_conv_b[layer], tl, 16, 32, "ln_silu",
                                       ln=(conf_ln_g[layer], conf_ln_b[layer]), out_dtype=BF16)
        cf_l = conv31(glu_l, _row_tile(l, 256))

        is_moe = layer % 2 == 1
        w_router = None
        if is_moe:
            wr_f = jnp.zeros((d, LANES), F32).at[:, :N_EXPERTS].set(moe_router[layer // 2])
            wr_hi = wr_f.astype(BF16)
            w_router = jnp.concatenate([wr_hi, (wr_f - wr_hi.astype(F32)).astype(BF16)], axis=1)
        tail = functools.partial(_tail, gmix=gmix, gffn=gffn, dskip=dskip, snorm=snorm, wgate=w_gate,
                                 wssd=wssd, wna=wna, wcf=wcf, wo=wo)
        outs = tail(x, mod_l, yf=yf_l, yb=yb_l, u=u_l, z=z_l, na=na_l, cf=cf_l, tm=_row_tile(l, 256),
                    w_router=w_router)
        x_mid, h2_l = outs[0], outs[1]
        if need_ctx:
            na_c = _na_context(q_c, k_c, v_c)
            cf_c = conv31(glu_c, _row_tile(lc, 256))
            outs_c = tail(ctx, mod_c, yf=yf_c, yb=yb_c, u=u_c, z=z_c, na=na_c, cf=cf_c,
                          tm=_row_tile(lc, 256), w_router=w_router)
            ctx_mid, h2_c = outs_c[0], outs_c[1]

        i = layer // 2
        if not is_moe:
            wg, wu, wd = (t[i].astype(BF16) for t in (ffn_gate, ffn_up, ffn_down))
            tf = wg.shape[1] // 2
            x = _ffn(h2_l, x_mid, mod_l, fnorm, wg, wu, wd, _row_tile(l, 512), tf, last)
            if need_ctx:
                ctx = _ffn(h2_c, ctx_mid, mod_c, fnorm, wg, wu, wd, _row_tile(lc, 256), tf, False)
        else:
            wg, wu, wd = (t[i].astype(BF16) for t in (moe_gate, moe_up, moe_down))
            x = _moe_sparse(h2_l, outs[2], x_mid, mod_l, fnorm, wg, wu, wd, 512, _row_tile(l, 256), last)
            if need_ctx:
                ctx = _moe_sparse(h2_c, outs_c[2], ctx_mid, mod_c, fnorm, wg, wu, wd, 512,
                                  _row_tile(lc, 256), False)
    return x
```

```python
import functools

import numpy as np
import jax
import jax.numpy as jnp
from jax import lax
from jax.experimental import pallas as pl
from jax.experimental.pallas import tpu as pltpu

F32 = jnp.float32
BF16 = jnp.bfloat16

NORM_EPS = 1e-6
NEG_INF = -1e30
GRID_W = 64

SSD_HEADS = 16
SSD_HEAD_DIM = 64
SSD_GROUPS = 2
SSD_STATE = 128
SSD_CHUNK = 128
SSD_D_INNER = SSD_HEADS * SSD_HEAD_DIM
SSD_XBC = SSD_D_INNER + 2 * SSD_GROUPS * SSD_STATE
SSD_CONV = 5

NA_HEADS = 8
NA_HEAD_DIM = 64
NA_WIDTH = NA_HEADS * NA_HEAD_DIM
NA_WIN_ROWS = 8
NA_WIN_COLS = 16

CONF_WIDTH = 512
CONF_KERNEL = 31
N_EXPERTS = 8

LANES = 128
SUBLANES = 8
VMEM_LIMIT = 56 << 20


def _cparams(*sem):
    return pltpu.CompilerParams(dimension_semantics=sem, vmem_limit_bytes=VMEM_LIMIT)


def _sigmoid(x):
    return 1.0 / (1.0 + jnp.exp(-x))


def _silu(x):
    return x * _sigmoid(x)


def _softplus(x):
    return jnp.maximum(x, 0.0) + jnp.log1p(jnp.exp(-jnp.abs(x)))


def _rms(x, g):
    return x * lax.rsqrt(jnp.mean(x * x, axis=-1, keepdims=True) + NORM_EPS) * g


def _dot(a, b):
    return jnp.dot(a, b, preferred_element_type=F32)


def _dot_nt(a, b):
    return lax.dot_general(a, b, (((1,), (1,)), ((), ())), preferred_element_type=F32)


def _split3(a):
    hi = a.astype(BF16)
    r1 = a - hi.astype(F32)
    mid = r1.astype(BF16)
    lo = (r1 - mid.astype(F32)).astype(BF16)
    return hi, mid, lo


def _const_spec(shape):
    nd = len(shape)
    return pl.BlockSpec(shape, lambda *_: (0,) * nd)


def _ada_body(c_ref, w_ref, b_ref, o_ref):
    s = _silu(c_ref[...]).astype(BF16)
    o_ref[...] = _dot(s, w_ref[...].astype(BF16)) + b_ref[...]


def _ada(cc, ada_w, ada_b):
    depth, d, n = ada_w.shape
    tn = 1536
    return pl.pallas_call(
        _ada_body,
        grid=(depth, n // tn),
        in_specs=[pl.BlockSpec((8, d), lambda l, j: (0, 0)),
                  pl.BlockSpec((None, d, tn), lambda l, j: (l, 0, j)),
                  pl.BlockSpec((None, 1, tn), lambda l, j: (l, 0, j))],
        out_specs=pl.BlockSpec((None, 8, tn), lambda l, j: (l, 0, j)),
        out_shape=jax.ShapeDtypeStruct((depth, 8, n), F32),
        compiler_params=_cparams("arbitrary", "arbitrary"),
        name="ada",
    )(cc, ada_w, ada_b.reshape(depth, 1, n))


_SEG_Z = (0, 1024)
_SEG_XBC = (1024, 2560)
_SEG_Q = (2560, 3072)
_SEG_K = (3072, 3584)
_SEG_V = (3584, 4096)
_SEG_GLU = (4096, 5120)


def _inproj_body(x_ref, sh_ref, sc_ref, g_ref, w_ref, wdt_ref, wdtt_ref, bl_ref, bs_ref,
                 z_ref, xbc_ref, q_ref, k_ref, v_ref, glu_ref, dt_ref, dtt_ref):
    h = (_rms(x_ref[0], g_ref[...]) * (1.0 + sc_ref[0]) + sh_ref[0]).astype(BF16)
    z_ref[0] = _dot(h, w_ref[:, _SEG_Z[0]:_SEG_Z[1]])
    xbc_ref[0] = _dot(h, w_ref[:, _SEG_XBC[0]:_SEG_XBC[1]])
    q_ref[0] = (_dot(h, w_ref[:, _SEG_Q[0]:_SEG_Q[1]]) * (NA_HEAD_DIM ** -0.5)).astype(BF16)
    k_ref[0] = _dot(h, w_ref[:, _SEG_K[0]:_SEG_K[1]]).astype(BF16)
    v_ref[0] = _dot(h, w_ref[:, _SEG_V[0]:_SEG_V[1]]).astype(BF16)
    r = _dot(h, w_ref[:, _SEG_GLU[0]:_SEG_GLU[1]])
    glu_ref[0] = r[:, :CONF_WIDTH] * _sigmoid(r[:, CONF_WIDTH:])
    dt_ref[0] = _softplus(_dot(h, wdt_ref[...]) + bl_ref[...])
    dtt_ref[0] = _softplus(_dot_nt(wdtt_ref[...], h) + bs_ref[...])


def _inproj(x, shift, scale, g, w_main, w_dt, w_dtt, b_lane, b_sub, tm):
    b, l, d = x.shape
    row = lambda n: pl.BlockSpec((1, tm, n), lambda i, j: (i, j, 0))
    mod = pl.BlockSpec((1, 1, d), lambda i, j: (i, 0, 0))
    return pl.pallas_call(
        _inproj_body,
        grid=(b, l // tm),
        in_specs=[row(d), mod, mod, _const_spec((1, d)), _const_spec(w_main.shape),
                  _const_spec(w_dt.shape), _const_spec(w_dtt.shape),
                  _const_spec(b_lane.shape), _const_spec(b_sub.shape)],
        out_specs=[row(1024), row(SSD_XBC), row(NA_WIDTH), row(NA_WIDTH), row(NA_WIDTH),
                   row(CONF_WIDTH), row(LANES),
                   pl.BlockSpec((1, 2 * SSD_HEADS, tm), lambda i, j: (i, 0, j))],
        out_shape=[jax.ShapeDtypeStruct((b, l, 1024), F32),
                   jax.ShapeDtypeStruct((b, l, SSD_XBC), F32),
                   jax.ShapeDtypeStruct((b, l, NA_WIDTH), BF16),
                   jax.ShapeDtypeStruct((b, l, NA_WIDTH), BF16),
                   jax.ShapeDtypeStruct((b, l, NA_WIDTH), BF16),
                   jax.ShapeDtypeStruct((b, l, CONF_WIDTH), F32),
                   jax.ShapeDtypeStruct((b, l, LANES), F32),
                   jax.ShapeDtypeStruct((b, 2 * SSD_HEADS, l), F32)],
        compiler_params=_cparams("parallel", "parallel"),
        name="inproj",
    )(x, shift, scale, g, w_main, w_dt, w_dtt, b_lane, b_sub)


def _dwconv_body(prev_ref, cur_ref, next_ref, w_ref, b_ref, *rest, taps, tl, hb, strip, mode):
    if mode == "ln_silu":
        lg_ref, lb_ref, o_ref, buf, *sh = rest
    else:
        o_ref, buf, *sh = rest
    i = pl.program_id(1)
    n = pl.num_programs(1)
    buf[0:hb, :] = jnp.where(i > 0, prev_ref[0], 0.0)
    buf[hb:hb + tl, :] = cur_ref[0]
    buf[hb + tl:hb + tl + hb, :] = jnp.where(i < n - 1, next_ref[0], 0.0)
    shifted = sh[0] if sh else None
    if shifted is not None:
        for s in range(1, SUBLANES):
            shifted[s - 1] = buf[s:s + shifted.shape[1], :]

    def window(off):
        if shifted is not None and off % SUBLANES:
            a = off - off % SUBLANES
            return shifted[off % SUBLANES - 1, a:a + strip, :]
        return buf[off:off + strip, :]

    def tap_weight(j):
        wj = w_ref[j * SUBLANES:(j + 1) * SUBLANES, :]
        return jnp.concatenate([wj] * (strip // SUBLANES), axis=0)

    half = taps // 2
    for r0 in range(0, tl, strip):
        acc = b_ref[...] + tap_weight(0) * window(r0 + hb - half)
        for j in range(1, taps):
            acc = acc + tap_weight(j) * window(r0 + hb - half + j)
        if mode == "silu":
            o_ref[0, r0:r0 + strip, :] = _silu(acc)
        else:
            mu = jnp.mean(acc, axis=-1, keepdims=True)
            xc = acc - mu
            var = jnp.mean(xc * xc, axis=-1, keepdims=True)
            y = xc * lax.rsqrt(var + NORM_EPS) * lg_ref[...] + lb_ref[...]
            o_ref[0, r0:r0 + strip, :] = _silu(y).astype(o_ref.dtype)


def _dwconv(x, w, bias, tl, hb, strip, mode, ln=None, out_dtype=F32):
    b, l, c = x.shape
    taps = w.shape[0]
    nh = tl // hb
    last = l // hb - 1
    ins = [x, x, x, jnp.repeat(w, SUBLANES, axis=0), bias.reshape(1, c)]
    specs = [pl.BlockSpec((1, hb, c), lambda i, j: (i, jnp.maximum(j * nh - 1, 0), 0)),
             pl.BlockSpec((1, tl, c), lambda i, j: (i, j, 0)),
             pl.BlockSpec((1, hb, c), lambda i, j: (i, jnp.minimum((j + 1) * nh, last), 0)),
             _const_spec((taps * SUBLANES, c)), _const_spec((1, c))]
    if mode == "ln_silu":
        ins += [ln[0].reshape(1, c), ln[1].reshape(1, c)]
        specs += [_const_spec((1, c)), _const_spec((1, c))]
    return pl.pallas_call(
        functools.partial(_dwconv_body, taps=taps, tl=tl, hb=hb, strip=strip, mode=mode),
        grid=(b, l // tl),
        in_specs=specs,
        out_specs=pl.BlockSpec((1, tl, c), lambda i, j: (i, j, 0)),
        out_shape=jax.ShapeDtypeStruct((b, l, c), out_dtype),
        scratch_shapes=[pltpu.VMEM((tl + 2 * hb, c), F32)] + (
            [pltpu.VMEM((SUBLANES - 1, tl + 2 * hb - SUBLANES, c), F32)] if taps > SUBLANES else []),
        compiler_params=_cparams("parallel", "parallel"),
        name="dwconv_" + mode,
    )(*ins)


def _ssd_dir(u_ref, dtc_ref, dtt_ref, al_ref, as_ref, st_ref, y_ref, d, reverse):
    q = SSD_CHUNK
    li = lax.broadcasted_iota(jnp.int32, (q, q), 0)
    si = lax.broadcasted_iota(jnp.int32, (q, q), 1)
    lane = lax.broadcasted_iota(jnp.int32, (q, LANES), 1)
    lo_half = lane < SSD_HEAD_DIM
    mask = (si >= li) if reverse else (si <= li)
    tri = jnp.where(mask, 1.0, 0.0).astype(BF16)
    tri_t = jnp.where((li >= si) if reverse else (li <= si), 1.0, 0.0).astype(BF16)

    a_lane = -jnp.exp(al_ref[...])
    a_sub = -jnp.exp(as_ref[d * SSD_HEADS:(d + 1) * SSD_HEADS, :])
    dtc = dtc_ref[0]
    hi, mid, lo = _split3(dtc * a_lane)
    cs = _dot(tri, hi) + _dot(tri, mid) + _dot(tri, lo)
    dtr = dtt_ref[0, d * SSD_HEADS:(d + 1) * SSD_HEADS, :] * a_sub
    hi, mid, lo = _split3(dtr)
    cs_r = _dot(hi, tri_t) + _dot(mid, tri_t) + _dot(lo, tri_t)

    hpg = SSD_HEADS // SSD_GROUPS
    gw = hpg * SSD_HEAD_DIM
    for g in range(SSD_GROUPS):
        bm = u_ref[0, :, SSD_D_INNER + g * SSD_STATE:SSD_D_INNER + (g + 1) * SSD_STATE]
        cm = u_ref[0, :, SSD_D_INNER + (SSD_GROUPS + g) * SSD_STATE:
                   SSD_D_INNER + (SSD_GROUPS + g + 1) * SSD_STATE].astype(BF16)
        bm_t = bm.T.astype(BF16)
        scores = _dot(cm, bm_t)
        st = st_ref[d, g]
        y_off = _dot(cm, st.astype(BF16))
        xdd = []
        tots = []
        for pp in range(hpg // 2):
            p = g * (hpg // 2) + pp
            c0 = 16 * d + 2 * p
            bc0 = jnp.broadcast_to(cs[:, c0:c0 + 1], (q, LANES))
            bc1 = jnp.broadcast_to(cs[:, c0 + 1:c0 + 2], (q, LANES))
            csx = jnp.where(lo_half, bc0, bc1)
            dtx = jnp.where(lo_half, jnp.broadcast_to(dtc[:, c0:c0 + 1], (q, LANES)),
                            jnp.broadcast_to(dtc[:, c0 + 1:c0 + 2], (q, LANES)))
            xd = u_ref[0, :, p * LANES:(p + 1) * LANES] * dtx
            xd_b = xd.astype(BF16)
            dec0 = jnp.exp(jnp.where(mask, bc0 - cs_r[2 * p:2 * p + 1, :], -jnp.inf))
            dec1 = jnp.exp(jnp.where(mask, bc1 - cs_r[2 * p + 1:2 * p + 2, :], -jnp.inf))
            y0 = _dot((scores * dec0).astype(BF16), xd_b)
            y1 = _dot((scores * dec1).astype(BF16), xd_b)
            y = jnp.where(lo_half, y0, y1) + y_off[:, pp * LANES:(pp + 1) * LANES] * jnp.exp(csx)
            y_ref[0, :, p * LANES:(p + 1) * LANES] = y
            totx = csx[0:1, :] if reverse else csx[q - 1:q, :]
            xdd.append((xd * jnp.exp(totx - csx)).astype(BF16))
            tots.append(totx)
        new_states = _dot(bm_t, jnp.concatenate(xdd, axis=1))
        st_ref[d, g] = st * jnp.exp(jnp.concatenate(tots, axis=1)) + new_states


def _ssd_body(uf_ref, ub_ref, dcf_ref, dcb_ref, dtf_ref, dtb_ref, al_ref, as_ref, h0_ref,
              yf_ref, yb_ref, ht_ref, st_ref):
    i = pl.program_id(1)

    @pl.when(i == 0)
    def _():
        st_ref[...] = h0_ref[0]

    _ssd_dir(uf_ref, dcf_ref, dtf_ref, al_ref, as_ref, st_ref, yf_ref, 0, False)
    _ssd_dir(ub_ref, dcb_ref, dtb_ref, al_ref, as_ref, st_ref, yb_ref, 1, True)

    @pl.when(i == pl.num_programs(1) - 1)
    def _():
        ht_ref[0] = st_ref[...]


def _ssd(u, dt, dtt, a_lane, a_sub, h0):
    b, l, _ = u.shape
    q = SSD_CHUNK
    nc = l // q
    fwd = lambda n: pl.BlockSpec((1, q, n), lambda i, j: (i, j, 0))
    bwd = lambda n: pl.BlockSpec((1, q, n), lambda i, j: (i, nc - 1 - j, 0))
    st_shape = (2, SSD_GROUPS, SSD_STATE, SSD_D_INNER // SSD_GROUPS)
    st_spec = pl.BlockSpec((1,) + st_shape, lambda i, j: (i, 0, 0, 0, 0))
    return pl.pallas_call(
        _ssd_body,
        grid=(b, nc),
        in_specs=[fwd(SSD_XBC), bwd(SSD_XBC), fwd(LANES), bwd(LANES),
                  pl.BlockSpec((1, 2 * SSD_HEADS, q), lambda i, j: (i, 0, j)),
                  pl.BlockSpec((1, 2 * SSD_HEADS, q), lambda i, j: (i, 0, nc - 1 - j)),
                  _const_spec((1, LANES)), _const_spec((2 * SSD_HEADS, 1)), st_spec],
        out_specs=[fwd(SSD_D_INNER), bwd(SSD_D_INNER), st_spec],
        out_shape=[jax.ShapeDtypeStruct((b, l, SSD_D_INNER), F32),
                   jax.ShapeDtypeStruct((b, l, SSD_D_INNER), F32),
                   jax.ShapeDtypeStruct((b,) + st_shape, F32)],
        scratch_shapes=[pltpu.VMEM(st_shape, F32)],
        compiler_params=_cparams("parallel", "arbitrary"),
        name="ssd",
    )(u, u, dt, dt, dtt, dtt, a_lane, a_sub, h0)


def _na_body(*refs, window, n_win, first_key_row=None):
    if window:
        q_ref, k_ref, v_ref, kc_ref, vc_ref, bias_ref, o_ref = refs
    else:
        q_ref, kc_ref, vc_ref, o_ref = refs
    tq = q_ref.shape[1]
    lane = lax.broadcasted_iota(jnp.int32, (tq, LANES), 1)
    lo_half = lane < NA_HEAD_DIM
    if window:
        start = pl.multiple_of(first_key_row(pl.program_id(1)) * GRID_W, GRID_W)
    scores = []
    for h in range(NA_HEADS):
        sl = slice((h // 2) * LANES, (h // 2 + 1) * LANES)
        q2 = q_ref[0, :, sl]
        qm = jnp.where(lo_half if h % 2 == 0 else jnp.logical_not(lo_half), q2, jnp.zeros_like(q2))
        s = _dot_nt(qm, kc_ref[0, :, sl])
        if window:
            s_w = _dot_nt(qm, k_ref[0, pl.ds(start, n_win), sl]) + bias_ref[0, h]
            s = jnp.concatenate([s_w, s], axis=1)
        scores.append(s)
    s_all = jnp.concatenate(scores, axis=0)
    p_all = jnp.exp(s_all - jnp.max(s_all, axis=-1, keepdims=True))
    inv = 1.0 / jnp.sum(p_all, axis=-1, keepdims=True)
    p_all = p_all.astype(BF16)
    for hp in range(NA_HEADS // 2):
        sl = slice(hp * LANES, (hp + 1) * LANES)
        outs = []
        for par in range(2):
            h = 2 * hp + par
            p = p_all[h * tq:(h + 1) * tq]
            o = _dot(p[:, n_win:], vc_ref[0, :, sl])
            if window:
                o = o + _dot(p[:, :n_win], v_ref[0, pl.ds(start, n_win), sl])
            outs.append(o * inv[h * tq:(h + 1) * tq])
        o_ref[0, :, sl] = jnp.where(lo_half, outs[0], outs[1]).astype(o_ref.dtype)


def _na_geometry(rows):
    wr = min(NA_WIN_ROWS, rows)
    qrows = 2 if rows % 2 == 0 else 1
    per_vreg = LANES // GRID_W
    uw = -(-(wr + qrows - 1) // per_vreg) * per_vreg
    return qrows, wr, min(rows, uw)


def _na_bias_table(rpb, rows):
    qrows, wr, uw = _na_geometry(rows)
    steps = rows // qrows
    r_all = np.arange(rows).reshape(steps, qrows)
    rs_all = np.clip(r_all - wr // 2, 0, rows - wr)
    ks_all = np.clip(np.arange(steps) * qrows - wr // 2, 0, rows - uw)
    krow = ks_all[:, None, None] + np.arange(uw)[None, None, :]
    row_ok = (krow >= rs_all[:, :, None]) & (krow < rs_all[:, :, None] + wr)
    drow_all = np.where(row_ok, krow - r_all[:, :, None] + NA_WIN_ROWS - 1, -1)
    uniq, var_of_step = np.unique(drow_all.reshape(steps, -1), axis=0, return_inverse=True)
    n_var = uniq.shape[0]
    uniq = uniq.reshape(n_var, qrows, uw)
    qcol = np.arange(GRID_W)
    kcol = np.arange(GRID_W)
    col_start = np.clip(qcol - NA_WIN_COLS // 2, 0, GRID_W - NA_WIN_COLS)
    rel = kcol[None, :] - col_start[:, None]
    col_ok = (rel >= 0) & (rel < NA_WIN_COLS)
    dcol = np.clip(kcol[None, :] - qcol[:, None] + NA_WIN_COLS - 1, 0, 2 * NA_WIN_COLS - 2)
    n_dcol = 2 * NA_WIN_COLS - 1
    onehot = jnp.asarray(dcol[None, :, :] == np.arange(n_dcol)[:, None, None], F32)
    sel = rpb.astype(F32)[:, np.maximum(uniq, 0).reshape(-1), :]
    tab = jnp.einsum("hrc,cqk->hqrk", sel, onehot, precision=lax.Precision.HIGHEST)
    tab = tab.reshape(NA_HEADS, GRID_W, n_var, qrows, uw, GRID_W)
    ok = col_ok[None, :, None, None, None, :] & (uniq >= 0)[None, None, :, :, :, None]
    tab = jnp.where(ok, tab, NEG_INF)
    tab = jnp.transpose(tab, (2, 0, 3, 1, 4, 5)).reshape(n_var, NA_HEADS, qrows * GRID_W, uw * GRID_W)
    return tab, jnp.asarray(var_of_step.reshape(-1), jnp.int32)


def _na_latent(q, k, v, kc, vc, rpb):
    b, l, w = q.shape
    rows = l // GRID_W
    qrows, wr, uw = _na_geometry(rows)
    tab, var_of_step = _na_bias_table(rpb, rows)
    tq = qrows * GRID_W
    n_win = uw * GRID_W
    lc = kc.shape[1]
    full = lambda n: pl.BlockSpec((1, n, w), lambda i, r, vr: (i, 0, 0))
    grid_spec = pltpu.PrefetchScalarGridSpec(
        num_scalar_prefetch=1,
        grid=(b, rows // qrows),
        in_specs=[pl.BlockSpec((1, tq, w), lambda i, r, vr: (i, r, 0)),
                  full(l), full(l), full(lc), full(lc),
                  pl.BlockSpec((1, NA_HEADS, tq, n_win), lambda i, r, vr: (vr[r], 0, 0, 0))],
        out_specs=pl.BlockSpec((1, tq, w), lambda i, r, vr: (i, r, 0)),
    )

    def body(vr_ref, *refs):
        _na_body(*refs, window=True, n_win=n_win, first_key_row=lambda step: jnp.clip(
            step * qrows - wr // 2, 0, rows - uw))

    return pl.pallas_call(
        body,
        grid_spec=grid_spec,
        out_shape=jax.ShapeDtypeStruct((b, l, w), BF16),
        compiler_params=_cparams("parallel", "arbitrary"),
        name="na_latent",
    )(var_of_step, q, k, v, kc, vc, tab)


def _na_context(q, kc, vc):
    b, lc, w = q.shape
    tq = GRID_W
    full = pl.BlockSpec((1, lc, w), lambda i, j: (i, 0, 0))
    return pl.pallas_call(
        functools.partial(_na_body, window=False, n_win=0),
        grid=(b, lc // tq),
        in_specs=[pl.BlockSpec((1, tq, w), lambda i, j: (i, j, 0)), full, full],
        out_specs=pl.BlockSpec((1, tq, w), lambda i, j: (i, j, 0)),
        out_shape=jax.ShapeDtypeStruct((b, lc, w), BF16),
        compiler_params=_cparams("parallel", "arbitrary"),
        name="na_context",
    )(q, kc, vc)


def _tail_body(x_ref, mod_ref, gmix_ref, gffn_ref, yf_ref, yb_ref, xs_ref, z_ref, na_ref, cf_ref,
               dskip_ref, snorm_ref, wgate_ref, wssd_ref, wna_ref, wcf_ref, wo_ref, *rest, router):
    if router:
        wr_ref, xo_ref, h2_ref, lg_ref = rest
    else:
        xo_ref, h2_ref = rest
    d = x_ref.shape[2]
    x = x_ref[0]
    mod = mod_ref[0]
    h = (_rms(x, gmix_ref[...]) * (1.0 + mod[1:2]) + mod[0:1]).astype(BF16)
    y = (yf_ref[0] + yb_ref[0] + dskip_ref[...] * xs_ref[0]) * _silu(z_ref[0])
    ssd = _dot(_rms(y, snorm_ref[...]).astype(BF16), wssd_ref[...])
    merged = _sigmoid(_dot(h, wgate_ref[:, 0:d])) * ssd
    merged = merged + _sigmoid(_dot(h, wgate_ref[:, d:2 * d])) * _dot(na_ref[0], wna_ref[...])
    merged = merged + _sigmoid(_dot(h, wgate_ref[:, 2 * d:3 * d])) * _dot(cf_ref[0], wcf_ref[...])
    xn = x + mod[2:3] * _dot(merged.astype(BF16), wo_ref[...])
    xo_ref[0] = xn
    h2 = _rms(xn, gffn_ref[...]) * (1.0 + mod[4:5]) + mod[3:4]
    if router:
        _to_token_tiles(h2_ref.at[0], h2)
    else:
        h2_ref[0] = h2.astype(BF16)
    if router:
        hi = h2.astype(BF16)
        mid = (h2 - hi.astype(F32)).astype(BF16)
        r = _dot(hi, wr_ref[...])
        lg_ref[0] = _top2_route(r[:, :LANES] + r[:, LANES:] + _dot(mid, wr_ref[:, :LANES]))


def _tail(x, mod, gmix, gffn, yf, yb, u, z, na, cf, dskip, snorm, wgate, wssd, wna, wcf, wo, tm,
          w_router=None):
    b, l, d = x.shape
    row = lambda n: pl.BlockSpec((1, tm, n), lambda i, j: (i, j, 0))
    ins = [x, mod, gmix, gffn, yf, yb, u, z, na, cf, dskip, snorm, wgate, wssd, wna, wcf, wo]
    specs = [row(d), pl.BlockSpec((1, 8, d), lambda i, j: (i, 0, 0)), _const_spec((1, d)),
             _const_spec((1, d)), row(d), row(d), row(SSD_D_INNER), row(d), row(NA_WIDTH),
             row(CONF_WIDTH), _const_spec((1, d)), _const_spec((1, d)), _const_spec(wgate.shape),
             _const_spec(wssd.shape), _const_spec(wna.shape), _const_spec(wcf.shape),
             _const_spec(wo.shape)]
    out_specs = [row(d), row(d)]
    out_shape = [jax.ShapeDtypeStruct((b, l, d), F32), jax.ShapeDtypeStruct((b, l, d), BF16)]
    if w_router is not None:
        assert d == SUBLANES * LANES
        out_specs[1] = pl.BlockSpec((1, tm * SUBLANES, LANES), lambda i, j: (i, j, 0))
        out_shape[1] = jax.ShapeDtypeStruct((b, l * SUBLANES, LANES), F32)
        ins.append(w_router)
        specs.append(_const_spec(w_router.shape))
        out_specs.append(row(LANES))
        out_shape.append(jax.ShapeDtypeStruct((b, l, LANES), F32))
    return pl.pallas_call(
        functools.partial(_tail_body, router=w_router is not None),
        grid=(b, l // tm),
        in_specs=specs,
        out_specs=out_specs,
        out_shape=out_shape,
        compiler_params=_cparams("parallel", "parallel"),
        name="tail",
    )(*ins)


def _ffn_epilogue(x_ref, mod_ref, fn_ref, acc, o_ref, final):
    xn = x_ref[0] + mod_ref[0][5:6] * acc
    if final:
        xn = _rms(xn, fn_ref[...])
    o_ref[0] = xn


def _ffn_body(h_ref, x_ref, mod_ref, fn_ref, wg_ref, wu_ref, wd_ref, o_ref, acc_ref, *, final):
    k = pl.program_id(2)

    @pl.when(k == 0)
    def _():
        acc_ref[...] = jnp.zeros_like(acc_ref)

    h = h_ref[0]
    a = (_silu(_dot(h, wg_ref[...])) * _dot(h, wu_ref[...])).astype(BF16)
    acc_ref[...] += _dot(a, wd_ref[...])

    @pl.when(k == pl.num_programs(2) - 1)
    def _():
        _ffn_epilogue(x_ref, mod_ref, fn_ref, acc_ref[...], o_ref, final)


def _ffn(h, x, mod, fnorm, wg, wu, wd, tm, tf, final):
    b, l, d = x.shape
    ff = wg.shape[1]
    row = pl.BlockSpec((1, tm, d), lambda i, j, k: (i, j, 0))
    return pl.pallas_call(
        functools.partial(_ffn_body, final=final),
        grid=(b, l // tm, ff // tf),
        in_specs=[row, row, pl.BlockSpec((1, 8, d), lambda i, j, k: (i, 0, 0)),
                  pl.BlockSpec((1, d), lambda i, j, k: (0, 0)),
                  pl.BlockSpec((d, tf), lambda i, j, k: (0, k)),
                  pl.BlockSpec((d, tf), lambda i, j, k: (0, k)),
                  pl.BlockSpec((tf, d), lambda i, j, k: (k, 0))],
        out_specs=row,
        out_shape=jax.ShapeDtypeStruct((b, l, d), F32),
        scratch_shapes=[pltpu.VMEM((tm, d), F32)],
        compiler_params=_cparams("parallel", "parallel", "arbitrary"),
        name="ffn",
    )(h, x, mod, fnorm, wg, wu, wd)


def _top2_route(logits):
    lane = lax.broadcasted_iota(jnp.int32, logits.shape, 1)
    lg = jnp.where(lane < N_EXPERTS, logits, -jnp.inf)
    v1 = jnp.max(lg, axis=-1, keepdims=True)
    i1 = jnp.min(jnp.where(lg == v1, lane, LANES), axis=-1, keepdims=True)
    lg2 = jnp.where(lane == i1, -jnp.inf, lg)
    v2 = jnp.max(lg2, axis=-1, keepdims=True)
    i2 = jnp.min(jnp.where(lg2 == v2, lane, LANES), axis=-1, keepdims=True)
    e2 = jnp.exp(v2 - v1)
    w1 = 1.0 / (1.0 + e2)
    w2 = e2 / (1.0 + e2)
    rec = jnp.where(lane == 0, i1.astype(F32), 0.0) + jnp.where(lane == 1, i2.astype(F32), 0.0)
    return rec + jnp.where(lane == 2, w1, 0.0) + jnp.where(lane == 3, w2, 0.0)


def _to_token_tiles(ref, val):
    n = val.shape[0]
    for s in range(SUBLANES):
        ref[pl.ds(s, n, stride=SUBLANES), :] = val[:, s * LANES:(s + 1) * LANES]


def _from_token_tiles(ref, first, n):
    return jnp.concatenate([ref[pl.ds(first * SUBLANES + s, n, stride=SUBLANES), :] for s in range(SUBLANES)],
                           axis=1)


def _token_tile(ref, idx):
    return ref.at[pl.ds(pl.multiple_of(idx * SUBLANES, SUBLANES), SUBLANES), :]


def _dispatch_body(pos_ref, h_hbm, init_hbm, xs_hbm, sem, *, tc):
    del init_hbm
    j = pl.program_id(0)

    def body(r, carry):
        src = _token_tile(h_hbm, j * tc + r)
        for k in range(2):
            pltpu.make_async_copy(src, _token_tile(xs_hbm, pos_ref[0, k, r]), sem).start()
        return carry

    lax.fori_loop(0, tc, body, 0)

    def wait_one_step():
        n = 2 * tc * SUBLANES
        pltpu.make_async_copy(h_hbm.at[pl.ds(0, n), :], xs_hbm.at[pl.ds(0, n), :], sem).wait()

    @pl.when(j > 0)
    def _():
        wait_one_step()

    @pl.when(j == pl.num_programs(0) - 1)
    def _():
        wait_one_step()


def _dispatch(h_tiles, pos, n_rows, tc):
    t = h_tiles.shape[0] // SUBLANES
    return pl.pallas_call(
        functools.partial(_dispatch_body, tc=tc),
        grid=(t // tc,),
        in_specs=[pl.BlockSpec((1, 2, tc), lambda j: (j, 0, 0), memory_space=pltpu.SMEM),
                  pl.BlockSpec(memory_space=pl.ANY), pl.BlockSpec(memory_space=pl.ANY)],
        out_specs=pl.BlockSpec(memory_space=pl.ANY),
        out_shape=jax.ShapeDtypeStruct((n_rows * SUBLANES, LANES), F32),
        scratch_shapes=[pltpu.SemaphoreType.DMA(())],
        input_output_aliases={2: 0},
        compiler_params=_cparams("arbitrary"),
        name="moe_dispatch",
    )(pos, h_tiles, jnp.zeros((n_rows * SUBLANES, LANES), F32))


def _gmm_body(be_ref, nu_ref, x_ref, wg_ref, wu_ref, wd_ref, y_ref, *, tm, n_chunk):
    j = pl.program_id(0)

    @pl.when(j < nu_ref[0])
    def _():
        h = _from_token_tiles(x_ref, 0, tm).astype(BF16)
        ff = wg_ref.shape[1]
        tf = ff // n_chunk
        acc = None
        for c in range(n_chunk):
            cs = slice(c * tf, (c + 1) * tf)
            a = (_silu(_dot(h, wg_ref[:, cs])) * _dot(h, wu_ref[:, cs])).astype(BF16)
            part = _dot(a, wd_ref[cs, :])
            acc = part if acc is None else acc + part
        _to_token_tiles(y_ref, acc)

    @pl.when(j >= nu_ref[0])
    def _():
        y_ref[...] = jnp.zeros_like(y_ref)


def _gmm(xs, block_expert, n_used, wg, wu, wd, tm):
    ne, d, ff = wg.shape
    n_blocks = xs.shape[0] // (tm * SUBLANES)
    wspec = lambda r, c: pl.BlockSpec((None, r, c), lambda j, be, nu: (be[j], 0, 0))
    grid_spec = pltpu.PrefetchScalarGridSpec(
        num_scalar_prefetch=2,
        grid=(n_blocks,),
        in_specs=[pl.BlockSpec((tm * SUBLANES, LANES), lambda j, be, nu: (jnp.minimum(j, nu[0] - 1), 0)),
                  wspec(d, ff), wspec(d, ff), wspec(ff, d)],
        out_specs=pl.BlockSpec((tm * SUBLANES, LANES), lambda j, be, nu: (j, 0)),
    )
    return pl.pallas_call(
        functools.partial(_gmm_body, tm=tm, n_chunk=4),
        grid_spec=grid_spec,
        out_shape=jax.ShapeDtypeStruct(xs.shape, F32),
        compiler_params=_cparams("arbitrary"),
        name="moe_gmm",
    )(block_expert, n_used, xs, wg, wu, wd)


def _combine_body(pos_ref, posn_ref, rt_ref, x_ref, mod_ref, fn_ref, y_hbm, o_ref, buf, sem, *, tc, final):
    nl = pl.num_programs(1)
    j = pl.program_id(0) * nl + pl.program_id(1)
    n = pl.num_programs(0) * nl
    slot = j % 2

    def start(ids_ref, s):
        def body(r, carry):
            for k in range(2):
                pltpu.make_async_copy(_token_tile(y_hbm, ids_ref[0, k, r]),
                                      _token_tile(buf.at[s], k * tc + r), sem.at[s]).start()
            return carry
        lax.fori_loop(0, tc, body, 0)

    @pl.when(j == 0)
    def _():
        start(pos_ref, 0)

    pltpu.make_async_copy(y_hbm.at[pl.ds(0, 2 * tc * SUBLANES), :], buf.at[slot], sem.at[slot]).wait()

    @pl.when(j + 1 < n)
    def _():
        start(posn_ref, 1 - slot)

    rt = rt_ref[0]
    cur = buf.at[slot]
    f = rt[:, 2:3] * _from_token_tiles(cur, 0, tc) + rt[:, 3:4] * _from_token_tiles(cur, tc, tc)
    _ffn_epilogue(x_ref, mod_ref, fn_ref, f, o_ref, final)


def _combine(route, x, mod, fnorm, pos, y_sorted, tc, final):
    b, l, d = x.shape
    nl = l // tc
    n = b * nl
    ids = lambda shift: pl.BlockSpec((1, 2, tc), lambda i, j: (jnp.minimum(i * nl + j + shift, n - 1), 0, 0),
                                     memory_space=pltpu.SMEM)
    row = lambda w: pl.BlockSpec((1, tc, w), lambda i, j: (i, j, 0))
    return pl.pallas_call(
        functools.partial(_combine_body, tc=tc, final=final),
        grid=(b, nl),
        in_specs=[ids(0), ids(1), row(LANES), row(d), pl.BlockSpec((1, 8, d), lambda i, j: (i, 0, 0)),
                  _const_spec((1, d)), pl.BlockSpec(memory_space=pl.ANY)],
        out_specs=row(d),
        out_shape=jax.ShapeDtypeStruct((b, l, d), F32),
        scratch_shapes=[pltpu.VMEM((2, 2 * tc * SUBLANES, LANES), F32), pltpu.SemaphoreType.DMA((2,))],
        compiler_params=_cparams("arbitrary", "arbitrary"),
        name="moe_combine",
    )(pos, pos, route, x, mod, fnorm, y_sorted)


def _moe_sparse(h2, route, x, mod, fnorm, wg, wu, wd, tm, tc, final):
    b, l, d = x.shape
    t = b * l
    ne = wg.shape[0]
    rt = route.reshape(t, LANES)
    experts = jnp.concatenate([rt[:, 0], rt[:, 1]]).astype(jnp.int32)
    onehot = (experts[:, None] == jnp.arange(ne, dtype=jnp.int32)[None, :]).astype(jnp.int32)
    csum = jnp.cumsum(onehot, axis=0)
    rank = jnp.sum((csum - onehot) * onehot, axis=1)
    counts = csum[-1]
    padded = ((counts + tm - 1) // tm) * tm
    ends = jnp.cumsum(padded)
    pos = jnp.sum(onehot * (ends - padded)[None, :], axis=1) + rank
    n_blocks = (2 * t) // tm + ne
    block_start = jnp.arange(n_blocks, dtype=jnp.int32) * tm
    block_expert = jnp.minimum(jnp.sum(block_start[:, None] >= ends[None, :], axis=1), ne - 1).astype(jnp.int32)
    n_used = (ends[-1:] // tm).astype(jnp.int32)
    pos2 = jnp.transpose(pos.astype(jnp.int32).reshape(2, t // tc, tc), (1, 0, 2))
    xs = _dispatch(h2.reshape(t * SUBLANES, LANES), pos2, n_blocks * tm, tc)
    y_sorted = _gmm(xs, block_expert, n_used, wg, wu, wd, tm)
    return _combine(route, x, mod, fnorm, pos2, y_sorted, tc, final)


def _row_tile(l, want):
    return want if l % want == 0 else l


def kernel(x, c, ctx, c_ctx, ada_w, ada_b, norm_mix, norm_ffn, w_in, ssd_conv_w, ssd_conv_b, ssd_a_log, ssd_dt_bias, ssd_d, ssd_norm, ssd_out, na_rpb, na_out, conf_conv_w, conf_conv_b, conf_ln_g, conf_ln_b, conf_out, w_o, ffn_gate, ffn_up, ffn_down, moe_router, moe_gate, moe_up, moe_down, final_norm):
    depth = w_in.shape[0]
    b, l, d = x.shape
    lc = ctx.shape[1]
    nh = SSD_HEADS

    cc = jnp.zeros((8, d), F32).at[:b].set(c).at[b].set(c_ctx)
    mods = _ada(cc, ada_w, ada_b)
    fnorm = final_norm.reshape(1, d)

    offs = np.cumsum((0, SSD_D_INNER, SSD_XBC, nh, nh, NA_WIDTH, NA_WIDTH, NA_WIDTH, 2 * CONF_WIDTH, 3 * d))
    seg = lambda w, i: w[:, offs[i]:offs[i + 1]]

    for layer in range(depth):
        need_ctx = layer < depth - 1
        last = layer == depth - 1
        m6 = mods[layer].reshape(8, 6, d)
        mod_l = jnp.zeros((b, 8, d), F32).at[:, :6].set(m6[:b])
        mod_c = jnp.broadcast_to(jnp.zeros((8, d), F32).at[:6].set(m6[b]), (b, 8, d))

        wl = w_in[layer]
        w_main = jnp.concatenate([seg(wl, i) for i in (0, 1, 4, 5, 6, 7)], axis=1).astype(BF16)
        w_dt = jnp.concatenate([seg(wl, 2), seg(wl, 3)], axis=1)
        w_dt_pad = jnp.zeros((d, LANES), F32).at[:, :2 * nh].set(w_dt).astype(BF16)
        w_dtt = w_dt.T.astype(BF16)
        w_gate = seg(wl, 8).astype(BF16)
        dtb = ssd_dt_bias[layer].reshape(2 * nh)
        b_lane = jnp.zeros((1, LANES), F32).at[0, :2 * nh].set(dtb)
        b_sub = dtb.reshape(2 * nh, 1)
        alog = ssd_a_log[layer].reshape(2 * nh)
        a_lane = jnp.zeros((1, LANES), F32).at[0, :2 * nh].set(alog)
        a_sub = alog.reshape(2 * nh, 1)
        dskip = jnp.repeat(ssd_d[layer], SSD_HEAD_DIM).reshape(1, SSD_D_INNER)
        gmix = norm_mix[layer].reshape(1, d)
        gffn = norm_ffn[layer].reshape(1, d)
        snorm = ssd_norm[layer].reshape(1, SSD_D_INNER)
        wssd = ssd_out[layer].astype(BF16)
        wna = na_out[layer].astype(BF16)
        wcf = conf_out[layer].astype(BF16)
        wo = w_o[layer].astype(BF16)

        def mixer_inputs(xx, mod, tm):
            return _inproj(xx, mod[:, 0:1], mod[:, 1:2], gmix, w_main, w_dt_pad, w_dtt, b_lane, b_sub, tm)

        z_c, xbc_c, q_c, k_c, v_c, glu_c, dt_c, dtt_c = mixer_inputs(ctx, mod_c, _row_tile(lc, 256))
        z_l, xbc_l, q_l, k_l, v_l, glu_l, dt_l, dtt_l = mixer_inputs(x, mod_l, _row_tile(l, 512))

        conv5 = lambda t, tl: _dwconv(t, ssd_conv_w[layer], ssd_conv_b[layer], tl, 8, 16, "silu")
        u_c = conv5(xbc_c, _row_tile(lc, 256))
        u_l = conv5(xbc_l, _row_tile(l, 256))
        h0 = jnp.zeros((b, 2, SSD_GROUPS, SSD_STATE, SSD_D_INNER // SSD_GROUPS), F32)
        yf_c, yb_c, st = _ssd(u_c, dt_c, dtt_c, a_lane, a_sub, h0)
        yf_l, yb_l, _ = _ssd(u_l, dt_l, dtt_l, a_lane, a_sub, st)

        na_l = _na_latent(q_l, k_l, v_l, k_c, v_c, na_rpb[layer])

        conv31 = lambda t, tl: _dwconv(t, conf_conv_w[layer], conf_conv_b[layer], tl, 16, 32, "ln_silu",
                                       ln=(conf_ln_g[layer], conf_ln_b[layer]), out_dtype=BF16)
        cf_l = conv31(glu_l, _row_tile(l, 256))

        is_moe = layer % 2 == 1
        w_router = None
        if is_moe:
            wr_f = jnp.zeros((d, LANES), F32).at[:, :N_EXPERTS].set(moe_router[layer // 2])
            wr_hi = wr_f.astype(BF16)
            w_router = jnp.concatenate([wr_hi, (wr_f - wr_hi.astype(F32)).astype(BF16)], axis=1)
        tail = functools.partial(_tail, gmix=gmix, gffn=gffn, dskip=dskip, snorm=snorm, wgate=w_gate,
                                 wssd=wssd, wna=wna, wcf=wcf, wo=wo)
        outs = tail(x, mod_l, yf=yf_l, yb=yb_l, u=u_l, z=z_l, na=na_l, cf=cf_l, tm=_row_tile(l, 256),
                    w_router=w_router)
        x_mid, h2_l = outs[0], outs[1]
        if need_ctx:
            na_c = _na_context(q_c, k_c, v_c)
            cf_c = conv31(glu_c, _row_tile(lc, 256))
            outs_c = tail(ctx, mod_c, yf=yf_c, yb=yb_c, u=u_c, z=z_c, na=na_c, cf=cf_c,
                          tm=_row_tile(lc, 256), w_router=w_router)
            ctx_mid, h2_c = outs_c[0], outs_c[1]

        i = layer // 2
        if not is_moe:
            wg, wu, wd = (t[i].astype(BF16) for t in (ffn_gate, ffn_up, ffn_down))
            tf = wg.shape[1] // 2
            x = _ffn(h2_l, x_mid, mod_l, fnorm, wg, wu, wd, _row_tile(l, 512), tf, last)
            if need_ctx:
                ctx = _ffn(h2_c, ctx_mid, mod_c, fnorm, wg, wu, wd, _row_tile(lc, 256), tf, False)
        else:
            wg, wu, wd = (t[i].astype(BF16) for t in (moe_gate, moe_up, moe_down))
            x = _moe_sparse(h2_l, outs[2], x_mid, mod_l, fnorm, wg, wu, wd, 512, _row_tile(l, 256), last)
            if need_ctx:
                ctx = _moe_sparse(h2_c, outs_c[2], ctx_mid, mod_c, fnorm, wg, wu, wd, 512,
                                  _row_tile(lc, 256), False)
    return x
```

```python
import functools

import numpy as np
import jax
import jax.numpy as jnp
from jax import lax
from jax.experimental import pallas as pl
from jax.experimental.pallas import tpu as pltpu

F32 = jnp.float32
BF16 = jnp.bfloat16

NORM_EPS = 1e-6
NEG_INF = -1e30
GRID_W = 64

SSD_HEADS = 16
SSD_HEAD_DIM = 64
SSD_GROUPS = 2
SSD_STATE = 128
SSD_CHUNK = 128
SSD_D_INNER = SSD_HEADS * SSD_HEAD_DIM
SSD_XBC = SSD_D_INNER + 2 * SSD_GROUPS * SSD_STATE
SSD_CONV = 5

NA_HEADS = 8
NA_HEAD_DIM = 64
NA_WIDTH = NA_HEADS * NA_HEAD_DIM
NA_WIN_ROWS = 8
NA_WIN_COLS = 16

CONF_WIDTH = 512
CONF_KERNEL = 31
N_EXPERTS = 8

LANES = 128
SUBLANES = 8
VMEM_LIMIT = 56 << 20


def _cparams(*sem):
    return pltpu.CompilerParams(dimension_semantics=sem, vmem_limit_bytes=VMEM_LIMIT)


def _sigmoid(x):
    return 1.0 / (1.0 + jnp.exp(-x))


def _silu(x):
    return x * _sigmoid(x)


def _softplus(x):
    return jnp.maximum(x, 0.0) + jnp.log1p(jnp.exp(-jnp.abs(x)))


def _rms(x, g):
    return x * lax.rsqrt(jnp.mean(x * x, axis=-1, keepdims=True) + NORM_EPS) * g


def _dot(a, b):
    return jnp.dot(a, b, preferred_element_type=F32)


def _dot_nt(a, b):
    return lax.dot_general(a, b, (((1,), (1,)), ((), ())), preferred_element_type=F32)


def _split3(a):
    hi = a.astype(BF16)
    r1 = a - hi.astype(F32)
    mid = r1.astype(BF16)
    lo = (r1 - mid.astype(F32)).astype(BF16)
    return hi, mid, lo


def _const_spec(shape):
    nd = len(shape)
    return pl.BlockSpec(shape, lambda *_: (0,) * nd)


def _ada_body(c_ref, w_ref, b_ref, o_ref):
    s = _silu(c_ref[...]).astype(BF16)
    o_ref[...] = _dot(s, w_ref[...].astype(BF16)) + b_ref[...]


def _ada(cc, ada_w, ada_b):
    depth, d, n = ada_w.shape
    tn = 1536
    return pl.pallas_call(
        _ada_body,
        grid=(depth, n // tn),
        in_specs=[pl.BlockSpec((8, d), lambda l, j: (0, 0)),
                  pl.BlockSpec((None, d, tn), lambda l, j: (l, 0, j)),
                  pl.BlockSpec((None, 1, tn), lambda l, j: (l, 0, j))],
        out_specs=pl.BlockSpec((None, 8, tn), lambda l, j: (l, 0, j)),
        out_shape=jax.ShapeDtypeStruct((depth, 8, n), F32),
        compiler_params=_cparams("arbitrary", "arbitrary"),
        name="ada",
    )(cc, ada_w, ada_b.reshape(depth, 1, n))


_SEG_Z = (0, 1024)
_SEG_XBC = (1024, 2560)
_SEG_Q = (2560, 3072)
_SEG_K = (3072, 3584)
_SEG_V = (3584, 4096)
_SEG_GLU = (4096, 5120)

CONV_HALO = 16


def _dwconv_tile(buf, shifted, w_ref, b_ref, taps, tl, strip, emit):
    if shifted is not None:
        for s in range(1, SUBLANES):
            shifted[s - 1] = buf[s:s + shifted.shape[1], :]

    def window(off):
        if shifted is not None and off % SUBLANES:
            a = off - off % SUBLANES
            return shifted[off % SUBLANES - 1, a:a + strip, :]
        return buf[off:off + strip, :]

    def tap_weight(j):
        wj = w_ref[j * SUBLANES:(j + 1) * SUBLANES, :]
        return jnp.concatenate([wj] * (strip // SUBLANES), axis=0)

    first = CONV_HALO - taps // 2
    for r0 in range(0, tl, strip):
        acc = b_ref[...] + tap_weight(0) * window(r0 + first)
        for j in range(1, taps):
            acc = acc + tap_weight(j) * window(r0 + first + j)
        emit(r0, acc)


def _mixin_body(xp_ref, x_ref, xn_ref, sh_ref, sc_ref, g_ref, w_ref, wdt_ref, wdtt_ref, bl_ref, bs_ref,
                c5w_ref, c5b_ref, c31w_ref, c31b_ref, lng_ref, lnb_ref,
                z_ref, u_ref, q_ref, k_ref, v_ref, cf_ref, dt_ref, dtt_ref, buf5, buf31, sh31, *, tm):
    j = pl.program_id(1)
    hb = CONV_HALO
    norm_mod = lambda t: (_rms(t, g_ref[...]) * (1.0 + sc_ref[0]) + sh_ref[0]).astype(BF16)
    h = norm_mod(x_ref[0])
    hh = norm_mod(jnp.concatenate([xp_ref[0], xn_ref[0]], axis=0))
    has_prev = j > 0
    has_next = j < pl.num_programs(1) - 1

    cw = 2 * LANES

    def glu_chunk(c):
        def run():
            a0 = _SEG_GLU[0] + c * cw
            g0 = a0 + CONF_WIDTH
            glu = lambda t: _dot(t, w_ref[:, a0:a0 + cw]) * _sigmoid(_dot(t, w_ref[:, g0:g0 + cw]))
            halo = glu(hh)
            cols = slice(c * cw, (c + 1) * cw)
            buf31[0:hb, cols] = jnp.where(has_prev, halo[0:hb], 0.0)
            buf31[hb:hb + tm, cols] = glu(h)
            buf31[hb + tm:, cols] = jnp.where(has_next, halo[hb:], 0.0)
        return run

    def proj_chunk(o_ref, seg, c, scale=None):
        def run():
            r = _dot(h, w_ref[:, seg[0] + c * cw:seg[0] + (c + 1) * cw])
            o_ref[0, :, c * cw:(c + 1) * cw] = (r if scale is None else r * scale).astype(o_ref.dtype)
        return run

    def dt_chunks():
        dt_ref[0] = _softplus(_dot(h, wdt_ref[...]) + bl_ref[...])
        dtt_ref[0] = _softplus(_dot_nt(wdtt_ref[...], h) + bs_ref[...])

    during_conv5 = [glu_chunk(c) for c in range(CONF_WIDTH // cw)]
    during_conv31 = ([proj_chunk(z_ref, _SEG_Z, c) for c in range(4)]
                     + [proj_chunk(q_ref, _SEG_Q, c, NA_HEAD_DIM ** -0.5) for c in range(2)]
                     + [proj_chunk(k_ref, _SEG_K, c) for c in range(2)]
                     + [proj_chunk(v_ref, _SEG_V, c) for c in range(2)] + [dt_chunks])

    def run_some(pending, n):
        for _ in range(min(n, len(pending))):
            pending.pop(0)()

    halo = _dot(hh, w_ref[:, _SEG_XBC[0]:_SEG_XBC[1]])
    buf5[0:hb, :] = jnp.where(has_prev, halo[0:hb], 0.0)
    buf5[hb:hb + tm, :] = _dot(h, w_ref[:, _SEG_XBC[0]:_SEG_XBC[1]])
    buf5[hb + tm:, :] = jnp.where(has_next, halo[hb:], 0.0)

    def emit_u(r0, acc):
        u_ref[0, r0:r0 + 16, :] = _silu(acc)
        if r0 % 64 == 0:
            run_some(during_conv5, 1)

    _dwconv_tile(buf5, None, c5w_ref, c5b_ref, SSD_CONV, tm, 16, emit_u)
    run_some(during_conv5, len(during_conv5))

    def emit_cf(r0, acc):
        xc = acc - jnp.mean(acc, axis=-1, keepdims=True)
        var = jnp.mean(xc * xc, axis=-1, keepdims=True)
        y = xc * lax.rsqrt(var + NORM_EPS) * lng_ref[...] + lnb_ref[...]
        cf_ref[0, r0:r0 + 32, :] = _silu(y).astype(BF16)
        run_some(during_conv31, 2)

    _dwconv_tile(buf31, sh31, c31w_ref, c31b_ref, CONF_KERNEL, tm, 32, emit_cf)
    run_some(during_conv31, len(during_conv31))


def _mixin(x, shift, scale, g, w_main, w_dt, w_dtt, b_lane, b_sub, c5w, c5b, c31w, c31b, ln_g, ln_b, tm):
    b, l, d = x.shape
    hb = CONV_HALO
    nh = tm // hb
    last = l // hb - 1
    row = lambda n: pl.BlockSpec((1, tm, n), lambda i, j: (i, j, 0))
    mod = pl.BlockSpec((1, 1, d), lambda i, j: (i, 0, 0))
    rep = lambda w: jnp.repeat(w, SUBLANES, axis=0)
    consts = [g, w_main, w_dt, w_dtt, b_lane, b_sub, rep(c5w), c5b.reshape(1, -1), rep(c31w),
              c31b.reshape(1, -1), ln_g.reshape(1, -1), ln_b.reshape(1, -1)]
    return pl.pallas_call(
        functools.partial(_mixin_body, tm=tm),
        grid=(b, l // tm),
        in_specs=[pl.BlockSpec((1, hb, d), lambda i, j: (i, jnp.maximum(j * nh - 1, 0), 0)),
                  row(d),
                  pl.BlockSpec((1, hb, d), lambda i, j: (i, jnp.minimum((j + 1) * nh, last), 0)),
                  mod, mod] + [_const_spec(c.shape) for c in consts],
        out_specs=[row(1024), row(SSD_XBC), row(NA_WIDTH), row(NA_WIDTH), row(NA_WIDTH),
                   row(CONF_WIDTH), row(LANES),
                   pl.BlockSpec((1, 2 * SSD_HEADS, tm), lambda i, j: (i, 0, j))],
        out_shape=[jax.ShapeDtypeStruct((b, l, 1024), F32),
                   jax.ShapeDtypeStruct((b, l, SSD_XBC), F32),
                   jax.ShapeDtypeStruct((b, l, NA_WIDTH), BF16),
                   jax.ShapeDtypeStruct((b, l, NA_WIDTH), BF16),
                   jax.ShapeDtypeStruct((b, l, NA_WIDTH), BF16),
                   jax.ShapeDtypeStruct((b, l, CONF_WIDTH), BF16),
                   jax.ShapeDtypeStruct((b, l, LANES), F32),
                   jax.ShapeDtypeStruct((b, 2 * SSD_HEADS, l), F32)],
        scratch_shapes=[pltpu.VMEM((tm + 2 * hb, SSD_XBC), F32),
                        pltpu.VMEM((tm + 2 * hb, CONF_WIDTH), F32),
                        pltpu.VMEM((SUBLANES - 1, tm + 2 * hb - SUBLANES, CONF_WIDTH), F32)],
        compiler_params=_cparams("parallel", "parallel"),
        name="mixin",
    )(x, x, x, shift, scale, *consts)


def _ssd_dir(u_ref, dtc_ref, dtt_ref, al_ref, as_ref, st_ref, y_ref, d, reverse):
    q = SSD_CHUNK
    li = lax.broadcasted_iota(jnp.int32, (q, q), 0)
    si = lax.broadcasted_iota(jnp.int32, (q, q), 1)
    lane = lax.broadcasted_iota(jnp.int32, (q, LANES), 1)
    lo_half = lane < SSD_HEAD_DIM
    mask = (si >= li) if reverse else (si <= li)
    tri = jnp.where(mask, 1.0, 0.0).astype(BF16)
    tri_t = jnp.where((li >= si) if reverse else (li <= si), 1.0, 0.0).astype(BF16)

    a_lane = -jnp.exp(al_ref[...])
    a_sub = -jnp.exp(as_ref[d * SSD_HEADS:(d + 1) * SSD_HEADS, :])
    dtc = dtc_ref[0]
    hi, mid, lo = _split3(dtc * a_lane)
    cs = _dot(tri, hi) + _dot(tri, mid) + _dot(tri, lo)
    dtr = dtt_ref[0, d * SSD_HEADS:(d + 1) * SSD_HEADS, :] * a_sub
    hi, mid, lo = _split3(dtr)
    cs_r = _dot(hi, tri_t) + _dot(mid, tri_t) + _dot(lo, tri_t)

    hpg = SSD_HEADS // SSD_GROUPS
    gw = hpg * SSD_HEAD_DIM
    for g in range(SSD_GROUPS):
        bm = u_ref[0, :, SSD_D_INNER + g * SSD_STATE:SSD_D_INNER + (g + 1) * SSD_STATE]
        cm = u_ref[0, :, SSD_D_INNER + (SSD_GROUPS + g) * SSD_STATE:
                   SSD_D_INNER + (SSD_GROUPS + g + 1) * SSD_STATE].astype(BF16)
        bm_t = bm.T.astype(BF16)
        scores = _dot(cm, bm_t)
        st = st_ref[d, g]
        y_off = _dot(cm, st.astype(BF16))
        xdd = []
        tots = []
        for pp in range(hpg // 2):
            p = g * (hpg // 2) + pp
            c0 = 16 * d + 2 * p
            bc0 = jnp.broadcast_to(cs[:, c0:c0 + 1], (q, LANES))
            bc1 = jnp.broadcast_to(cs[:, c0 + 1:c0 + 2], (q, LANES))
            csx = jnp.where(lo_half, bc0, bc1)
            dtx = jnp.where(lo_half, jnp.broadcast_to(dtc[:, c0:c0 + 1], (q, LANES)),
                            jnp.broadcast_to(dtc[:, c0 + 1:c0 + 2], (q, LANES)))
            xd = u_ref[0, :, p * LANES:(p + 1) * LANES] * dtx
            xd_b = xd.astype(BF16)
            dec0 = jnp.exp(jnp.where(mask, bc0 - cs_r[2 * p:2 * p + 1, :], -jnp.inf))
            dec1 = jnp.exp(jnp.where(mask, bc1 - cs_r[2 * p + 1:2 * p + 2, :], -jnp.inf))
            y0 = _dot((scores * dec0).astype(BF16), xd_b)
            y1 = _dot((scores * dec1).astype(BF16), xd_b)
            y = jnp.where(lo_half, y0, y1) + y_off[:, pp * LANES:(pp + 1) * LANES] * jnp.exp(csx)
            y_ref[0, :, p * LANES:(p + 1) * LANES] = y
            totx = csx[0:1, :] if reverse else csx[q - 1:q, :]
            xdd.append((xd * jnp.exp(totx - csx)).astype(BF16))
            tots.append(totx)
        new_states = _dot(bm_t, jnp.concatenate(xdd, axis=1))
        st_ref[d, g] = st * jnp.exp(jnp.concatenate(tots, axis=1)) + new_states


def _ssd_body(uf_ref, ub_ref, dcf_ref, dcb_ref, dtf_ref, dtb_ref, al_ref, as_ref, h0_ref,
              yf_ref, yb_ref, ht_ref, st_ref):
    i = pl.program_id(1)

    @pl.when(i == 0)
    def _():
        st_ref[...] = h0_ref[0]

    _ssd_dir(uf_ref, dcf_ref, dtf_ref, al_ref, as_ref, st_ref, yf_ref, 0, False)
    _ssd_dir(ub_ref, dcb_ref, dtb_ref, al_ref, as_ref, st_ref, yb_ref, 1, True)

    @pl.when(i == pl.num_programs(1) - 1)
    def _():
        ht_ref[0] = st_ref[...]


def _ssd(u, dt, dtt, a_lane, a_sub, h0):
    b, l, _ = u.shape
    q = SSD_CHUNK
    nc = l // q
    fwd = lambda n: pl.BlockSpec((1, q, n), lambda i, j: (i, j, 0))
    bwd = lambda n: pl.BlockSpec((1, q, n), lambda i, j: (i, nc - 1 - j, 0))
    st_shape = (2, SSD_GROUPS, SSD_STATE, SSD_D_INNER // SSD_GROUPS)
    st_spec = pl.BlockSpec((1,) + st_shape, lambda i, j: (i, 0, 0, 0, 0))
    return pl.pallas_call(
        _ssd_body,
        grid=(b, nc),
        in_specs=[fwd(SSD_XBC), bwd(SSD_XBC), fwd(LANES), bwd(LANES),
                  pl.BlockSpec((1, 2 * SSD_HEADS, q), lambda i, j: (i, 0, j)),
                  pl.BlockSpec((1, 2 * SSD_HEADS, q), lambda i, j: (i, 0, nc - 1 - j)),
                  _const_spec((1, LANES)), _const_spec((2 * SSD_HEADS, 1)), st_spec],
        out_specs=[fwd(SSD_D_INNER), bwd(SSD_D_INNER), st_spec],
        out_shape=[jax.ShapeDtypeStruct((b, l, SSD_D_INNER), F32),
                   jax.ShapeDtypeStruct((b, l, SSD_D_INNER), F32),
                   jax.ShapeDtypeStruct((b,) + st_shape, F32)],
        scratch_shapes=[pltpu.VMEM(st_shape, F32)],
        compiler_params=_cparams("parallel", "arbitrary"),
        name="ssd",
    )(u, u, dt, dt, dtt, dtt, a_lane, a_sub, h0)


def _na_body(*refs, window, n_win, first_key_row=None):
    if window:
        q_ref, k_ref, v_ref, kc_ref, vc_ref, bias_ref, o_ref = refs
    else:
        q_ref, kc_ref, vc_ref, o_ref = refs
    tq = q_ref.shape[1]
    lane = lax.broadcasted_iota(jnp.int32, (tq, LANES), 1)
    lo_half = lane < NA_HEAD_DIM
    if window:
        start = pl.multiple_of(first_key_row(pl.program_id(1)) * GRID_W, GRID_W)
    scores = []
    for h in range(NA_HEADS):
        sl = slice((h // 2) * LANES, (h // 2 + 1) * LANES)
        q2 = q_ref[0, :, sl]
        qm = jnp.where(lo_half if h % 2 == 0 else jnp.logical_not(lo_half), q2, jnp.zeros_like(q2))
        s = _dot_nt(qm, kc_ref[0, :, sl])
        if window:
            s_w = _dot_nt(qm, k_ref[0, pl.ds(start, n_win), sl]) + bias_ref[0, h]
            s = jnp.concatenate([s_w, s], axis=1)
        scores.append(s)
    s_all = jnp.concatenate(scores, axis=0)
    p_all = jnp.exp(s_all - jnp.max(s_all, axis=-1, keepdims=True))
    inv = 1.0 / jnp.sum(p_all, axis=-1, keepdims=True)
    p_all = p_all.astype(BF16)
    for hp in range(NA_HEADS // 2):
        sl = slice(hp * LANES, (hp + 1) * LANES)
        outs = []
        for par in range(2):
            h = 2 * hp + par
            p = p_all[h * tq:(h + 1) * tq]
            o = _dot(p[:, n_win:], vc_ref[0, :, sl])
            if window:
                o = o + _dot(p[:, :n_win], v_ref[0, pl.ds(start, n_win), sl])
            outs.append(o * inv[h * tq:(h + 1) * tq])
        o_ref[0, :, sl] = jnp.where(lo_half, outs[0], outs[1]).astype(o_ref.dtype)


def _na_geometry(rows):
    wr = min(NA_WIN_ROWS, rows)
    qrows = 2 if rows % 2 == 0 else 1
    per_vreg = LANES // GRID_W
    uw = -(-(wr + qrows - 1) // per_vreg) * per_vreg
    return qrows, wr, min(rows, uw)


def _na_bias_table(rpb, rows):
    qrows, wr, uw = _na_geometry(rows)
    steps = rows // qrows
    r_all = np.arange(rows).reshape(steps, qrows)
    rs_all = np.clip(r_all - wr // 2, 0, rows - wr)
    ks_all = np.clip(np.arange(steps) * qrows - wr // 2, 0, rows - uw)
    krow = ks_all[:, None, None] + np.arange(uw)[None, None, :]
    row_ok = (krow >= rs_all[:, :, None]) & (krow < rs_all[:, :, None] + wr)
    drow_all = np.where(row_ok, krow - r_all[:, :, None] + NA_WIN_ROWS - 1, -1)
    uniq, var_of_step = np.unique(drow_all.reshape(steps, -1), axis=0, return_inverse=True)
    n_var = uniq.shape[0]
    uniq = uniq.reshape(n_var, qrows, uw)
    qcol = np.arange(GRID_W)
    kcol = np.arange(GRID_W)
    col_start = np.clip(qcol - NA_WIN_COLS // 2, 0, GRID_W - NA_WIN_COLS)
    rel = kcol[None, :] - col_start[:, None]
    col_ok = (rel >= 0) & (rel < NA_WIN_COLS)
    dcol = np.clip(kcol[None, :] - qcol[:, None] + NA_WIN_COLS - 1, 0, 2 * NA_WIN_COLS - 2)
    n_dcol = 2 * NA_WIN_COLS - 1
    onehot = jnp.asarray(dcol[None, :, :] == np.arange(n_dcol)[:, None, None], F32)
    sel = rpb.astype(F32)[:, np.maximum(uniq, 0).reshape(-1), :]
    tab = jnp.einsum("hrc,cqk->hqrk", sel, onehot, precision=lax.Precision.HIGHEST)
    tab = tab.reshape(NA_HEADS, GRID_W, n_var, qrows, uw, GRID_W)
    ok = col_ok[None, :, None, None, None, :] & (uniq >= 0)[None, None, :, :, :, None]
    tab = jnp.where(ok, tab, NEG_INF)
    tab = jnp.transpose(tab, (2, 0, 3, 1, 4, 5)).reshape(n_var, NA_HEADS, qrows * GRID_W, uw * GRID_W)
    return tab, jnp.asarray(var_of_step.reshape(-1), jnp.int32)


def _na_latent(q, k, v, kc, vc, rpb):
    b, l, w = q.shape
    rows = l // GRID_W
    qrows, wr, uw = _na_geometry(rows)
    tab, var_of_step = _na_bias_table(rpb, rows)
    tq = qrows * GRID_W
    n_win = uw * GRID_W
    lc = kc.shape[1]
    full = lambda n: pl.BlockSpec((1, n, w), lambda i, r, vr: (i, 0, 0))
    grid_spec = pltpu.PrefetchScalarGridSpec(
        num_scalar_prefetch=1,
        grid=(b, rows // qrows),
        in_specs=[pl.BlockSpec((1, tq, w), lambda i, r, vr: (i, r, 0)),
                  full(l), full(l), full(lc), full(lc),
                  pl.BlockSpec((1, NA_HEADS, tq, n_win), lambda i, r, vr: (vr[r], 0, 0, 0))],
        out_specs=pl.BlockSpec((1, tq, w), lambda i, r, vr: (i, r, 0)),
    )

    def body(vr_ref, *refs):
        _na_body(*refs, window=True, n_win=n_win, first_key_row=lambda step: jnp.clip(
            step * qrows - wr // 2, 0, rows - uw))

    return pl.pallas_call(
        body,
        grid_spec=grid_spec,
        out_shape=jax.ShapeDtypeStruct((b, l, w), BF16),
        compiler_params=_cparams("parallel", "arbitrary"),
        name="na_latent",
    )(var_of_step, q, k, v, kc, vc, tab)


def _na_context(q, kc, vc):
    b, lc, w = q.shape
    tq = GRID_W
    full = pl.BlockSpec((1, lc, w), lambda i, j: (i, 0, 0))
    return pl.pallas_call(
        functools.partial(_na_body, window=False, n_win=0),
        grid=(b, lc // tq),
        in_specs=[pl.BlockSpec((1, tq, w), lambda i, j: (i, j, 0)), full, full],
        out_specs=pl.BlockSpec((1, tq, w), lambda i, j: (i, j, 0)),
        out_shape=jax.ShapeDtypeStruct((b, lc, w), BF16),
        compiler_params=_cparams("parallel", "arbitrary"),
        name="na_context",
    )(q, kc, vc)


def _tail_body(x_ref, mod_ref, gmix_ref, gffn_ref, yf_ref, yb_ref, xs_ref, z_ref, na_ref, cf_ref,
               dskip_ref, snorm_ref, wgate_ref, wssd_ref, wna_ref, wcf_ref, wo_ref, *rest, router):
    if router:
        wr_ref, xo_ref, h2_ref, lg_ref = rest
    else:
        xo_ref, h2_ref = rest
    d = x_ref.shape[2]
    x = x_ref[0]
    mod = mod_ref[0]
    h = (_rms(x, gmix_ref[...]) * (1.0 + mod[1:2]) + mod[0:1]).astype(BF16)
    y = (yf_ref[0] + yb_ref[0] + dskip_ref[...] * xs_ref[0]) * _silu(z_ref[0])
    ssd = _dot(_rms(y, snorm_ref[...]).astype(BF16), wssd_ref[...])
    merged = _sigmoid(_dot(h, wgate_ref[:, 0:d])) * ssd
    merged = merged + _sigmoid(_dot(h, wgate_ref[:, d:2 * d])) * _dot(na_ref[0], wna_ref[...])
    merged = merged + _sigmoid(_dot(h, wgate_ref[:, 2 * d:3 * d])) * _dot(cf_ref[0], wcf_ref[...])
    xn = x + mod[2:3] * _dot(merged.astype(BF16), wo_ref[...])
    xo_ref[0] = xn
    h2 = _rms(xn, gffn_ref[...]) * (1.0 + mod[4:5]) + mod[3:4]
    if router:
        _to_token_tiles(h2_ref.at[0], h2)
    else:
        h2_ref[0] = h2.astype(BF16)
    if router:
        hi = h2.astype(BF16)
        mid = (h2 - hi.astype(F32)).astype(BF16)
        r = _dot(hi, wr_ref[...])
        lg_ref[0] = _top2_route(r[:, :LANES] + r[:, LANES:] + _dot(mid, wr_ref[:, :LANES]))


def _tail(x, mod, gmix, gffn, yf, yb, u, z, na, cf, dskip, snorm, wgate, wssd, wna, wcf, wo, tm,
          w_router=None):
    b, l, d = x.shape
    row = lambda n: pl.BlockSpec((1, tm, n), lambda i, j: (i, j, 0))
    ins = [x, mod, gmix, gffn, yf, yb, u, z, na, cf, dskip, snorm, wgate, wssd, wna, wcf, wo]
    specs = [row(d), pl.BlockSpec((1, 8, d), lambda i, j: (i, 0, 0)), _const_spec((1, d)),
             _const_spec((1, d)), row(d), row(d), row(SSD_D_INNER), row(d), row(NA_WIDTH),
             row(CONF_WIDTH), _const_spec((1, d)), _const_spec((1, d)), _const_spec(wgate.shape),
             _const_spec(wssd.shape), _const_spec(wna.shape), _const_spec(wcf.shape),
             _const_spec(wo.shape)]
    out_specs = [row(d), row(d)]
    out_shape = [jax.ShapeDtypeStruct((b, l, d), F32), jax.ShapeDtypeStruct((b, l, d), BF16)]
    if w_router is not None:
        assert d == SUBLANES * LANES
        out_specs[1] = pl.BlockSpec((1, tm * SUBLANES, LANES), lambda i, j: (i, j, 0))
        out_shape[1] = jax.ShapeDtypeStruct((b, l * SUBLANES, LANES), F32)
        ins.append(w_router)
        specs.append(_const_spec(w_router.shape))
        out_specs.append(row(LANES))
        out_shape.append(jax.ShapeDtypeStruct((b, l, LANES), F32))
    return pl.pallas_call(
        functools.partial(_tail_body, router=w_router is not None),
        grid=(b, l // tm),
        in_specs=specs,
        out_specs=out_specs,
        out_shape=out_shape,
        compiler_params=_cparams("parallel", "parallel"),
        name="tail",
    )(*ins)


def _ffn_epilogue(x_ref, mod_ref, fn_ref, acc, o_ref, final):
    xn = x_ref[0] + mod_ref[0][5:6] * acc
    if final:
        xn = _rms(xn, fn_ref[...])
    o_ref[0] = xn


def _ffn_body(h_ref, x_ref, mod_ref, fn_ref, wg_ref, wu_ref, wd_ref, o_ref, acc_ref, *, final):
    k = pl.program_id(2)

    @pl.when(k == 0)
    def _():
        acc_ref[...] = jnp.zeros_like(acc_ref)

    h = h_ref[0]
    a = (_silu(_dot(h, wg_ref[...])) * _dot(h, wu_ref[...])).astype(BF16)
    acc_ref[...] += _dot(a, wd_ref[...])

    @pl.when(k == pl.num_programs(2) - 1)
    def _():
        _ffn_epilogue(x_ref, mod_ref, fn_ref, acc_ref[...], o_ref, final)


def _ffn(h, x, mod, fnorm, wg, wu, wd, tm, tf, final):
    b, l, d = x.shape
    ff = wg.shape[1]
    row = pl.BlockSpec((1, tm, d), lambda i, j, k: (i, j, 0))
    return pl.pallas_call(
        functools.partial(_ffn_body, final=final),
        grid=(b, l // tm, ff // tf),
        in_specs=[row, row, pl.BlockSpec((1, 8, d), lambda i, j, k: (i, 0, 0)),
                  pl.BlockSpec((1, d), lambda i, j, k: (0, 0)),
                  pl.BlockSpec((d, tf), lambda i, j, k: (0, k)),
                  pl.BlockSpec((d, tf), lambda i, j, k: (0, k)),
                  pl.BlockSpec((tf, d), lambda i, j, k: (k, 0))],
        out_specs=row,
        out_shape=jax.ShapeDtypeStruct((b, l, d), F32),
        scratch_shapes=[pltpu.VMEM((tm, d), F32)],
        compiler_params=_cparams("parallel", "parallel", "arbitrary"),
        name="ffn",
    )(h, x, mod, fnorm, wg, wu, wd)


def _top2_route(logits):
    lane = lax.broadcasted_iota(jnp.int32, logits.shape, 1)
    lg = jnp.where(lane < N_EXPERTS, logits, -jnp.inf)
    v1 = jnp.max(lg, axis=-1, keepdims=True)
    i1 = jnp.min(jnp.where(lg == v1, lane, LANES), axis=-1, keepdims=True)
    lg2 = jnp.where(lane == i1, -jnp.inf, lg)
    v2 = jnp.max(lg2, axis=-1, keepdims=True)
    i2 = jnp.min(jnp.where(lg2 == v2, lane, LANES), axis=-1, keepdims=True)
    e2 = jnp.exp(v2 - v1)
    w1 = 1.0 / (1.0 + e2)
    w2 = e2 / (1.0 + e2)
    rec = jnp.where(lane == 0, i1.astype(F32), 0.0) + jnp.where(lane == 1, i2.astype(F32), 0.0)
    return rec + jnp.where(lane == 2, w1, 0.0) + jnp.where(lane == 3, w2, 0.0)


def _to_token_tiles(ref, val):
    n = val.shape[0]
    for s in range(SUBLANES):
        ref[pl.ds(s, n, stride=SUBLANES), :] = val[:, s * LANES:(s + 1) * LANES]


def _from_token_tiles(ref, first, n):
    return jnp.concatenate([ref[pl.ds(first * SUBLANES + s, n, stride=SUBLANES), :] for s in range(SUBLANES)],
                           axis=1)


def _token_tile(ref, idx):
    return ref.at[pl.ds(pl.multiple_of(idx * SUBLANES, SUBLANES), SUBLANES), :]


def _dispatch_body(pos_ref, h_ref, init_hbm, xs_hbm, sem, *, tc):
    del init_hbm

    def body(r, carry):
        src = _token_tile(h_ref, r)
        for k in range(2):
            pltpu.make_async_copy(src, _token_tile(xs_hbm, pos_ref[0, k, r]), sem).start()
        return carry

    lax.fori_loop(0, tc, body, 0)
    for _ in range(2):
        pltpu.make_async_copy(h_ref, xs_hbm.at[pl.ds(0, tc * SUBLANES), :], sem).wait()


def _dispatch(h_tiles, pos, n_rows, tc):
    t = h_tiles.shape[0] // SUBLANES
    return pl.pallas_call(
        functools.partial(_dispatch_body, tc=tc),
        grid=(t // tc,),
        in_specs=[pl.BlockSpec((1, 2, tc), lambda j: (j, 0, 0), memory_space=pltpu.SMEM),
                  pl.BlockSpec((tc * SUBLANES, LANES), lambda j: (j, 0)),
                  pl.BlockSpec(memory_space=pl.ANY)],
        out_specs=pl.BlockSpec(memory_space=pl.ANY),
        out_shape=jax.ShapeDtypeStruct((n_rows * SUBLANES, LANES), F32),
        scratch_shapes=[pltpu.SemaphoreType.DMA(())],
        input_output_aliases={2: 0},
        compiler_params=_cparams("arbitrary"),
        name="moe_dispatch",
    )(pos, h_tiles, jnp.zeros((n_rows * SUBLANES, LANES), F32))


def _gmm_body(be_ref, nu_ref, x_ref, wg_ref, wu_ref, wd_ref, y_ref, *, tm, n_chunk):
    j = pl.program_id(0)

    @pl.when(j < nu_ref[0])
    def _():
        h = _from_token_tiles(x_ref, 0, tm).astype(BF16)
        ff = wg_ref.shape[1]
        tf = ff // n_chunk
        acc = None
        for c in range(n_chunk):
            cs = slice(c * tf, (c + 1) * tf)
            a = (_silu(_dot(h, wg_ref[:, cs])) * _dot(h, wu_ref[:, cs])).astype(BF16)
            part = _dot(a, wd_ref[cs, :])
            acc = part if acc is None else acc + part
        _to_token_tiles(y_ref, acc)

    @pl.when(j >= nu_ref[0])
    def _():
        y_ref[...] = jnp.zeros_like(y_ref)


def _gmm(xs, block_expert, n_used, wg, wu, wd, tm):
    ne, d, ff = wg.shape
    n_blocks = xs.shape[0] // (tm * SUBLANES)
    wspec = lambda r, c: pl.BlockSpec((None, r, c), lambda j, be, nu: (be[j], 0, 0))
    grid_spec = pltpu.PrefetchScalarGridSpec(
        num_scalar_prefetch=2,
        grid=(n_blocks,),
        in_specs=[pl.BlockSpec((tm * SUBLANES, LANES), lambda j, be, nu: (jnp.minimum(j, nu[0] - 1), 0)),
                  wspec(d, ff), wspec(d, ff), wspec(ff, d)],
        out_specs=pl.BlockSpec((tm * SUBLANES, LANES), lambda j, be, nu: (j, 0)),
    )
    return pl.pallas_call(
        functools.partial(_gmm_body, tm=tm, n_chunk=4),
        grid_spec=grid_spec,
        out_shape=jax.ShapeDtypeStruct(xs.shape, F32),
        compiler_params=_cparams("arbitrary"),
        name="moe_gmm",
    )(block_expert, n_used, xs, wg, wu, wd)


def _combine_body(pos_ref, posn_ref, rt_ref, x_ref, mod_ref, fn_ref, y_hbm, o_ref, buf, sem, *, tc, final):
    nl = pl.num_programs(1)
    j = pl.program_id(0) * nl + pl.program_id(1)
    n = pl.num_programs(0) * nl
    slot = j % 2

    def start(ids_ref, s):
        def body(r, carry):
            for k in range(2):
                pltpu.make_async_copy(_token_tile(y_hbm, ids_ref[0, k, r]),
                                      _token_tile(buf.at[s], k * tc + r), sem.at[s]).start()
            return carry
        lax.fori_loop(0, tc, body, 0)

    @pl.when(j == 0)
    def _():
        start(pos_ref, 0)

    pltpu.make_async_copy(y_hbm.at[pl.ds(0, 2 * tc * SUBLANES), :], buf.at[slot], sem.at[slot]).wait()

    @pl.when(j + 1 < n)
    def _():
        start(posn_ref, 1 - slot)

    rt = rt_ref[0]
    cur = buf.at[slot]
    f = rt[:, 2:3] * _from_token_tiles(cur, 0, tc) + rt[:, 3:4] * _from_token_tiles(cur, tc, tc)
    _ffn_epilogue(x_ref, mod_ref, fn_ref, f, o_ref, final)


def _combine(route, x, mod, fnorm, pos, y_sorted, tc, final):
    b, l, d = x.shape
    nl = l // tc
    n = b * nl
    ids = lambda shift: pl.BlockSpec((1, 2, tc), lambda i, j: (jnp.minimum(i * nl + j + shift, n - 1), 0, 0),
                                     memory_space=pltpu.SMEM)
    row = lambda w: pl.BlockSpec((1, tc, w), lambda i, j: (i, j, 0))
    return pl.pallas_call(
        functools.partial(_combine_body, tc=tc, final=final),
        grid=(b, nl),
        in_specs=[ids(0), ids(1), row(LANES), row(d), pl.BlockSpec((1, 8, d), lambda i, j: (i, 0, 0)),
                  _const_spec((1, d)), pl.BlockSpec(memory_space=pl.ANY)],
        out_specs=row(d),
        out_shape=jax.ShapeDtypeStruct((b, l, d), F32),
        scratch_shapes=[pltpu.VMEM((2, 2 * tc * SUBLANES, LANES), F32), pltpu.SemaphoreType.DMA((2,))],
        compiler_params=_cparams("arbitrary", "arbitrary"),
        name="moe_combine",
    )(pos, pos, route, x, mod, fnorm, y_sorted)


def _moe_sparse(h2, route, x, mod, fnorm, wg, wu, wd, tm, tc, final):
    b, l, d = x.shape
    t = b * l
    ne = wg.shape[0]
    rt = route.reshape(t, LANES)
    experts = jnp.concatenate([rt[:, 0], rt[:, 1]]).astype(jnp.int32)
    onehot = (experts[:, None] == jnp.arange(ne, dtype=jnp.int32)[None, :]).astype(jnp.int32)
    csum = jnp.cumsum(onehot, axis=0)
    rank = jnp.sum((csum - onehot) * onehot, axis=1)
    counts = csum[-1]
    padded = ((counts + tm - 1) // tm) * tm
    ends = jnp.cumsum(padded)
    pos = jnp.sum(onehot * (ends - padded)[None, :], axis=1) + rank
    n_blocks = (2 * t) // tm + ne
    block_start = jnp.arange(n_blocks, dtype=jnp.int32) * tm
    block_expert = jnp.minimum(jnp.sum(block_start[:, None] >= ends[None, :], axis=1), ne - 1).astype(jnp.int32)
    n_used = (ends[-1:] // tm).astype(jnp.int32)
    tiled = lambda n: jnp.transpose(pos.astype(jnp.int32).reshape(2, t // n, n), (1, 0, 2))
    pos2 = tiled(tc)
    td = 1024 if t % 1024 == 0 else tc
    xs = _dispatch(h2.reshape(t * SUBLANES, LANES), tiled(td), n_blocks * tm, td)
    y_sorted = _gmm(xs, block_expert, n_used, wg, wu, wd, tm)
    return _combine(route, x, mod, fnorm, pos2, y_sorted, tc, final)


def _row_tile(l, want):
    return want if l % want == 0 else l


def kernel(x, c, ctx, c_ctx, ada_w, ada_b, norm_mix, norm_ffn, w_in, ssd_conv_w, ssd_conv_b, ssd_a_log, ssd_dt_bias, ssd_d, ssd_norm, ssd_out, na_rpb, na_out, conf_conv_w, conf_conv_b, conf_ln_g, conf_ln_b, conf_out, w_o, ffn_gate, ffn_up, ffn_down, moe_router, moe_gate, moe_up, moe_down, final_norm):
    depth = w_in.shape[0]
    b, l, d = x.shape
    lc = ctx.shape[1]
    nh = SSD_HEADS

    cc = jnp.zeros((8, d), F32).at[:b].set(c).at[b].set(c_ctx)
    mods = _ada(cc, ada_w, ada_b)
    fnorm = final_norm.reshape(1, d)

    offs = np.cumsum((0, SSD_D_INNER, SSD_XBC, nh, nh, NA_WIDTH, NA_WIDTH, NA_WIDTH, 2 * CONF_WIDTH, 3 * d))
    seg = lambda w, i: w[:, offs[i]:offs[i + 1]]

    for layer in range(depth):
        need_ctx = layer < depth - 1
        last = layer == depth - 1
        m6 = mods[layer].reshape(8, 6, d)
        mod_l = jnp.zeros((b, 8, d), F32).at[:, :6].set(m6[:b])
        mod_c = jnp.broadcast_to(jnp.zeros((8, d), F32).at[:6].set(m6[b]), (b, 8, d))

        wl = w_in[layer]
        w_main = jnp.concatenate([seg(wl, i) for i in (0, 1, 4, 5, 6, 7)], axis=1).astype(BF16)
        w_dt = jnp.concatenate([seg(wl, 2), seg(wl, 3)], axis=1)
        w_dt_pad = jnp.zeros((d, LANES), F32).at[:, :2 * nh].set(w_dt).astype(BF16)
        w_dtt = w_dt.T.astype(BF16)
        w_gate = seg(wl, 8).astype(BF16)
        dtb = ssd_dt_bias[layer].reshape(2 * nh)
        b_lane = jnp.zeros((1, LANES), F32).at[0, :2 * nh].set(dtb)
        b_sub = dtb.reshape(2 * nh, 1)
        alog = ssd_a_log[layer].reshape(2 * nh)
        a_lane = jnp.zeros((1, LANES), F32).at[0, :2 * nh].set(alog)
        a_sub = alog.reshape(2 * nh, 1)
        dskip = jnp.repeat(ssd_d[layer], SSD_HEAD_DIM).reshape(1, SSD_D_INNER)
        gmix = norm_mix[layer].reshape(1, d)
        gffn = norm_ffn[layer].reshape(1, d)
        snorm = ssd_norm[layer].reshape(1, SSD_D_INNER)
        wssd = ssd_out[layer].astype(BF16)
        wna = na_out[layer].astype(BF16)
        wcf = conf_out[layer].astype(BF16)
        wo = w_o[layer].astype(BF16)

        def mixer_inputs(xx, mod, tm):
            return _mixin(xx, mod[:, 0:1], mod[:, 1:2], gmix, w_main, w_dt_pad, w_dtt, b_lane, b_sub,
                          ssd_conv_w[layer], ssd_conv_b[layer], conf_conv_w[layer], conf_conv_b[layer],
                          conf_ln_g[layer], conf_ln_b[layer], tm)

        z_c, u_c, q_c, k_c, v_c, cf_c, dt_c, dtt_c = mixer_inputs(ctx, mod_c, _row_tile(lc, 256))
        z_l, u_l, q_l, k_l, v_l, cf_l, dt_l, dtt_l = mixer_inputs(x, mod_l, _row_tile(l, 256))

        h0 = jnp.zeros((b, 2, SSD_GROUPS, SSD_STATE, SSD_D_INNER // SSD_GROUPS), F32)
        yf_c, yb_c, st = _ssd(u_c, dt_c, dtt_c, a_lane, a_sub, h0)
        yf_l, yb_l, _ = _ssd(u_l, dt_l, dtt_l, a_lane, a_sub, st)

        na_l = _na_latent(q_l, k_l, v_l, k_c, v_c, na_rpb[layer])

        is_moe = layer % 2 == 1
        w_router = None
        if is_moe:
            wr_f = jnp.zeros((d, LANES), F32).at[:, :N_EXPERTS].set(moe_router[layer // 2])
            wr_hi = wr_f.astype(BF16)
            w_router = jnp.concatenate([wr_hi, (wr_f - wr_hi.astype(F32)).astype(BF16)], axis=1)
        tail = functools.partial(_tail, gmix=gmix, gffn=gffn, dskip=dskip, snorm=snorm, wgate=w_gate,
                                 wssd=wssd, wna=wna, wcf=wcf, wo=wo)
        outs = tail(x, mod_l, yf=yf_l, yb=yb_l, u=u_l, z=z_l, na=na_l, cf=cf_l, tm=_row_tile(l, 256),
                    w_router=w_router)
        x_mid, h2_l = outs[0], outs[1]
        if need_ctx:
            na_c = _na_context(q_c, k_c, v_c)
            outs_c = tail(ctx, mod_c, yf=yf_c, yb=yb_c, u=u_c, z=z_c, na=na_c, cf=cf_c,
                          tm=_row_tile(lc, 256), w_router=w_router)
            ctx_mid, h2_c = outs_c[0], outs_c[1]

        i = layer // 2
        if not is_moe:
            wg, wu, wd = (t[i].astype(BF16) for t in (ffn_gate, ffn_up, ffn_down))
            tf = wg.shape[1] // 2
            x = _ffn(h2_l, x_mid, mod_l, fnorm, wg, wu, wd, _row_tile(l, 512), tf, last)
            if need_ctx:
                ctx = _ffn(h2_c, ctx_mid, mod_c, fnorm, wg, wu, wd, _row_tile(lc, 256), tf, False)
        else:
            wg, wu, wd = (t[i].astype(BF16) for t in (moe_gate, moe_up, moe_down))
            x = _moe_sparse(h2_l, outs[2], x_mid, mod_l, fnorm, wg, wu, wd, 512, _row_tile(l, 256), last)
            if need_ctx:
                ctx = _moe_sparse(h2_c, outs_c[2], ctx_mid, mod_c, fnorm, wg, wu, wd, 512,
                                  _row_tile(lc, 256), False)
    return x
```

```python
import functools

import numpy as np
import jax
import jax.numpy as jnp
from jax import lax
from jax.experimental import pallas as pl
from jax.experimental.pallas import tpu as pltpu

F32 = jnp.float32
BF16 = jnp.bfloat16

NORM_EPS = 1e-6
NEG_INF = -1e30
GRID_W = 64

SSD_HEADS = 16
SSD_HEAD_DIM = 64
SSD_GROUPS = 2
SSD_STATE = 128
SSD_CHUNK = 128
SSD_D_INNER = SSD_HEADS * SSD_HEAD_DIM
SSD_XBC = SSD_D_INNER + 2 * SSD_GROUPS * SSD_STATE
SSD_CONV = 5

NA_HEADS = 8
NA_HEAD_DIM = 64
NA_WIDTH = NA_HEADS * NA_HEAD_DIM
NA_WIN_ROWS = 8
NA_WIN_COLS = 16

CONF_WIDTH = 512
CONF_KERNEL = 31
N_EXPERTS = 8

LANES = 128
SUBLANES = 8
VMEM_LIMIT = 56 << 20


def _cparams(*sem):
    return pltpu.CompilerParams(dimension_semantics=sem, vmem_limit_bytes=VMEM_LIMIT)


def _sigmoid(x):
    return 1.0 / (1.0 + jnp.exp(-x))


def _silu(x):
    return x * _sigmoid(x)


def _softplus(x):
    return jnp.maximum(x, 0.0) + jnp.log1p(jnp.exp(-jnp.abs(x)))


def _rms(x, g):
    return x * lax.rsqrt(jnp.mean(x * x, axis=-1, keepdims=True) + NORM_EPS) * g


def _dot(a, b):
    return jnp.dot(a, b, preferred_element_type=F32)


def _dot_nt(a, b):
    return lax.dot_general(a, b, (((1,), (1,)), ((), ())), preferred_element_type=F32)


def _split3(a):
    hi = a.astype(BF16)
    r1 = a - hi.astype(F32)
    mid = r1.astype(BF16)
    lo = (r1 - mid.astype(F32)).astype(BF16)
    return hi, mid, lo


def _const_spec(shape):
    nd = len(shape)
    return pl.BlockSpec(shape, lambda *_: (0,) * nd, pipeline_mode=pl.Buffered(1))


def _ada_body(c_ref, w_ref, b_ref, o_ref):
    s = _silu(c_ref[...]).astype(BF16)
    o_ref[...] = _dot(s, w_ref[...].astype(BF16)) + b_ref[...]


def _ada(cc, ada_w, ada_b):
    depth, d, n = ada_w.shape
    tn = 1536
    return pl.pallas_call(
        _ada_body,
        grid=(depth, n // tn),
        in_specs=[pl.BlockSpec((8, d), lambda l, j: (0, 0)),
                  pl.BlockSpec((None, d, tn), lambda l, j: (l, 0, j)),
                  pl.BlockSpec((None, 1, tn), lambda l, j: (l, 0, j))],
        out_specs=pl.BlockSpec((None, 8, tn), lambda l, j: (l, 0, j)),
        out_shape=jax.ShapeDtypeStruct((depth, 8, n), F32),
        compiler_params=_cparams("arbitrary", "arbitrary"),
        name="ada",
    )(cc, ada_w, ada_b.reshape(depth, 1, n))


_SEG_Z = (0, 1024)
_SEG_XBC = (1024, 2560)
_SEG_Q = (2560, 3072)
_SEG_K = (3072, 3584)
_SEG_V = (3584, 4096)
_SEG_GLU = (4096, 5120)

CONV_HALO = 16


def _dwconv_tile(buf, shifted, w_ref, b_ref, taps, tl, strip, emit):
    if shifted is not None:
        for s in range(1, SUBLANES):
            shifted[s - 1] = buf[s:s + shifted.shape[1], :]

    def window(off):
        if shifted is not None and off % SUBLANES:
            a = off - off % SUBLANES
            return shifted[off % SUBLANES - 1, a:a + strip, :]
        return buf[off:off + strip, :]

    def tap_weight(j):
        wj = w_ref[j * SUBLANES:(j + 1) * SUBLANES, :]
        return jnp.concatenate([wj] * (strip // SUBLANES), axis=0)

    first = CONV_HALO - taps // 2
    for r0 in range(0, tl, strip):
        acc = b_ref[...] + tap_weight(0) * window(r0 + first)
        for j in range(1, taps):
            acc = acc + tap_weight(j) * window(r0 + first + j)
        emit(r0, acc)


def _mixin_body(xp_ref, x_ref, xn_ref, sh_ref, sc_ref, g_ref, w_ref, wdt_ref, wdtt_ref, bl_ref, bs_ref,
                c5w_ref, c5b_ref, c31w_ref, c31b_ref, lng_ref, lnb_ref,
                z_ref, u_ref, q_ref, k_ref, v_ref, cf_ref, dt_ref, dtt_ref, buf5, buf31, sh31, *, tm):
    j = pl.program_id(1)
    hb = CONV_HALO
    norm_mod = lambda t: (_rms(t, g_ref[...]) * (1.0 + sc_ref[0]) + sh_ref[0]).astype(BF16)
    h = norm_mod(x_ref[0])
    hh = norm_mod(jnp.concatenate([xp_ref[0], xn_ref[0]], axis=0))
    has_prev = j > 0
    has_next = j < pl.num_programs(1) - 1

    cw = 2 * LANES

    def glu_chunk(c):
        def run():
            a0 = _SEG_GLU[0] + c * cw
            g0 = a0 + CONF_WIDTH
            glu = lambda t: _dot(t, w_ref[:, a0:a0 + cw]) * _sigmoid(_dot(t, w_ref[:, g0:g0 + cw]))
            halo = glu(hh)
            cols = slice(c * cw, (c + 1) * cw)
            buf31[0:hb, cols] = jnp.where(has_prev, halo[0:hb], 0.0)
            buf31[hb:hb + tm, cols] = glu(h)
            buf31[hb + tm:, cols] = jnp.where(has_next, halo[hb:], 0.0)
        return run

    def proj_chunk(o_ref, seg, c, scale=None):
        def run():
            r = _dot(h, w_ref[:, seg[0] + c * cw:seg[0] + (c + 1) * cw])
            o_ref[0, :, c * cw:(c + 1) * cw] = (r if scale is None else r * scale).astype(o_ref.dtype)
        return run

    def dt_chunks():
        dt_ref[0] = _softplus(_dot(h, wdt_ref[...]) + bl_ref[...])
        dtt_ref[0] = _softplus(_dot_nt(wdtt_ref[...], h) + bs_ref[...])

    during_conv5 = [glu_chunk(c) for c in range(CONF_WIDTH // cw)]
    during_conv31 = ([proj_chunk(z_ref, _SEG_Z, c) for c in range(4)]
                     + [proj_chunk(q_ref, _SEG_Q, c, NA_HEAD_DIM ** -0.5) for c in range(2)]
                     + [proj_chunk(k_ref, _SEG_K, c) for c in range(2)]
                     + [proj_chunk(v_ref, _SEG_V, c) for c in range(2)] + [dt_chunks])

    def run_some(pending, n):
        for _ in range(min(n, len(pending))):
            pending.pop(0)()

    halo = _dot(hh, w_ref[:, _SEG_XBC[0]:_SEG_XBC[1]])
    buf5[0:hb, :] = jnp.where(has_prev, halo[0:hb], 0.0)
    buf5[hb:hb + tm, :] = _dot(h, w_ref[:, _SEG_XBC[0]:_SEG_XBC[1]])
    buf5[hb + tm:, :] = jnp.where(has_next, halo[hb:], 0.0)

    def emit_u(r0, acc):
        u_ref[0, r0:r0 + 16, :] = _silu(acc)
        if r0 % 64 == 0:
            run_some(during_conv5, 1)

    _dwconv_tile(buf5, None, c5w_ref, c5b_ref, SSD_CONV, tm, 16, emit_u)
    run_some(during_conv5, len(during_conv5))

    def emit_cf(r0, acc):
        xc = acc - jnp.mean(acc, axis=-1, keepdims=True)
        var = jnp.mean(xc * xc, axis=-1, keepdims=True)
        y = xc * lax.rsqrt(var + NORM_EPS) * lng_ref[...] + lnb_ref[...]
        cf_ref[0, r0:r0 + 32, :] = _silu(y).astype(BF16)
        run_some(during_conv31, 2)

    _dwconv_tile(buf31, sh31, c31w_ref, c31b_ref, CONF_KERNEL, tm, 32, emit_cf)
    run_some(during_conv31, len(during_conv31))


def _mixin(x, shift, scale, g, w_main, w_dt, w_dtt, b_lane, b_sub, c5w, c5b, c31w, c31b, ln_g, ln_b, tm):
    b, l, d = x.shape
    hb = CONV_HALO
    nh = tm // hb
    last = l // hb - 1
    row = lambda n: pl.BlockSpec((1, tm, n), lambda i, j: (i, j, 0))
    mod = pl.BlockSpec((1, 1, d), lambda i, j: (i, 0, 0))
    rep = lambda w: jnp.repeat(w, SUBLANES, axis=0)
    consts = [g, w_main, w_dt, w_dtt, b_lane, b_sub, rep(c5w), c5b.reshape(1, -1), rep(c31w),
              c31b.reshape(1, -1), ln_g.reshape(1, -1), ln_b.reshape(1, -1)]
    return pl.pallas_call(
        functools.partial(_mixin_body, tm=tm),
        grid=(b, l // tm),
        in_specs=[pl.BlockSpec((1, hb, d), lambda i, j: (i, jnp.maximum(j * nh - 1, 0), 0)),
                  row(d),
                  pl.BlockSpec((1, hb, d), lambda i, j: (i, jnp.minimum((j + 1) * nh, last), 0)),
                  mod, mod] + [_const_spec(c.shape) for c in consts],
        out_specs=[row(1024), row(SSD_XBC), row(NA_WIDTH), row(NA_WIDTH), row(NA_WIDTH),
                   row(CONF_WIDTH), row(LANES),
                   pl.BlockSpec((1, 2 * SSD_HEADS, tm), lambda i, j: (i, 0, j))],
        out_shape=[jax.ShapeDtypeStruct((b, l, 1024), F32),
                   jax.ShapeDtypeStruct((b, l, SSD_XBC), F32),
                   jax.ShapeDtypeStruct((b, l, NA_WIDTH), BF16),
                   jax.ShapeDtypeStruct((b, l, NA_WIDTH), BF16),
                   jax.ShapeDtypeStruct((b, l, NA_WIDTH), BF16),
                   jax.ShapeDtypeStruct((b, l, CONF_WIDTH), BF16),
                   jax.ShapeDtypeStruct((b, l, LANES), F32),
                   jax.ShapeDtypeStruct((b, 2 * SSD_HEADS, l), F32)],
        scratch_shapes=[pltpu.VMEM((tm + 2 * hb, SSD_XBC), F32),
                        pltpu.VMEM((tm + 2 * hb, CONF_WIDTH), F32),
                        pltpu.VMEM((SUBLANES - 1, tm + 2 * hb - SUBLANES, CONF_WIDTH), F32)],
        compiler_params=_cparams("parallel", "parallel"),
        name="mixin",
    )(x, x, x, shift, scale, *consts)


def _ssd_dir(u_ref, dtc_ref, dtt_ref, al_ref, as_ref, st_ref, y_ref, d, reverse):
    q = SSD_CHUNK
    li = lax.broadcasted_iota(jnp.int32, (q, q), 0)
    si = lax.broadcasted_iota(jnp.int32, (q, q), 1)
    lane = lax.broadcasted_iota(jnp.int32, (q, LANES), 1)
    lo_half = lane < SSD_HEAD_DIM
    mask = (si >= li) if reverse else (si <= li)
    tri = jnp.where(mask, 1.0, 0.0).astype(BF16)
    tri_t = jnp.where((li >= si) if reverse else (li <= si), 1.0, 0.0).astype(BF16)

    a_lane = -jnp.exp(al_ref[...])
    a_sub = -jnp.exp(as_ref[d * SSD_HEADS:(d + 1) * SSD_HEADS, :])
    dtc = dtc_ref[0]
    hi, mid, lo = _split3(dtc * a_lane)
    cs = _dot(tri, hi) + _dot(tri, mid) + _dot(tri, lo)
    dtr = dtt_ref[0, d * SSD_HEADS:(d + 1) * SSD_HEADS, :] * a_sub
    hi, mid, lo = _split3(dtr)
    cs_r = _dot(hi, tri_t) + _dot(mid, tri_t) + _dot(lo, tri_t)
    yield

    hpg = SSD_HEADS // SSD_GROUPS
    gw = hpg * SSD_HEAD_DIM
    for g in range(SSD_GROUPS):
        bm = u_ref[0, :, SSD_D_INNER + g * SSD_STATE:SSD_D_INNER + (g + 1) * SSD_STATE]
        cm = u_ref[0, :, SSD_D_INNER + (SSD_GROUPS + g) * SSD_STATE:
                   SSD_D_INNER + (SSD_GROUPS + g + 1) * SSD_STATE].astype(BF16)
        bm_t = bm.T.astype(BF16)
        scores = _dot(cm, bm_t)
        st = st_ref[d, g]
        y_off = _dot(cm, st.astype(BF16))
        yield
        xdd = []
        tots = []
        for pp in range(hpg // 2):
            p = g * (hpg // 2) + pp
            c0 = 16 * d + 2 * p
            bc0 = jnp.broadcast_to(cs[:, c0:c0 + 1], (q, LANES))
            bc1 = jnp.broadcast_to(cs[:, c0 + 1:c0 + 2], (q, LANES))
            csx = jnp.where(lo_half, bc0, bc1)
            dtx = jnp.where(lo_half, jnp.broadcast_to(dtc[:, c0:c0 + 1], (q, LANES)),
                            jnp.broadcast_to(dtc[:, c0 + 1:c0 + 2], (q, LANES)))
            xd = u_ref[0, :, p * LANES:(p + 1) * LANES] * dtx
            xd_b = xd.astype(BF16)
            dec0 = jnp.exp(jnp.where(mask, bc0 - cs_r[2 * p:2 * p + 1, :], -jnp.inf))
            dec1 = jnp.exp(jnp.where(mask, bc1 - cs_r[2 * p + 1:2 * p + 2, :], -jnp.inf))
            y0 = _dot((scores * dec0).astype(BF16), xd_b)
            y1 = _dot((scores * dec1).astype(BF16), xd_b)
            y = jnp.where(lo_half, y0, y1) + y_off[:, pp * LANES:(pp + 1) * LANES] * jnp.exp(csx)
            y_ref[0, :, p * LANES:(p + 1) * LANES] = y
            totx = csx[0:1, :] if reverse else csx[q - 1:q, :]
            xdd.append((xd * jnp.exp(totx - csx)).astype(BF16))
            tots.append(totx)
            yield
        new_states = _dot(bm_t, jnp.concatenate(xdd, axis=1))
        st_ref[d, g] = st * jnp.exp(jnp.concatenate(tots, axis=1)) + new_states


def _ssd_body(uf_ref, ub_ref, dcf_ref, dcb_ref, dtf_ref, dtb_ref, al_ref, as_ref, h0_ref,
              yf_ref, yb_ref, ht_ref, st_ref):
    i = pl.program_id(1)

    @pl.when(i == 0)
    def _():
        st_ref[...] = h0_ref[0]

    pending = [_ssd_dir(uf_ref, dcf_ref, dtf_ref, al_ref, as_ref, st_ref, yf_ref, 0, False),
               _ssd_dir(ub_ref, dcb_ref, dtb_ref, al_ref, as_ref, st_ref, yb_ref, 1, True)]
    while pending:
        pending = [g for g in pending if next(g, "done") != "done"]

    @pl.when(i == pl.num_programs(1) - 1)
    def _():
        ht_ref[0] = st_ref[...]


def _ssd(u, dt, dtt, a_lane, a_sub, h0):
    b, l, _ = u.shape
    q = SSD_CHUNK
    nc = l // q
    fwd = lambda n: pl.BlockSpec((1, q, n), lambda i, j: (i, j, 0))
    bwd = lambda n: pl.BlockSpec((1, q, n), lambda i, j: (i, nc - 1 - j, 0))
    st_shape = (2, SSD_GROUPS, SSD_STATE, SSD_D_INNER // SSD_GROUPS)
    st_spec = pl.BlockSpec((1,) + st_shape, lambda i, j: (i, 0, 0, 0, 0))
    return pl.pallas_call(
        _ssd_body,
        grid=(b, nc),
        in_specs=[fwd(SSD_XBC), bwd(SSD_XBC), fwd(LANES), bwd(LANES),
                  pl.BlockSpec((1, 2 * SSD_HEADS, q), lambda i, j: (i, 0, j)),
                  pl.BlockSpec((1, 2 * SSD_HEADS, q), lambda i, j: (i, 0, nc - 1 - j)),
                  _const_spec((1, LANES)), _const_spec((2 * SSD_HEADS, 1)), st_spec],
        out_specs=[fwd(SSD_D_INNER), bwd(SSD_D_INNER), st_spec],
        out_shape=[jax.ShapeDtypeStruct((b, l, SSD_D_INNER), F32),
                   jax.ShapeDtypeStruct((b, l, SSD_D_INNER), F32),
                   jax.ShapeDtypeStruct((b,) + st_shape, F32)],
        scratch_shapes=[pltpu.VMEM(st_shape, F32)],
        compiler_params=_cparams("parallel", "arbitrary"),
        name="ssd",
    )(u, u, dt, dt, dtt, dtt, a_lane, a_sub, h0)


def _na_body(*refs, window, n_win, first_key_row=None):
    if window:
        q_ref, k_ref, v_ref, kc_ref, vc_ref, bias_ref, o_ref = refs
    else:
        q_ref, kc_ref, vc_ref, o_ref = refs
    tq = q_ref.shape[1]
    lane = lax.broadcasted_iota(jnp.int32, (tq, LANES), 1)
    lo_half = lane < NA_HEAD_DIM
    if window:
        start = pl.multiple_of(first_key_row(pl.program_id(1)) * GRID_W, GRID_W)
    scores = []
    for h in range(NA_HEADS):
        sl = slice((h // 2) * LANES, (h // 2 + 1) * LANES)
        q2 = q_ref[0, :, sl]
        qm = jnp.where(lo_half if h % 2 == 0 else jnp.logical_not(lo_half), q2, jnp.zeros_like(q2))
        s = _dot_nt(qm, kc_ref[0, :, sl])
        if window:
            s_w = _dot_nt(qm, k_ref[0, pl.ds(start, n_win), sl]) + bias_ref[0, h]
            s = jnp.concatenate([s_w, s], axis=1)
        scores.append(s)
    s_all = jnp.concatenate(scores, axis=0)
    p_all = jnp.exp(s_all - jnp.max(s_all, axis=-1, keepdims=True))
    inv = 1.0 / jnp.sum(p_all, axis=-1, keepdims=True)
    p_all = p_all.astype(BF16)
    for hp in range(NA_HEADS // 2):
        sl = slice(hp * LANES, (hp + 1) * LANES)
        outs = []
        for par in range(2):
            h = 2 * hp + par
            p = p_all[h * tq:(h + 1) * tq]
            o = _dot(p[:, n_win:], vc_ref[0, :, sl])
            if window:
                o = o + _dot(p[:, :n_win], v_ref[0, pl.ds(start, n_win), sl])
            outs.append(o * inv[h * tq:(h + 1) * tq])
        o_ref[0, :, sl] = jnp.where(lo_half, outs[0], outs[1]).astype(o_ref.dtype)


def _na_geometry(rows):
    wr = min(NA_WIN_ROWS, rows)
    qrows = 2 if rows % 2 == 0 else 1
    per_vreg = LANES // GRID_W
    uw = -(-(wr + qrows - 1) // per_vreg) * per_vreg
    return qrows, wr, min(rows, uw)


def _na_bias_table(rpb, rows):
    qrows, wr, uw = _na_geometry(rows)
    steps = rows // qrows
    r_all = np.arange(rows).reshape(steps, qrows)
    rs_all = np.clip(r_all - wr // 2, 0, rows - wr)
    ks_all = np.clip(np.arange(steps) * qrows - wr // 2, 0, rows - uw)
    krow = ks_all[:, None, None] + np.arange(uw)[None, None, :]
    row_ok = (krow >= rs_all[:, :, None]) & (krow < rs_all[:, :, None] + wr)
    drow_all = np.where(row_ok, krow - r_all[:, :, None] + NA_WIN_ROWS - 1, -1)
    uniq, var_of_step = np.unique(drow_all.reshape(steps, -1), axis=0, return_inverse=True)
    n_var = uniq.shape[0]
    uniq = uniq.reshape(n_var, qrows, uw)
    qcol = np.arange(GRID_W)
    kcol = np.arange(GRID_W)
    col_start = np.clip(qcol - NA_WIN_COLS // 2, 0, GRID_W - NA_WIN_COLS)
    rel = kcol[None, :] - col_start[:, None]
    col_ok = (rel >= 0) & (rel < NA_WIN_COLS)
    dcol = np.clip(kcol[None, :] - qcol[:, None] + NA_WIN_COLS - 1, 0, 2 * NA_WIN_COLS - 2)
    n_dcol = 2 * NA_WIN_COLS - 1
    onehot = jnp.asarray(dcol[None, :, :] == np.arange(n_dcol)[:, None, None], F32)
    sel = rpb.astype(F32)[:, np.maximum(uniq, 0).reshape(-1), :]
    tab = jnp.einsum("hrc,cqk->hqrk", sel, onehot, precision=lax.Precision.HIGHEST)
    tab = tab.reshape(NA_HEADS, GRID_W, n_var, qrows, uw, GRID_W)
    ok = col_ok[None, :, None, None, None, :] & (uniq >= 0)[None, None, :, :, :, None]
    tab = jnp.where(ok, tab, NEG_INF)
    tab = jnp.transpose(tab, (2, 0, 3, 1, 4, 5)).reshape(n_var, NA_HEADS, qrows * GRID_W, uw * GRID_W)
    return tab, jnp.asarray(var_of_step.reshape(-1), jnp.int32)


def _na_latent(q, k, v, kc, vc, rpb):
    b, l, w = q.shape
    rows = l // GRID_W
    qrows, wr, uw = _na_geometry(rows)
    tab, var_of_step = _na_bias_table(rpb, rows)
    tq = qrows * GRID_W
    n_win = uw * GRID_W
    lc = kc.shape[1]
    full = lambda n: pl.BlockSpec((1, n, w), lambda i, r, vr: (i, 0, 0))
    grid_spec = pltpu.PrefetchScalarGridSpec(
        num_scalar_prefetch=1,
        grid=(b, rows // qrows),
        in_specs=[pl.BlockSpec((1, tq, w), lambda i, r, vr: (i, r, 0)),
                  full(l), full(l), full(lc), full(lc),
                  pl.BlockSpec((1, NA_HEADS, tq, n_win), lambda i, r, vr: (vr[r], 0, 0, 0))],
        out_specs=pl.BlockSpec((1, tq, w), lambda i, r, vr: (i, r, 0)),
    )

    def body(vr_ref, *refs):
        _na_body(*refs, window=True, n_win=n_win, first_key_row=lambda step: jnp.clip(
            step * qrows - wr // 2, 0, rows - uw))

    return pl.pallas_call(
        body,
        grid_spec=grid_spec,
        out_shape=jax.ShapeDtypeStruct((b, l, w), BF16),
        compiler_params=_cparams("parallel", "arbitrary"),
        name="na_latent",
    )(var_of_step, q, k, v, kc, vc, tab)


def _na_context(q, kc, vc):
    b, lc, w = q.shape
    tq = GRID_W
    full = pl.BlockSpec((1, lc, w), lambda i, j: (i, 0, 0))
    return pl.pallas_call(
        functools.partial(_na_body, window=False, n_win=0),
        grid=(b, lc // tq),
        in_specs=[pl.BlockSpec((1, tq, w), lambda i, j: (i, j, 0)), full, full],
        out_specs=pl.BlockSpec((1, tq, w), lambda i, j: (i, j, 0)),
        out_shape=jax.ShapeDtypeStruct((b, lc, w), BF16),
        compiler_params=_cparams("parallel", "arbitrary"),
        name="na_context",
    )(q, kc, vc)


def _tail_body(x_ref, mod_ref, gmix_ref, gffn_ref, yf_ref, yb_ref, xs_ref, z_ref, na_ref, cf_ref,
               dskip_ref, snorm_ref, wgate_ref, wssd_ref, wna_ref, wcf_ref, wo_ref, *rest, router):
    if router:
        wr_ref, xo_ref, h2_ref, lg_ref = rest
    else:
        xo_ref, h2_ref = rest
    d = x_ref.shape[2]
    x = x_ref[0]
    mod = mod_ref[0]
    h = (_rms(x, gmix_ref[...]) * (1.0 + mod[1:2]) + mod[0:1]).astype(BF16)
    y = (yf_ref[0] + yb_ref[0] + dskip_ref[...] * xs_ref[0]) * _silu(z_ref[0])
    ssd = _dot(_rms(y, snorm_ref[...]).astype(BF16), wssd_ref[...])
    merged = _sigmoid(_dot(h, wgate_ref[:, 0:d])) * ssd
    merged = merged + _sigmoid(_dot(h, wgate_ref[:, d:2 * d])) * _dot(na_ref[0], wna_ref[...])
    merged = merged + _sigmoid(_dot(h, wgate_ref[:, 2 * d:3 * d])) * _dot(cf_ref[0], wcf_ref[...])
    xn = x + mod[2:3] * _dot(merged.astype(BF16), wo_ref[...])
    xo_ref[0] = xn
    h2 = _rms(xn, gffn_ref[...]) * (1.0 + mod[4:5]) + mod[3:4]
    if router:
        _to_token_tiles(h2_ref.at[0], h2)
    else:
        h2_ref[0] = h2.astype(BF16)
    if router:
        hi = h2.astype(BF16)
        mid = (h2 - hi.astype(F32)).astype(BF16)
        r = _dot(hi, wr_ref[...])
        lg_ref[0] = _top2_route(r[:, :LANES] + r[:, LANES:] + _dot(mid, wr_ref[:, :LANES]))


def _tail(x, mod, gmix, gffn, yf, yb, u, z, na, cf, dskip, snorm, wgate, wssd, wna, wcf, wo, tm,
          w_router=None):
    b, l, d = x.shape
    row = lambda n: pl.BlockSpec((1, tm, n), lambda i, j: (i, j, 0))
    ins = [x, mod, gmix, gffn, yf, yb, u, z, na, cf, dskip, snorm, wgate, wssd, wna, wcf, wo]
    specs = [row(d), pl.BlockSpec((1, 8, d), lambda i, j: (i, 0, 0)), _const_spec((1, d)),
             _const_spec((1, d)), row(d), row(d), row(SSD_D_INNER), row(d), row(NA_WIDTH),
             row(CONF_WIDTH), _const_spec((1, d)), _const_spec((1, d)), _const_spec(wgate.shape),
             _const_spec(wssd.shape), _const_spec(wna.shape), _const_spec(wcf.shape),
             _const_spec(wo.shape)]
    out_specs = [row(d), row(d)]
    out_shape = [jax.ShapeDtypeStruct((b, l, d), F32), jax.ShapeDtypeStruct((b, l, d), BF16)]
    if w_router is not None:
        assert d == SUBLANES * LANES
        out_specs[1] = pl.BlockSpec((1, tm * SUBLANES, LANES), lambda i, j: (i, j, 0))
        out_shape[1] = jax.ShapeDtypeStruct((b, l * SUBLANES, LANES), F32)
        ins.append(w_router)
        specs.append(_const_spec(w_router.shape))
        out_specs.append(row(LANES))
        out_shape.append(jax.ShapeDtypeStruct((b, l, LANES), F32))
    return pl.pallas_call(
        functools.partial(_tail_body, router=w_router is not None),
        grid=(b, l // tm),
        in_specs=specs,
        out_specs=out_specs,
        out_shape=out_shape,
        compiler_params=_cparams("parallel", "parallel"),
        name="tail",
    )(*ins)


def _ffn_epilogue(x_ref, mod_ref, fn_ref, acc, o_ref, final):
    xn = x_ref[0] + mod_ref[0][5:6] * acc
    if final:
        xn = _rms(xn, fn_ref[...])
    o_ref[0] = xn


def _ffn_body(h_ref, x_ref, mod_ref, fn_ref, wg_ref, wu_ref, wd_ref, o_ref, acc_ref, *, final):
    k = pl.program_id(2)

    @pl.when(k == 0)
    def _():
        acc_ref[...] = jnp.zeros_like(acc_ref)

    h = h_ref[0]
    a = (_silu(_dot(h, wg_ref[...])) * _dot(h, wu_ref[...])).astype(BF16)
    acc_ref[...] += _dot(a, wd_ref[...])

    @pl.when(k == pl.num_programs(2) - 1)
    def _():
        _ffn_epilogue(x_ref, mod_ref, fn_ref, acc_ref[...], o_ref, final)


def _ffn(h, x, mod, fnorm, wg, wu, wd, tm, tf, final):
    b, l, d = x.shape
    ff = wg.shape[1]
    row = pl.BlockSpec((1, tm, d), lambda i, j, k: (i, j, 0))
    return pl.pallas_call(
        functools.partial(_ffn_body, final=final),
        grid=(b, l // tm, ff // tf),
        in_specs=[row, row, pl.BlockSpec((1, 8, d), lambda i, j, k: (i, 0, 0)),
                  pl.BlockSpec((1, d), lambda i, j, k: (0, 0)),
                  pl.BlockSpec((d, tf), lambda i, j, k: (0, k)),
                  pl.BlockSpec((d, tf), lambda i, j, k: (0, k)),
                  pl.BlockSpec((tf, d), lambda i, j, k: (k, 0))],
        out_specs=row,
        out_shape=jax.ShapeDtypeStruct((b, l, d), F32),
        scratch_shapes=[pltpu.VMEM((tm, d), F32)],
        compiler_params=_cparams("parallel", "parallel", "arbitrary"),
        name="ffn",
    )(h, x, mod, fnorm, wg, wu, wd)


def _top2_route(logits):
    lane = lax.broadcasted_iota(jnp.int32, logits.shape, 1)
    lg = jnp.where(lane < N_EXPERTS, logits, -jnp.inf)
    v1 = jnp.max(lg, axis=-1, keepdims=True)
    i1 = jnp.min(jnp.where(lg == v1, lane, LANES), axis=-1, keepdims=True)
    lg2 = jnp.where(lane == i1, -jnp.inf, lg)
    v2 = jnp.max(lg2, axis=-1, keepdims=True)
    i2 = jnp.min(jnp.where(lg2 == v2, lane, LANES), axis=-1, keepdims=True)
    e2 = jnp.exp(v2 - v1)
    w1 = 1.0 / (1.0 + e2)
    w2 = e2 / (1.0 + e2)
    rec = jnp.where(lane == 0, i1.astype(F32), 0.0) + jnp.where(lane == 1, i2.astype(F32), 0.0)
    return rec + jnp.where(lane == 2, w1, 0.0) + jnp.where(lane == 3, w2, 0.0)


def _to_token_tiles(ref, val):
    n = val.shape[0]
    for s in range(SUBLANES):
        ref[pl.ds(s, n, stride=SUBLANES), :] = val[:, s * LANES:(s + 1) * LANES]


def _from_token_tiles(ref, first, n):
    return jnp.concatenate([ref[pl.ds(first * SUBLANES + s, n, stride=SUBLANES), :] for s in range(SUBLANES)],
                           axis=1)


def _token_tile(ref, idx):
    return ref.at[pl.ds(pl.multiple_of(idx * SUBLANES, SUBLANES), SUBLANES), :]


def _dispatch_body(lb_ref, pos_ref, h_ref, xs_hbm, zero_buf, sem, *, tc, tm):
    @pl.when(pl.program_id(0) == 0)
    def _():
        zero_buf[...] = jnp.zeros_like(zero_buf)
        fills = [pltpu.make_async_copy(
            zero_buf, xs_hbm.at[pl.ds(pl.multiple_of(lb_ref[e] * (tm * SUBLANES), tm * SUBLANES),
                                      tm * SUBLANES), :], sem) for e in range(lb_ref.shape[0])]
        for f in fills:
            f.start()
            f.wait()

    def body(r, carry):
        src = _token_tile(h_ref, r)
        for k in range(2):
            pltpu.make_async_copy(src, _token_tile(xs_hbm, pos_ref[0, k, r]), sem).start()
        return carry

    lax.fori_loop(0, tc, body, 0, unroll=4)
    for _ in range(2):
        pltpu.make_async_copy(h_ref, xs_hbm.at[pl.ds(0, tc * SUBLANES), :], sem).wait()


def _dispatch(h_tiles, pos, clear_blocks, n_rows, tc, tm):
    t = h_tiles.shape[0] // SUBLANES
    grid_spec = pltpu.PrefetchScalarGridSpec(
        num_scalar_prefetch=1,
        grid=(t // tc,),
        in_specs=[pl.BlockSpec((1, 2, tc), lambda j, lb: (j, 0, 0), memory_space=pltpu.SMEM),
                  pl.BlockSpec((tc * SUBLANES, LANES), lambda j, lb: (j, 0))],
        out_specs=pl.BlockSpec(memory_space=pl.ANY),
        scratch_shapes=[pltpu.VMEM((tm * SUBLANES, LANES), F32), pltpu.SemaphoreType.DMA(())],
    )
    return pl.pallas_call(
        functools.partial(_dispatch_body, tc=tc, tm=tm),
        grid_spec=grid_spec,
        out_shape=jax.ShapeDtypeStruct((n_rows * SUBLANES, LANES), F32),
        compiler_params=_cparams("arbitrary"),
        name="moe_dispatch",
    )(clear_blocks, pos, h_tiles)


def _gmm_body(be_ref, nu_ref, x_ref, wg_ref, wu_ref, wd_ref, y_ref, *, tm, n_chunk):
    j = pl.program_id(0)

    @pl.when(j < nu_ref[0])
    def _():
        h = _from_token_tiles(x_ref, 0, tm).astype(BF16)
        ff = wg_ref.shape[1]
        tf = ff // n_chunk
        acc = None
        for c in range(n_chunk):
            cs = slice(c * tf, (c + 1) * tf)
            a = (_silu(_dot(h, wg_ref[:, cs])) * _dot(h, wu_ref[:, cs])).astype(BF16)
            part = _dot(a, wd_ref[cs, :])
            acc = part if acc is None else acc + part
        _to_token_tiles(y_ref, acc)

    @pl.when(j >= nu_ref[0])
    def _():
        y_ref[...] = jnp.zeros_like(y_ref)


def _gmm(xs, block_expert, n_used, wg, wu, wd, tm):
    ne, d, ff = wg.shape
    n_blocks = xs.shape[0] // (tm * SUBLANES)
    wspec = lambda r, c: pl.BlockSpec((None, r, c), lambda j, be, nu: (be[j], 0, 0))
    grid_spec = pltpu.PrefetchScalarGridSpec(
        num_scalar_prefetch=2,
        grid=(n_blocks,),
        in_specs=[pl.BlockSpec((tm * SUBLANES, LANES), lambda j, be, nu: (jnp.minimum(j, nu[0] - 1), 0)),
                  wspec(d, ff), wspec(d, ff), wspec(ff, d)],
        out_specs=pl.BlockSpec((tm * SUBLANES, LANES), lambda j, be, nu: (j, 0)),
    )
    return pl.pallas_call(
        functools.partial(_gmm_body, tm=tm, n_chunk=4),
        grid_spec=grid_spec,
        out_shape=jax.ShapeDtypeStruct(xs.shape, F32),
        compiler_params=_cparams("arbitrary"),
        name="moe_gmm",
    )(block_expert, n_used, xs, wg, wu, wd)


def _combine_body(pos_ref, posn_ref, rt_ref, x_ref, mod_ref, fn_ref, y_hbm, o_ref, buf, sem, *, tc, final):
    nl = pl.num_programs(1)
    j = pl.program_id(0) * nl + pl.program_id(1)
    n = pl.num_programs(0) * nl
    slot = j % 2

    def start(ids_ref, s):
        def body(r, carry):
            for k in range(2):
                pltpu.make_async_copy(_token_tile(y_hbm, ids_ref[0, k, r]),
                                      _token_tile(buf.at[s], k * tc + r), sem.at[s]).start()
            return carry
        lax.fori_loop(0, tc, body, 0, unroll=4)

    @pl.when(j == 0)
    def _():
        start(pos_ref, 0)

    pltpu.make_async_copy(y_hbm.at[pl.ds(0, 2 * tc * SUBLANES), :], buf.at[slot], sem.at[slot]).wait()

    @pl.when(j + 1 < n)
    def _():
        start(posn_ref, 1 - slot)

    rt = rt_ref[0]
    cur = buf.at[slot]
    f = rt[:, 2:3] * _from_token_tiles(cur, 0, tc) + rt[:, 3:4] * _from_token_tiles(cur, tc, tc)
    _ffn_epilogue(x_ref, mod_ref, fn_ref, f, o_ref, final)


def _combine(route, x, mod, fnorm, pos, y_sorted, tc, final):
    b, l, d = x.shape
    nl = l // tc
    n = b * nl
    ids = lambda shift: pl.BlockSpec((1, 2, tc), lambda i, j: (jnp.minimum(i * nl + j + shift, n - 1), 0, 0),
                                     memory_space=pltpu.SMEM)
    row = lambda w: pl.BlockSpec((1, tc, w), lambda i, j: (i, j, 0))
    return pl.pallas_call(
        functools.partial(_combine_body, tc=tc, final=final),
        grid=(b, nl),
        in_specs=[ids(0), ids(1), row(LANES), row(d), pl.BlockSpec((1, 8, d), lambda i, j: (i, 0, 0)),
                  _const_spec((1, d)), pl.BlockSpec(memory_space=pl.ANY)],
        out_specs=row(d),
        out_shape=jax.ShapeDtypeStruct((b, l, d), F32),
        scratch_shapes=[pltpu.VMEM((2, 2 * tc * SUBLANES, LANES), F32), pltpu.SemaphoreType.DMA((2,))],
        compiler_params=_cparams("arbitrary", "arbitrary"),
        name="moe_combine",
    )(pos, pos, route, x, mod, fnorm, y_sorted)


def _moe_sparse(h2, route, x, mod, fnorm, wg, wu, wd, tm, tc, final):
    b, l, d = x.shape
    t = b * l
    ne = wg.shape[0]
    rt = route.reshape(t, LANES)
    experts = jnp.concatenate([rt[:, 0], rt[:, 1]]).astype(jnp.int32)
    onehot = (experts[:, None] == jnp.arange(ne, dtype=jnp.int32)[None, :]).astype(jnp.int32)
    csum = jnp.cumsum(onehot, axis=0)
    rank = jnp.sum((csum - onehot) * onehot, axis=1)
    counts = csum[-1]
    padded = ((counts + tm - 1) // tm) * tm
    ends = jnp.cumsum(padded)
    pos = jnp.sum(onehot * (ends - padded)[None, :], axis=1) + rank
    n_blocks = (2 * t) // tm + ne
    block_start = jnp.arange(n_blocks, dtype=jnp.int32) * tm
    block_expert = jnp.minimum(jnp.sum(block_start[:, None] >= ends[None, :], axis=1), ne - 1).astype(jnp.int32)
    n_used = (ends[-1:] // tm).astype(jnp.int32)
    tiled = lambda n: jnp.transpose(pos.astype(jnp.int32).reshape(2, t // n, n), (1, 0, 2))
    pos2 = tiled(tc)
    td = 1024 if t % 1024 == 0 else tc
    clear = jnp.concatenate([ends // tm - 1, n_used[0] + jnp.arange(ne, dtype=jnp.int32)])
    clear = jnp.clip(clear, 0, n_blocks - 1).astype(jnp.int32)
    xs = _dispatch(h2.reshape(t * SUBLANES, LANES), tiled(td), clear, n_blocks * tm, td, tm)
    y_sorted = _gmm(xs, block_expert, n_used, wg, wu, wd, tm)
    return _combine(route, x, mod, fnorm, pos2, y_sorted, tc, final)


def _row_tile(l, want):
    return want if l % want == 0 else l


def kernel(x, c, ctx, c_ctx, ada_w, ada_b, norm_mix, norm_ffn, w_in, ssd_conv_w, ssd_conv_b, ssd_a_log, ssd_dt_bias, ssd_d, ssd_norm, ssd_out, na_rpb, na_out, conf_conv_w, conf_conv_b, conf_ln_g, conf_ln_b, conf_out, w_o, ffn_gate, ffn_up, ffn_down, moe_router, moe_gate, moe_up, moe_down, final_norm):
    depth = w_in.shape[0]
    b, l, d = x.shape
    lc = ctx.shape[1]
    nh = SSD_HEADS

    cc = jnp.zeros((8, d), F32).at[:b].set(c).at[b].set(c_ctx)
    mods = _ada(cc, ada_w, ada_b)
    fnorm = final_norm.reshape(1, d)

    offs = np.cumsum((0, SSD_D_INNER, SSD_XBC, nh, nh, NA_WIDTH, NA_WIDTH, NA_WIDTH, 2 * CONF_WIDTH, 3 * d))
    seg = lambda w, i: w[:, offs[i]:offs[i + 1]]

    for layer in range(depth):
        need_ctx = layer < depth - 1
        last = layer == depth - 1
        m6 = mods[layer].reshape(8, 6, d)
        mod_l = jnp.zeros((b, 8, d), F32).at[:, :6].set(m6[:b])
        mod_c = jnp.broadcast_to(jnp.zeros((8, d), F32).at[:6].set(m6[b]), (b, 8, d))

        wl = w_in[layer]
        w_main = jnp.concatenate([seg(wl, i) for i in (0, 1, 4, 5, 6, 7)], axis=1).astype(BF16)
        w_dt = jnp.concatenate([seg(wl, 2), seg(wl, 3)], axis=1)
        w_dt_pad = jnp.zeros((d, LANES), F32).at[:, :2 * nh].set(w_dt).astype(BF16)
        w_dtt = w_dt.T.astype(BF16)
        w_gate = seg(wl, 8).astype(BF16)
        dtb = ssd_dt_bias[layer].reshape(2 * nh)
        b_lane = jnp.zeros((1, LANES), F32).at[0, :2 * nh].set(dtb)
        b_sub = dtb.reshape(2 * nh, 1)
        alog = ssd_a_log[layer].reshape(2 * nh)
        a_lane = jnp.zeros((1, LANES), F32).at[0, :2 * nh].set(alog)
        a_sub = alog.reshape(2 * nh, 1)
        dskip = jnp.repeat(ssd_d[layer], SSD_HEAD_DIM).reshape(1, SSD_D_INNER)
        gmix = norm_mix[layer].reshape(1, d)
        gffn = norm_ffn[layer].reshape(1, d)
        snorm = ssd_norm[layer].reshape(1, SSD_D_INNER)
        wssd = ssd_out[layer].astype(BF16)
        wna = na_out[layer].astype(BF16)
        wcf = conf_out[layer].astype(BF16)
        wo = w_o[layer].astype(BF16)

        def mixer_inputs(xx, mod, tm):
            return _mixin(xx, mod[:, 0:1], mod[:, 1:2], gmix, w_main, w_dt_pad, w_dtt, b_lane, b_sub,
                          ssd_conv_w[layer], ssd_conv_b[layer], conf_conv_w[layer], conf_conv_b[layer],
                          conf_ln_g[layer], conf_ln_b[layer], tm)

        z_c, u_c, q_c, k_c, v_c, cf_c, dt_c, dtt_c = mixer_inputs(ctx, mod_c, _row_tile(lc, 256))
        z_l, u_l, q_l, k_l, v_l, cf_l, dt_l, dtt_l = mixer_inputs(x, mod_l, _row_tile(l, 256))

        h0 = jnp.zeros((b, 2, SSD_GROUPS, SSD_STATE, SSD_D_INNER // SSD_GROUPS), F32)
        yf_c, yb_c, st = _ssd(u_c, dt_c, dtt_c, a_lane, a_sub, h0)
        yf_l, yb_l, _ = _ssd(u_l, dt_l, dtt_l, a_lane, a_sub, st)

        na_l = _na_latent(q_l, k_l, v_l, k_c, v_c, na_rpb[layer])

        is_moe = layer % 2 == 1
        w_router = None
        if is_moe:
            wr_f = jnp.zeros((d, LANES), F32).at[:, :N_EXPERTS].set(moe_router[layer // 2])
            wr_hi = wr_f.astype(BF16)
            w_router = jnp.concatenate([wr_hi, (wr_f - wr_hi.astype(F32)).astype(BF16)], axis=1)
        tail = functools.partial(_tail, gmix=gmix, gffn=gffn, dskip=dskip, snorm=snorm, wgate=w_gate,
                                 wssd=wssd, wna=wna, wcf=wcf, wo=wo)
        outs = tail(x, mod_l, yf=yf_l, yb=yb_l, u=u_l, z=z_l, na=na_l, cf=cf_l, tm=_row_tile(l, 512),
                    w_router=w_router)
        x_mid, h2_l = outs[0], outs[1]
        if need_ctx:
            na_c = _na_context(q_c, k_c, v_c)
            outs_c = tail(ctx, mod_c, yf=yf_c, yb=yb_c, u=u_c, z=z_c, na=na_c, cf=cf_c,
                          tm=_row_tile(lc, 256), w_router=w_router)
            ctx_mid, h2_c = outs_c[0], outs_c[1]

        i = layer // 2
        if not is_moe:
            wg, wu, wd = (t[i].astype(BF16) for t in (ffn_gate, ffn_up, ffn_down))
            tf = wg.shape[1] // 2
            x = _ffn(h2_l, x_mid, mod_l, fnorm, wg, wu, wd, _row_tile(l, 512), tf, last)
            if need_ctx:
                ctx = _ffn(h2_c, ctx_mid, mod_c, fnorm, wg, wu, wd, _row_tile(lc, 256), tf, False)
        else:
            wg, wu, wd = (t[i].astype(BF16) for t in (moe_gate, moe_up, moe_down))
            x = _moe_sparse(h2_l, outs[2], x_mid, mod_l, fnorm, wg, wu, wd, 512, _row_tile(l, 256), last)
            if need_ctx:
                ctx = _moe_sparse(h2_c, outs_c[2], ctx_mid, mod_c, fnorm, wg, wu, wd, 512,
                                  _row_tile(lc, 256), False)
    return x
```

```python
import functools

import numpy as np
import jax
import jax.numpy as jnp
from jax import lax
from jax.experimental import pallas as pl
from jax.experimental.pallas import tpu as pltpu

F32 = jnp.float32
BF16 = jnp.bfloat16

NORM_EPS = 1e-6
NEG_INF = -1e30
GRID_W = 64

SSD_HEADS = 16
SSD_HEAD_DIM = 64
SSD_GROUPS = 2
SSD_STATE = 128
SSD_CHUNK = 128
SSD_D_INNER = SSD_HEADS * SSD_HEAD_DIM
SSD_XBC = SSD_D_INNER + 2 * SSD_GROUPS * SSD_STATE
SSD_CONV = 5

NA_HEADS = 8
NA_HEAD_DIM = 64
NA_WIDTH = NA_HEADS * NA_HEAD_DIM
NA_WIN_ROWS = 8
NA_WIN_COLS = 16
LOG2_E = 1.4426950408889634
NA_Q_SCALE = NA_HEAD_DIM ** -0.5 * LOG2_E

CONF_WIDTH = 512
CONF_KERNEL = 31
N_EXPERTS = 8

LANES = 128
SUBLANES = 8
VMEM_LIMIT = 56 << 20


def _cparams(*sem):
    return pltpu.CompilerParams(dimension_semantics=sem, vmem_limit_bytes=VMEM_LIMIT)


def _sigmoid(x):
    return 1.0 / (1.0 + jnp.exp(-x))


def _silu(x):
    return x * _sigmoid(x)


def _softplus(x):
    return jnp.maximum(x, 0.0) + jnp.log1p(jnp.exp(-jnp.abs(x)))


def _rms(x, g):
    return x * lax.rsqrt(jnp.mean(x * x, axis=-1, keepdims=True) + NORM_EPS) * g


def _dot(a, b):
    return jnp.dot(a, b, preferred_element_type=F32)


def _dot_nt(a, b):
    return lax.dot_general(a, b, (((1,), (1,)), ((), ())), preferred_element_type=F32)


def _split3(a):
    hi = a.astype(BF16)
    r1 = a - hi.astype(F32)
    mid = r1.astype(BF16)
    lo = (r1 - mid.astype(F32)).astype(BF16)
    return hi, mid, lo


def _const_spec(shape):
    nd = len(shape)
    return pl.BlockSpec(shape, lambda *_: (0,) * nd, pipeline_mode=pl.Buffered(1))


def _ada_body(c_ref, w_ref, b_ref, o_ref):
    s = _silu(c_ref[...]).astype(BF16)
    o_ref[...] = _dot(s, w_ref[...].astype(BF16)) + b_ref[...]


def _ada(cc, ada_w, ada_b):
    depth, d, n = ada_w.shape
    tn = 1536
    return pl.pallas_call(
        _ada_body,
        grid=(depth, n // tn),
        in_specs=[pl.BlockSpec((8, d), lambda l, j: (0, 0)),
                  pl.BlockSpec((None, d, tn), lambda l, j: (l, 0, j)),
                  pl.BlockSpec((None, 1, tn), lambda l, j: (l, 0, j))],
        out_specs=pl.BlockSpec((None, 8, tn), lambda l, j: (l, 0, j)),
        out_shape=jax.ShapeDtypeStruct((depth, 8, n), F32),
        compiler_params=_cparams("arbitrary", "arbitrary"),
        name="ada",
    )(cc, ada_w, ada_b.reshape(depth, 1, n))


_SEG_Z = (0, 1024)
_SEG_XBC = (1024, 2560)
_SEG_Q = (2560, 3072)
_SEG_K = (3072, 3584)
_SEG_V = (3584, 4096)
_SEG_GLU = (4096, 5120)

CONV_HALO = 16


def _dwconv_tile(buf, shifted, w_ref, b_ref, taps, tl, strip, emit):
    if shifted is not None:
        for s in range(1, SUBLANES):
            shifted[s - 1] = buf[s:s + shifted.shape[1], :]

    def window(off):
        if shifted is not None and off % SUBLANES:
            a = off - off % SUBLANES
            return shifted[off % SUBLANES - 1, a:a + strip, :]
        return buf[off:off + strip, :]

    def tap_weight(j):
        wj = w_ref[j * SUBLANES:(j + 1) * SUBLANES, :]
        return jnp.concatenate([wj] * (strip // SUBLANES), axis=0)

    first = CONV_HALO - taps // 2
    for r0 in range(0, tl, strip):
        acc = b_ref[...] + tap_weight(0) * window(r0 + first)
        for j in range(1, taps):
            acc = acc + tap_weight(j) * window(r0 + first + j)
        emit(r0, acc)


def _mixin_body(xp_ref, x_ref, xn_ref, sh_ref, sc_ref, g_ref, w_ref, wdt_ref, wdtt_ref, bl_ref, bs_ref,
                c5w_ref, c5b_ref, c31w_ref, c31b_ref, lng_ref, lnb_ref,
                z_ref, u_ref, q_ref, k_ref, v_ref, cf_ref, dt_ref, dtt_ref, buf5, buf31, sh31, *, tm):
    j = pl.program_id(1)
    hb = CONV_HALO
    norm_mod = lambda t: (_rms(t, g_ref[...]) * (1.0 + sc_ref[0]) + sh_ref[0]).astype(BF16)
    h = norm_mod(x_ref[0])
    hh = norm_mod(jnp.concatenate([xp_ref[0], xn_ref[0]], axis=0))
    has_prev = j > 0
    has_next = j < pl.num_programs(1) - 1

    cw = 2 * LANES

    def glu_chunk(c):
        def run():
            a0 = _SEG_GLU[0] + c * cw
            g0 = a0 + CONF_WIDTH
            glu = lambda t: _dot(t, w_ref[:, a0:a0 + cw]) * _sigmoid(_dot(t, w_ref[:, g0:g0 + cw]))
            halo = glu(hh)
            cols = slice(c * cw, (c + 1) * cw)
            buf31[0:hb, cols] = jnp.where(has_prev, halo[0:hb], 0.0)
            buf31[hb:hb + tm, cols] = glu(h)
            buf31[hb + tm:, cols] = jnp.where(has_next, halo[hb:], 0.0)
        return run

    def proj_chunk(o_ref, seg, c, scale=None):
        def run():
            r = _dot(h, w_ref[:, seg[0] + c * cw:seg[0] + (c + 1) * cw])
            o_ref[0, :, c * cw:(c + 1) * cw] = (r if scale is None else r * scale).astype(o_ref.dtype)
        return run

    def dt_chunks():
        dt_ref[0] = _softplus(_dot(h, wdt_ref[...]) + bl_ref[...])
        dtt_ref[0] = _softplus(_dot_nt(wdtt_ref[...], h) + bs_ref[...])

    during_conv5 = [glu_chunk(c) for c in range(CONF_WIDTH // cw)]
    during_conv31 = ([proj_chunk(z_ref, _SEG_Z, c) for c in range(4)]
                     + [proj_chunk(q_ref, _SEG_Q, c, NA_Q_SCALE) for c in range(2)]
                     + [proj_chunk(k_ref, _SEG_K, c) for c in range(2)]
                     + [proj_chunk(v_ref, _SEG_V, c) for c in range(2)] + [dt_chunks])

    def run_some(pending, n):
        for _ in range(min(n, len(pending))):
            pending.pop(0)()

    halo = _dot(hh, w_ref[:, _SEG_XBC[0]:_SEG_XBC[1]])
    buf5[0:hb, :] = jnp.where(has_prev, halo[0:hb], 0.0)
    buf5[hb:hb + tm, :] = _dot(h, w_ref[:, _SEG_XBC[0]:_SEG_XBC[1]])
    buf5[hb + tm:, :] = jnp.where(has_next, halo[hb:], 0.0)

    def emit_u(r0, acc):
        u_ref[0, r0:r0 + 16, :] = _silu(acc)
        if r0 % 64 == 0:
            run_some(during_conv5, 1)

    _dwconv_tile(buf5, None, c5w_ref, c5b_ref, SSD_CONV, tm, 16, emit_u)
    run_some(during_conv5, len(during_conv5))

    def emit_cf(r0, acc):
        xc = acc - jnp.mean(acc, axis=-1, keepdims=True)
        var = jnp.mean(xc * xc, axis=-1, keepdims=True)
        y = xc * lax.rsqrt(var + NORM_EPS) * lng_ref[...] + lnb_ref[...]
        cf_ref[0, r0:r0 + 32, :] = _silu(y).astype(BF16)
        run_some(during_conv31, 2)

    _dwconv_tile(buf31, sh31, c31w_ref, c31b_ref, CONF_KERNEL, tm, 32, emit_cf)
    run_some(during_conv31, len(during_conv31))


def _mixin(x, shift, scale, g, w_main, w_dt, w_dtt, b_lane, b_sub, c5w, c5b, c31w, c31b, ln_g, ln_b, tm):
    b, l, d = x.shape
    hb = CONV_HALO
    nh = tm // hb
    last = l // hb - 1
    row = lambda n: pl.BlockSpec((1, tm, n), lambda i, j: (i, j, 0))
    mod = pl.BlockSpec((1, 1, d), lambda i, j: (i, 0, 0))
    rep = lambda w: jnp.repeat(w, SUBLANES, axis=0)
    consts = [g, w_main, w_dt, w_dtt, b_lane, b_sub, rep(c5w), c5b.reshape(1, -1), rep(c31w),
              c31b.reshape(1, -1), ln_g.reshape(1, -1), ln_b.reshape(1, -1)]
    return pl.pallas_call(
        functools.partial(_mixin_body, tm=tm),
        grid=(b, l // tm),
        in_specs=[pl.BlockSpec((1, hb, d), lambda i, j: (i, jnp.maximum(j * nh - 1, 0), 0)),
                  row(d),
                  pl.BlockSpec((1, hb, d), lambda i, j: (i, jnp.minimum((j + 1) * nh, last), 0)),
                  mod, mod] + [_const_spec(c.shape) for c in consts],
        out_specs=[row(1024), row(SSD_XBC), row(NA_WIDTH), row(NA_WIDTH), row(NA_WIDTH),
                   row(CONF_WIDTH), row(LANES),
                   pl.BlockSpec((1, 2 * SSD_HEADS, tm), lambda i, j: (i, 0, j))],
        out_shape=[jax.ShapeDtypeStruct((b, l, 1024), F32),
                   jax.ShapeDtypeStruct((b, l, SSD_XBC), F32),
                   jax.ShapeDtypeStruct((b, l, NA_WIDTH), BF16),
                   jax.ShapeDtypeStruct((b, l, NA_WIDTH), BF16),
                   jax.ShapeDtypeStruct((b, l, NA_WIDTH), BF16),
                   jax.ShapeDtypeStruct((b, l, CONF_WIDTH), BF16),
                   jax.ShapeDtypeStruct((b, l, LANES), F32),
                   jax.ShapeDtypeStruct((b, 2 * SSD_HEADS, l), F32)],
        scratch_shapes=[pltpu.VMEM((tm + 2 * hb, SSD_XBC), F32),
                        pltpu.VMEM((tm + 2 * hb, CONF_WIDTH), F32),
                        pltpu.VMEM((SUBLANES - 1, tm + 2 * hb - SUBLANES, CONF_WIDTH), F32)],
        compiler_params=_cparams("parallel", "parallel"),
        name="mixin",
    )(x, x, x, shift, scale, *consts)


def _ssd_dir(u_ref, dtc_ref, dtt_ref, al_ref, as_ref, st_ref, y_ref, d, reverse):
    q = SSD_CHUNK
    li = lax.broadcasted_iota(jnp.int32, (q, q), 0)
    si = lax.broadcasted_iota(jnp.int32, (q, q), 1)
    lane = lax.broadcasted_iota(jnp.int32, (q, LANES), 1)
    lo_half = lane < SSD_HEAD_DIM
    mask = (si >= li) if reverse else (si <= li)
    tri = jnp.where(mask, 1.0, 0.0).astype(BF16)
    tri_t = jnp.where((li >= si) if reverse else (li <= si), 1.0, 0.0).astype(BF16)

    a_lane = -jnp.exp(al_ref[...])
    a_sub = -jnp.exp(as_ref[d * SSD_HEADS:(d + 1) * SSD_HEADS, :])
    dtc = dtc_ref[0]
    hi, mid, lo = _split3(dtc * a_lane)
    cs = _dot(tri, hi) + _dot(tri, mid) + _dot(tri, lo)
    dtr = dtt_ref[0, d * SSD_HEADS:(d + 1) * SSD_HEADS, :] * a_sub
    hi, mid, lo = _split3(dtr)
    cs_r = _dot(hi, tri_t) + _dot(mid, tri_t) + _dot(lo, tri_t)
    yield

    hpg = SSD_HEADS // SSD_GROUPS
    gw = hpg * SSD_HEAD_DIM
    for g in range(SSD_GROUPS):
        bm = u_ref[0, :, SSD_D_INNER + g * SSD_STATE:SSD_D_INNER + (g + 1) * SSD_STATE]
        cm = u_ref[0, :, SSD_D_INNER + (SSD_GROUPS + g) * SSD_STATE:
                   SSD_D_INNER + (SSD_GROUPS + g + 1) * SSD_STATE].astype(BF16)
        bm_t = bm.T.astype(BF16)
        scores = _dot(cm, bm_t)
        st = st_ref[d, g]
        y_off = _dot(cm, st.astype(BF16))
        yield
        xdd = []
        tots = []
        for pp in range(hpg // 2):
            p = g * (hpg // 2) + pp
            c0 = 16 * d + 2 * p
            bc0 = jnp.broadcast_to(cs[:, c0:c0 + 1], (q, LANES))
            bc1 = jnp.broadcast_to(cs[:, c0 + 1:c0 + 2], (q, LANES))
            csx = jnp.where(lo_half, bc0, bc1)
            dtx = jnp.where(lo_half, jnp.broadcast_to(dtc[:, c0:c0 + 1], (q, LANES)),
                            jnp.broadcast_to(dtc[:, c0 + 1:c0 + 2], (q, LANES)))
            xd = u_ref[0, :, p * LANES:(p + 1) * LANES] * dtx
            xd_b = xd.astype(BF16)
            dec0 = jnp.exp(jnp.where(mask, bc0 - cs_r[2 * p:2 * p + 1, :], -jnp.inf))
            dec1 = jnp.exp(jnp.where(mask, bc1 - cs_r[2 * p + 1:2 * p + 2, :], -jnp.inf))
            y0 = _dot((scores * dec0).astype(BF16), xd_b)
            y1 = _dot((scores * dec1).astype(BF16), xd_b)
            y = jnp.where(lo_half, y0, y1) + y_off[:, pp * LANES:(pp + 1) * LANES] * jnp.exp(csx)
            y_ref[0, :, p * LANES:(p + 1) * LANES] = y
            totx = csx[0:1, :] if reverse else csx[q - 1:q, :]
            xdd.append((xd * jnp.exp(totx - csx)).astype(BF16))
            tots.append(totx)
            yield
        new_states = _dot(bm_t, jnp.concatenate(xdd, axis=1))
        st_ref[d, g] = st * jnp.exp(jnp.concatenate(tots, axis=1)) + new_states


def _ssd_body(uf_ref, ub_ref, dcf_ref, dcb_ref, dtf_ref, dtb_ref, al_ref, as_ref, h0_ref,
              yf_ref, yb_ref, ht_ref, st_ref):
    i = pl.program_id(1)

    @pl.when(i == 0)
    def _():
        st_ref[...] = h0_ref[0]

    pending = [_ssd_dir(uf_ref, dcf_ref, dtf_ref, al_ref, as_ref, st_ref, yf_ref, 0, False),
               _ssd_dir(ub_ref, dcb_ref, dtb_ref, al_ref, as_ref, st_ref, yb_ref, 1, True)]
    while pending:
        pending = [g for g in pending if next(g, "done") != "done"]

    @pl.when(i == pl.num_programs(1) - 1)
    def _():
        ht_ref[0] = st_ref[...]


def _ssd(u, dt, dtt, a_lane, a_sub, h0):
    b, l, _ = u.shape
    q = SSD_CHUNK
    nc = l // q
    fwd = lambda n: pl.BlockSpec((1, q, n), lambda i, j: (i, j, 0))
    bwd = lambda n: pl.BlockSpec((1, q, n), lambda i, j: (i, nc - 1 - j, 0))
    st_shape = (2, SSD_GROUPS, SSD_STATE, SSD_D_INNER // SSD_GROUPS)
    st_spec = pl.BlockSpec((1,) + st_shape, lambda i, j: (i, 0, 0, 0, 0))
    return pl.pallas_call(
        _ssd_body,
        grid=(b, nc),
        in_specs=[fwd(SSD_XBC), bwd(SSD_XBC), fwd(LANES), bwd(LANES),
                  pl.BlockSpec((1, 2 * SSD_HEADS, q), lambda i, j: (i, 0, j)),
                  pl.BlockSpec((1, 2 * SSD_HEADS, q), lambda i, j: (i, 0, nc - 1 - j)),
                  _const_spec((1, LANES)), _const_spec((2 * SSD_HEADS, 1)), st_spec],
        out_specs=[fwd(SSD_D_INNER), bwd(SSD_D_INNER), st_spec],
        out_shape=[jax.ShapeDtypeStruct((b, l, SSD_D_INNER), F32),
                   jax.ShapeDtypeStruct((b, l, SSD_D_INNER), F32),
                   jax.ShapeDtypeStruct((b,) + st_shape, F32)],
        scratch_shapes=[pltpu.VMEM(st_shape, F32)],
        compiler_params=_cparams("parallel", "arbitrary"),
        name="ssd",
    )(u, u, dt, dt, dtt, dtt, a_lane, a_sub, h0)


def _na_body(*refs, window, n_win, first_key_row=None):
    if window:
        q_ref, k_ref, v_ref, kc_ref, vc_ref, bias_ref, o_ref = refs
    else:
        q_ref, kc_ref, vc_ref, o_ref = refs
    tq = q_ref.shape[1]
    lane = lax.broadcasted_iota(jnp.int32, (tq, LANES), 1)
    lo_half = lane < NA_HEAD_DIM
    if window:
        start = pl.multiple_of(first_key_row(pl.program_id(1)) * GRID_W, GRID_W)
    scores = []
    for h in range(NA_HEADS):
        sl = slice((h // 2) * LANES, (h // 2 + 1) * LANES)
        q2 = q_ref[0, :, sl]
        qm = jnp.where(lo_half if h % 2 == 0 else jnp.logical_not(lo_half), q2, jnp.zeros_like(q2))
        s = _dot_nt(qm, kc_ref[0, :, sl])
        if window:
            s_w = _dot_nt(qm, k_ref[0, pl.ds(start, n_win), sl]) + bias_ref[0, h]
            s = jnp.concatenate([s_w, s], axis=1)
        scores.append(s)
    s_all = jnp.concatenate(scores, axis=0)
    p_all = jnp.exp2(s_all - jnp.max(s_all, axis=-1, keepdims=True))
    inv = 1.0 / jnp.sum(p_all, axis=-1, keepdims=True)
    p_all = p_all.astype(BF16)
    for hp in range(NA_HEADS // 2):
        sl = slice(hp * LANES, (hp + 1) * LANES)
        outs = []
        for par in range(2):
            h = 2 * hp + par
            p = p_all[h * tq:(h + 1) * tq]
            o = _dot(p[:, n_win:], vc_ref[0, :, sl])
            if window:
                o = o + _dot(p[:, :n_win], v_ref[0, pl.ds(start, n_win), sl])
            outs.append(o * inv[h * tq:(h + 1) * tq])
        o_ref[0, :, sl] = jnp.where(lo_half, outs[0], outs[1]).astype(o_ref.dtype)


def _na_geometry(rows):
    wr = min(NA_WIN_ROWS, rows)
    qrows = 2 if rows % 2 == 0 else 1
    per_vreg = LANES // GRID_W
    uw = -(-(wr + qrows - 1) // per_vreg) * per_vreg
    return qrows, wr, min(rows, uw)


def _na_bias_table(rpb, rows):
    qrows, wr, uw = _na_geometry(rows)
    steps = rows // qrows
    r_all = np.arange(rows).reshape(steps, qrows)
    rs_all = np.clip(r_all - wr // 2, 0, rows - wr)
    ks_all = np.clip(np.arange(steps) * qrows - wr // 2, 0, rows - uw)
    krow = ks_all[:, None, None] + np.arange(uw)[None, None, :]
    row_ok = (krow >= rs_all[:, :, None]) & (krow < rs_all[:, :, None] + wr)
    drow_all = np.where(row_ok, krow - r_all[:, :, None] + NA_WIN_ROWS - 1, -1)
    uniq, var_of_step = np.unique(drow_all.reshape(steps, -1), axis=0, return_inverse=True)
    n_var = uniq.shape[0]
    uniq = uniq.reshape(n_var, qrows, uw)
    qcol = np.arange(GRID_W)
    kcol = np.arange(GRID_W)
    col_start = np.clip(qcol - NA_WIN_COLS // 2, 0, GRID_W - NA_WIN_COLS)
    rel = kcol[None, :] - col_start[:, None]
    col_ok = (rel >= 0) & (rel < NA_WIN_COLS)
    dcol = np.clip(kcol[None, :] - qcol[:, None] + NA_WIN_COLS - 1, 0, 2 * NA_WIN_COLS - 2)
    n_dcol = 2 * NA_WIN_COLS - 1
    onehot = jnp.asarray(dcol[None, :, :] == np.arange(n_dcol)[:, None, None], F32)
    sel = rpb.astype(F32)[:, np.maximum(uniq, 0).reshape(-1), :]
    tab = jnp.einsum("hrc,cqk->hqrk", sel, onehot, precision=lax.Precision.HIGHEST)
    tab = tab.reshape(NA_HEADS, GRID_W, n_var, qrows, uw, GRID_W)
    ok = col_ok[None, :, None, None, None, :] & (uniq >= 0)[None, None, :, :, :, None]
    tab = jnp.where(ok, tab * LOG2_E, NEG_INF)
    tab = jnp.transpose(tab, (2, 0, 3, 1, 4, 5)).reshape(n_var, NA_HEADS, qrows * GRID_W, uw * GRID_W)
    return tab, jnp.asarray(var_of_step.reshape(-1), jnp.int32)


def _na_latent(q, k, v, kc, vc, rpb):
    b, l, w = q.shape
    rows = l // GRID_W
    qrows, wr, uw = _na_geometry(rows)
    tab, var_of_step = _na_bias_table(rpb, rows)
    tq = qrows * GRID_W
    n_win = uw * GRID_W
    lc = kc.shape[1]
    full = lambda n: pl.BlockSpec((1, n, w), lambda i, r, vr: (i, 0, 0))
    grid_spec = pltpu.PrefetchScalarGridSpec(
        num_scalar_prefetch=1,
        grid=(b, rows // qrows),
        in_specs=[pl.BlockSpec((1, tq, w), lambda i, r, vr: (i, r, 0)),
                  full(l), full(l), full(lc), full(lc),
                  pl.BlockSpec((1, NA_HEADS, tq, n_win), lambda i, r, vr: (vr[r], 0, 0, 0))],
        out_specs=pl.BlockSpec((1, tq, w), lambda i, r, vr: (i, r, 0)),
    )

    def body(vr_ref, *refs):
        _na_body(*refs, window=True, n_win=n_win, first_key_row=lambda step: jnp.clip(
            step * qrows - wr // 2, 0, rows - uw))

    return pl.pallas_call(
        body,
        grid_spec=grid_spec,
        out_shape=jax.ShapeDtypeStruct((b, l, w), BF16),
        compiler_params=_cparams("parallel", "arbitrary"),
        name="na_latent",
    )(var_of_step, q, k, v, kc, vc, tab)


def _na_context(q, kc, vc):
    b, lc, w = q.shape
    tq = GRID_W
    full = pl.BlockSpec((1, lc, w), lambda i, j: (i, 0, 0))
    return pl.pallas_call(
        functools.partial(_na_body, window=False, n_win=0),
        grid=(b, lc // tq),
        in_specs=[pl.BlockSpec((1, tq, w), lambda i, j: (i, j, 0)), full, full],
        out_specs=pl.BlockSpec((1, tq, w), lambda i, j: (i, j, 0)),
        out_shape=jax.ShapeDtypeStruct((b, lc, w), BF16),
        compiler_params=_cparams("parallel", "arbitrary"),
        name="na_context",
    )(q, kc, vc)


def _tail_body(x_ref, mod_ref, gmix_ref, gffn_ref, yf_ref, yb_ref, xs_ref, z_ref, na_ref, cf_ref,
               dskip_ref, snorm_ref, wgate_ref, wssd_ref, wna_ref, wcf_ref, wo_ref, *rest, router):
    if router:
        wr_ref, xo_ref, h2_ref, lg_ref = rest
    else:
        xo_ref, h2_ref = rest
    d = x_ref.shape[2]
    x = x_ref[0]
    mod = mod_ref[0]
    h = (_rms(x, gmix_ref[...]) * (1.0 + mod[1:2]) + mod[0:1]).astype(BF16)
    y = (yf_ref[0] + yb_ref[0] + dskip_ref[...] * xs_ref[0]) * _silu(z_ref[0])
    ssd = _dot(_rms(y, snorm_ref[...]).astype(BF16), wssd_ref[...])
    merged = _sigmoid(_dot(h, wgate_ref[:, 0:d])) * ssd
    merged = merged + _sigmoid(_dot(h, wgate_ref[:, d:2 * d])) * _dot(na_ref[0], wna_ref[...])
    merged = merged + _sigmoid(_dot(h, wgate_ref[:, 2 * d:3 * d])) * _dot(cf_ref[0], wcf_ref[...])
    xn = x + mod[2:3] * _dot(merged.astype(BF16), wo_ref[...])
    xo_ref[0] = xn
    h2 = _rms(xn, gffn_ref[...]) * (1.0 + mod[4:5]) + mod[3:4]
    if router:
        _to_token_tiles(h2_ref.at[0], h2)
    else:
        h2_ref[0] = h2.astype(BF16)
    if router:
        hi = h2.astype(BF16)
        mid = (h2 - hi.astype(F32)).astype(BF16)
        r = _dot(hi, wr_ref[...])
        lg_ref[0] = _top2_route(r[:, :LANES] + r[:, LANES:] + _dot(mid, wr_ref[:, :LANES]))


def _tail(x, mod, gmix, gffn, yf, yb, u, z, na, cf, dskip, snorm, wgate, wssd, wna, wcf, wo, tm,
          w_router=None):
    b, l, d = x.shape
    row = lambda n: pl.BlockSpec((1, tm, n), lambda i, j: (i, j, 0))
    ins = [x, mod, gmix, gffn, yf, yb, u, z, na, cf, dskip, snorm, wgate, wssd, wna, wcf, wo]
    specs = [row(d), pl.BlockSpec((1, 8, d), lambda i, j: (i, 0, 0)), _const_spec((1, d)),
             _const_spec((1, d)), row(d), row(d), row(SSD_D_INNER), row(d), row(NA_WIDTH),
             row(CONF_WIDTH), _const_spec((1, d)), _const_spec((1, d)), _const_spec(wgate.shape),
             _const_spec(wssd.shape), _const_spec(wna.shape), _const_spec(wcf.shape),
             _const_spec(wo.shape)]
    out_specs = [row(d), row(d)]
    out_shape = [jax.ShapeDtypeStruct((b, l, d), F32), jax.ShapeDtypeStruct((b, l, d), BF16)]
    if w_router is not None:
        assert d == SUBLANES * LANES
        out_specs[1] = pl.BlockSpec((1, tm * SUBLANES, LANES), lambda i, j: (i, j, 0))
        out_shape[1] = jax.ShapeDtypeStruct((b, l * SUBLANES, LANES), F32)
        ins.append(w_router)
        specs.append(_const_spec(w_router.shape))
        out_specs.append(row(LANES))
        out_shape.append(jax.ShapeDtypeStruct((b, l, LANES), F32))
    return pl.pallas_call(
        functools.partial(_tail_body, router=w_router is not None),
        grid=(b, l // tm),
        in_specs=specs,
        out_specs=out_specs,
        out_shape=out_shape,
        compiler_params=_cparams("parallel", "parallel"),
        name="tail",
    )(*ins)


def _ffn_epilogue(x_ref, mod_ref, fn_ref, acc, o_ref, final):
    xn = x_ref[0] + mod_ref[0][5:6] * acc
    if final:
        xn = _rms(xn, fn_ref[...])
    o_ref[0] = xn


def _ffn_body(h_ref, x_ref, mod_ref, fn_ref, wg_ref, wu_ref, wd_ref, o_ref, acc_ref, *, final):
    k = pl.program_id(2)

    @pl.when(k == 0)
    def _():
        acc_ref[...] = jnp.zeros_like(acc_ref)

    h = h_ref[0]
    a = (_silu(_dot(h, wg_ref[...])) * _dot(h, wu_ref[...])).astype(BF16)
    acc_ref[...] += _dot(a, wd_ref[...])

    @pl.when(k == pl.num_programs(2) - 1)
    def _():
        _ffn_epilogue(x_ref, mod_ref, fn_ref, acc_ref[...], o_ref, final)


def _ffn(h, x, mod, fnorm, wg, wu, wd, tm, tf, final):
    b, l, d = x.shape
    ff = wg.shape[1]
    row = pl.BlockSpec((1, tm, d), lambda i, j, k: (i, j, 0))
    return pl.pallas_call(
        functools.partial(_ffn_body, final=final),
        grid=(b, l // tm, ff // tf),
        in_specs=[row, row, pl.BlockSpec((1, 8, d), lambda i, j, k: (i, 0, 0)),
                  pl.BlockSpec((1, d), lambda i, j, k: (0, 0)),
                  pl.BlockSpec((d, tf), lambda i, j, k: (0, k)),
                  pl.BlockSpec((d, tf), lambda i, j, k: (0, k)),
                  pl.BlockSpec((tf, d), lambda i, j, k: (k, 0))],
        out_specs=row,
        out_shape=jax.ShapeDtypeStruct((b, l, d), F32),
        scratch_shapes=[pltpu.VMEM((tm, d), F32)],
        compiler_params=_cparams("parallel", "parallel", "arbitrary"),
        name="ffn",
    )(h, x, mod, fnorm, wg, wu, wd)


def _top2_route(logits):
    lane = lax.broadcasted_iota(jnp.int32, logits.shape, 1)
    lg = jnp.where(lane < N_EXPERTS, logits, -jnp.inf)
    v1 = jnp.max(lg, axis=-1, keepdims=True)
    i1 = jnp.min(jnp.where(lg == v1, lane, LANES), axis=-1, keepdims=True)
    lg2 = jnp.where(lane == i1, -jnp.inf, lg)
    v2 = jnp.max(lg2, axis=-1, keepdims=True)
    i2 = jnp.min(jnp.where(lg2 == v2, lane, LANES), axis=-1, keepdims=True)
    e2 = jnp.exp(v2 - v1)
    w1 = 1.0 / (1.0 + e2)
    w2 = e2 / (1.0 + e2)
    rec = jnp.where(lane == 0, i1.astype(F32), 0.0) + jnp.where(lane == 1, i2.astype(F32), 0.0)
    return rec + jnp.where(lane == 2, w1, 0.0) + jnp.where(lane == 3, w2, 0.0)


def _to_token_tiles(ref, val):
    n = val.shape[0]
    for s in range(SUBLANES):
        ref[pl.ds(s, n, stride=SUBLANES), :] = val[:, s * LANES:(s + 1) * LANES]


def _from_token_tiles(ref, first, n):
    return jnp.concatenate([ref[pl.ds(first * SUBLANES + s, n, stride=SUBLANES), :] for s in range(SUBLANES)],
                           axis=1)


def _token_tile(ref, idx):
    return ref.at[pl.ds(pl.multiple_of(idx * SUBLANES, SUBLANES), SUBLANES), :]


def _dispatch_body(lb_ref, pos_ref, h_ref, xs_hbm, zero_buf, sem, *, tc, tm):
    @pl.when(pl.program_id(0) == 0)
    def _():
        zero_buf[...] = jnp.zeros_like(zero_buf)
        fills = [pltpu.make_async_copy(
            zero_buf, xs_hbm.at[pl.ds(pl.multiple_of(lb_ref[e] * (tm * SUBLANES), tm * SUBLANES),
                                      tm * SUBLANES), :], sem) for e in range(lb_ref.shape[0])]
        for f in fills:
            f.start()
            f.wait()

    def body(r, carry):
        src = _token_tile(h_ref, r)
        for k in range(2):
            pltpu.make_async_copy(src, _token_tile(xs_hbm, pos_ref[0, k, r]), sem).start()
        return carry

    lax.fori_loop(0, tc, body, 0, unroll=4)
    for _ in range(2):
        pltpu.make_async_copy(h_ref, xs_hbm.at[pl.ds(0, tc * SUBLANES), :], sem).wait()


def _dispatch(h_tiles, pos, clear_blocks, n_rows, tc, tm):
    t = h_tiles.shape[0] // SUBLANES
    grid_spec = pltpu.PrefetchScalarGridSpec(
        num_scalar_prefetch=1,
        grid=(t // tc,),
        in_specs=[pl.BlockSpec((1, 2, tc), lambda j, lb: (j, 0, 0), memory_space=pltpu.SMEM),
                  pl.BlockSpec((tc * SUBLANES, LANES), lambda j, lb: (j, 0))],
        out_specs=pl.BlockSpec(memory_space=pl.ANY),
        scratch_shapes=[pltpu.VMEM((tm * SUBLANES, LANES), F32), pltpu.SemaphoreType.DMA(())],
    )
    return pl.pallas_call(
        functools.partial(_dispatch_body, tc=tc, tm=tm),
        grid_spec=grid_spec,
        out_shape=jax.ShapeDtypeStruct((n_rows * SUBLANES, LANES), F32),
        compiler_params=_cparams("arbitrary"),
        name="moe_dispatch",
    )(clear_blocks, pos, h_tiles)


def _gmm_body(be_ref, nu_ref, x_ref, wg_ref, wu_ref, wd_ref, y_ref, *, tm, n_chunk):
    j = pl.program_id(0)

    @pl.when(j < nu_ref[0])
    def _():
        h = _from_token_tiles(x_ref, 0, tm).astype(BF16)
        ff = wg_ref.shape[1]
        tf = ff // n_chunk
        acc = None
        for c in range(n_chunk):
            cs = slice(c * tf, (c + 1) * tf)
            a = (_silu(_dot(h, wg_ref[:, cs])) * _dot(h, wu_ref[:, cs])).astype(BF16)
            part = _dot(a, wd_ref[cs, :])
            acc = part if acc is None else acc + part
        _to_token_tiles(y_ref, acc)

    @pl.when(j >= nu_ref[0])
    def _():
        y_ref[...] = jnp.zeros_like(y_ref)


def _gmm(xs, block_expert, n_used, wg, wu, wd, tm):
    ne, d, ff = wg.shape
    n_blocks = xs.shape[0] // (tm * SUBLANES)
    wspec = lambda r, c: pl.BlockSpec((None, r, c), lambda j, be, nu: (be[j], 0, 0))
    grid_spec = pltpu.PrefetchScalarGridSpec(
        num_scalar_prefetch=2,
        grid=(n_blocks,),
        in_specs=[pl.BlockSpec((tm * SUBLANES, LANES), lambda j, be, nu: (jnp.minimum(j, nu[0] - 1), 0)),
                  wspec(d, ff), wspec(d, ff), wspec(ff, d)],
        out_specs=pl.BlockSpec((tm * SUBLANES, LANES), lambda j, be, nu: (j, 0)),
    )
    return pl.pallas_call(
        functools.partial(_gmm_body, tm=tm, n_chunk=4),
        grid_spec=grid_spec,
        out_shape=jax.ShapeDtypeStruct(xs.shape, F32),
        compiler_params=_cparams("arbitrary"),
        name="moe_gmm",
    )(block_expert, n_used, xs, wg, wu, wd)


def _combine_body(pos_ref, posn_ref, rt_ref, x_ref, mod_ref, fn_ref, y_hbm, o_ref, buf, sem, *, tc, final):
    nl = pl.num_programs(1)
    j = pl.program_id(0) * nl + pl.program_id(1)
    n = pl.num_programs(0) * nl
    slot = j % 2

    def start(ids_ref, s):
        def body(r, carry):
            for k in range(2):
                pltpu.make_async_copy(_token_tile(y_hbm, ids_ref[0, k, r]),
                                      _token_tile(buf.at[s], k * tc + r), sem.at[s]).start()
            return carry
        lax.fori_loop(0, tc, body, 0, unroll=4)

    @pl.when(j == 0)
    def _():
        start(pos_ref, 0)

    pltpu.make_async_copy(y_hbm.at[pl.ds(0, 2 * tc * SUBLANES), :], buf.at[slot], sem.at[slot]).wait()

    @pl.when(j + 1 < n)
    def _():
        start(posn_ref, 1 - slot)

    rt = rt_ref[0]
    cur = buf.at[slot]
    f = rt[:, 2:3] * _from_token_tiles(cur, 0, tc) + rt[:, 3:4] * _from_token_tiles(cur, tc, tc)
    _ffn_epilogue(x_ref, mod_ref, fn_ref, f, o_ref, final)


def _combine(route, x, mod, fnorm, pos, y_sorted, tc, final):
    b, l, d = x.shape
    nl = l // tc
    n = b * nl
    ids = lambda shift: pl.BlockSpec((1, 2, tc), lambda i, j: (jnp.minimum(i * nl + j + shift, n - 1), 0, 0),
                                     memory_space=pltpu.SMEM)
    row = lambda w: pl.BlockSpec((1, tc, w), lambda i, j: (i, j, 0))
    return pl.pallas_call(
        functools.partial(_combine_body, tc=tc, final=final),
        grid=(b, nl),
        in_specs=[ids(0), ids(1), row(LANES), row(d), pl.BlockSpec((1, 8, d), lambda i, j: (i, 0, 0)),
                  _const_spec((1, d)), pl.BlockSpec(memory_space=pl.ANY)],
        out_specs=row(d),
        out_shape=jax.ShapeDtypeStruct((b, l, d), F32),
        scratch_shapes=[pltpu.VMEM((2, 2 * tc * SUBLANES, LANES), F32), pltpu.SemaphoreType.DMA((2,))],
        compiler_params=_cparams("arbitrary", "arbitrary"),
        name="moe_combine",
    )(pos, pos, route, x, mod, fnorm, y_sorted)


def _moe_sparse(h2, route, x, mod, fnorm, wg, wu, wd, tm, tc, final):
    b, l, d = x.shape
    t = b * l
    ne = wg.shape[0]
    rt = route.reshape(t, LANES)
    experts = jnp.concatenate([rt[:, 0], rt[:, 1]]).astype(jnp.int32)
    onehot = (experts[:, None] == jnp.arange(ne, dtype=jnp.int32)[None, :]).astype(jnp.int32)
    csum = jnp.cumsum(onehot, axis=0)
    rank = jnp.sum((csum - onehot) * onehot, axis=1)
    counts = csum[-1]
    padded = ((counts + tm - 1) // tm) * tm
    ends = jnp.cumsum(padded)
    pos = jnp.sum(onehot * (ends - padded)[None, :], axis=1) + rank
    n_blocks = (2 * t) // tm + ne
    block_start = jnp.arange(n_blocks, dtype=jnp.int32) * tm
    block_expert = jnp.minimum(jnp.sum(block_start[:, None] >= ends[None, :], axis=1), ne - 1).astype(jnp.int32)
    n_used = (ends[-1:] // tm).astype(jnp.int32)
    tiled = lambda n: jnp.transpose(pos.astype(jnp.int32).reshape(2, t // n, n), (1, 0, 2))
    pos2 = tiled(tc)
    td = TILE_DISPATCH if t % TILE_DISPATCH == 0 else tc
    clear = jnp.concatenate([ends // tm - 1, n_used[0] + jnp.arange(ne, dtype=jnp.int32)])
    clear = jnp.clip(clear, 0, n_blocks - 1).astype(jnp.int32)
    xs = _dispatch(h2.reshape(t * SUBLANES, LANES), tiled(td), clear, n_blocks * tm, td, tm)
    y_sorted = _gmm(xs, block_expert, n_used, wg, wu, wd, tm)
    return _combine(route, x, mod, fnorm, pos2, y_sorted, tc, final)


TILE_LATENT = 512
TILE_CONTEXT = 256
TILE_COMBINE = 256
TILE_DISPATCH = 1024


def _row_tile(l, want):
    return want if l % want == 0 else l


def kernel(x, c, ctx, c_ctx, ada_w, ada_b, norm_mix, norm_ffn, w_in, ssd_conv_w, ssd_conv_b, ssd_a_log, ssd_dt_bias, ssd_d, ssd_norm, ssd_out, na_rpb, na_out, conf_conv_w, conf_conv_b, conf_ln_g, conf_ln_b, conf_out, w_o, ffn_gate, ffn_up, ffn_down, moe_router, moe_gate, moe_up, moe_down, final_norm):
    depth = w_in.shape[0]
    b, l, d = x.shape
    lc = ctx.shape[1]
    nh = SSD_HEADS

    cc = jnp.zeros((8, d), F32).at[:b].set(c).at[b].set(c_ctx)
    mods = _ada(cc, ada_w, ada_b)
    fnorm = final_norm.reshape(1, d)

    offs = np.cumsum((0, SSD_D_INNER, SSD_XBC, nh, nh, NA_WIDTH, NA_WIDTH, NA_WIDTH, 2 * CONF_WIDTH, 3 * d))
    seg = lambda w, i: w[:, offs[i]:offs[i + 1]]

    for layer in range(depth):
        need_ctx = layer < depth - 1
        last = layer == depth - 1
        m6 = mods[layer].reshape(8, 6, d)
        mod_l = jnp.zeros((b, 8, d), F32).at[:, :6].set(m6[:b])
        mod_c = jnp.broadcast_to(jnp.zeros((8, d), F32).at[:6].set(m6[b]), (b, 8, d))

        wl = w_in[layer]
        w_main = jnp.concatenate([seg(wl, i) for i in (0, 1, 4, 5, 6, 7)], axis=1).astype(BF16)
        w_dt = jnp.concatenate([seg(wl, 2), seg(wl, 3)], axis=1)
        w_dt_pad = jnp.zeros((d, LANES), F32).at[:, :2 * nh].set(w_dt).astype(BF16)
        w_dtt = w_dt.T.astype(BF16)
        w_gate = seg(wl, 8).astype(BF16)
        dtb = ssd_dt_bias[layer].reshape(2 * nh)
        b_lane = jnp.zeros((1, LANES), F32).at[0, :2 * nh].set(dtb)
        b_sub = dtb.reshape(2 * nh, 1)
        alog = ssd_a_log[layer].reshape(2 * nh)
        a_lane = jnp.zeros((1, LANES), F32).at[0, :2 * nh].set(alog)
        a_sub = alog.reshape(2 * nh, 1)
        dskip = jnp.repeat(ssd_d[layer], SSD_HEAD_DIM).reshape(1, SSD_D_INNER)
        gmix = norm_mix[layer].reshape(1, d)
        gffn = norm_ffn[layer].reshape(1, d)
        snorm = ssd_norm[layer].reshape(1, SSD_D_INNER)
        wssd = ssd_out[layer].astype(BF16)
        wna = na_out[layer].astype(BF16)
        wcf = conf_out[layer].astype(BF16)
        wo = w_o[layer].astype(BF16)

        def mixer_inputs(xx, mod, tm):
            return _mixin(xx, mod[:, 0:1], mod[:, 1:2], gmix, w_main, w_dt_pad, w_dtt, b_lane, b_sub,
                          ssd_conv_w[layer], ssd_conv_b[layer], conf_conv_w[layer], conf_conv_b[layer],
                          conf_ln_g[layer], conf_ln_b[layer], tm)

        z_c, u_c, q_c, k_c, v_c, cf_c, dt_c, dtt_c = mixer_inputs(ctx, mod_c, _row_tile(lc, TILE_CONTEXT))
        z_l, u_l, q_l, k_l, v_l, cf_l, dt_l, dtt_l = mixer_inputs(x, mod_l, _row_tile(l, TILE_LATENT))

        h0 = jnp.zeros((b, 2, SSD_GROUPS, SSD_STATE, SSD_D_INNER // SSD_GROUPS), F32)
        yf_c, yb_c, st = _ssd(u_c, dt_c, dtt_c, a_lane, a_sub, h0)
        yf_l, yb_l, _ = _ssd(u_l, dt_l, dtt_l, a_lane, a_sub, st)

        na_l = _na_latent(q_l, k_l, v_l, k_c, v_c, na_rpb[layer])

        is_moe = layer % 2 == 1
        w_router = None
        if is_moe:
            wr_f = jnp.zeros((d, LANES), F32).at[:, :N_EXPERTS].set(moe_router[layer // 2])
            wr_hi = wr_f.astype(BF16)
            w_router = jnp.concatenate([wr_hi, (wr_f - wr_hi.astype(F32)).astype(BF16)], axis=1)
        tail = functools.partial(_tail, gmix=gmix, gffn=gffn, dskip=dskip, snorm=snorm, wgate=w_gate,
                                 wssd=wssd, wna=wna, wcf=wcf, wo=wo)
        outs = tail(x, mod_l, yf=yf_l, yb=yb_l, u=u_l, z=z_l, na=na_l, cf=cf_l, tm=_row_tile(l, TILE_LATENT),
                    w_router=w_router)
        x_mid, h2_l = outs[0], outs[1]
        if need_ctx:
            na_c = _na_context(q_c, k_c, v_c)
            outs_c = tail(ctx, mod_c, yf=yf_c, yb=yb_c, u=u_c, z=z_c, na=na_c, cf=cf_c,
                          tm=_row_tile(lc, TILE_CONTEXT), w_router=w_router)
            ctx_mid, h2_c = outs_c[0], outs_c[1]

        i = layer // 2
        if not is_moe:
            wg, wu, wd = (t[i].astype(BF16) for t in (ffn_gate, ffn_up, ffn_down))
            tf = wg.shape[1] // 2
            x = _ffn(h2_l, x_mid, mod_l, fnorm, wg, wu, wd, _row_tile(l, TILE_LATENT), tf, last)
            if need_ctx:
                ctx = _ffn(h2_c, ctx_mid, mod_c, fnorm, wg, wu, wd, _row_tile(lc, TILE_CONTEXT), tf, False)
        else:
            wg, wu, wd = (t[i].astype(BF16) for t in (moe_gate, moe_up, moe_down))
            x = _moe_sparse(h2_l, outs[2], x_mid, mod_l, fnorm, wg, wu, wd, TILE_LATENT,
                            _row_tile(l, TILE_COMBINE), last)
            if need_ctx:
                ctx = _moe_sparse(h2_c, outs_c[2], ctx_mid, mod_c, fnorm, wg, wu, wd, TILE_LATENT,
                                  _row_tile(lc, TILE_COMBINE), False)
    return x
```

```python
import functools

import numpy as np
import jax
import jax.numpy as jnp
from jax import lax
from jax.experimental import pallas as pl
from jax.experimental.pallas import tpu as pltpu

F32 = jnp.float32
BF16 = jnp.bfloat16

NORM_EPS = 1e-6
NEG_INF = -1e30
GRID_W = 64

SSD_HEADS = 16
SSD_HEAD_DIM = 64
SSD_GROUPS = 2
SSD_STATE = 128
SSD_CHUNK = 128
SSD_D_INNER = SSD_HEADS * SSD_HEAD_DIM
SSD_XBC = SSD_D_INNER + 2 * SSD_GROUPS * SSD_STATE
SSD_CONV = 5

NA_HEADS = 8
NA_HEAD_DIM = 64
NA_WIDTH = NA_HEADS * NA_HEAD_DIM
NA_WIN_ROWS = 8
NA_WIN_COLS = 16
LOG2_E = 1.4426950408889634
NA_Q_SCALE = NA_HEAD_DIM ** -0.5 * LOG2_E

CONF_WIDTH = 512
CONF_KERNEL = 31
N_EXPERTS = 8

LANES = 128
SUBLANES = 8
VMEM_LIMIT = 56 << 20


def _cparams(*sem):
    return pltpu.CompilerParams(dimension_semantics=sem, vmem_limit_bytes=VMEM_LIMIT)


def _sigmoid(x):
    return 1.0 / (1.0 + jnp.exp(-x))


def _silu(x):
    return x * _sigmoid(x)


def _softplus(x):
    return jnp.maximum(x, 0.0) + jnp.log1p(jnp.exp(-jnp.abs(x)))


def _rms(x, g):
    return x * lax.rsqrt(jnp.mean(x * x, axis=-1, keepdims=True) + NORM_EPS) * g


def _dot(a, b):
    return jnp.dot(a, b, preferred_element_type=F32)


def _dot_nt(a, b):
    return lax.dot_general(a, b, (((1,), (1,)), ((), ())), preferred_element_type=F32)


def _split3(a):
    hi = a.astype(BF16)
    r1 = a - hi.astype(F32)
    mid = r1.astype(BF16)
    lo = (r1 - mid.astype(F32)).astype(BF16)
    return hi, mid, lo


def _const_spec(shape):
    nd = len(shape)
    return pl.BlockSpec(shape, lambda *_: (0,) * nd, pipeline_mode=pl.Buffered(1))


def _ada_body(c_ref, w_ref, b_ref, o_ref):
    s = _silu(c_ref[...]).astype(BF16)
    o_ref[...] = _dot(s, w_ref[...].astype(BF16)) + b_ref[...]


def _ada(cc, ada_w, ada_b):
    depth, d, n = ada_w.shape
    tn = 1536
    return pl.pallas_call(
        _ada_body,
        grid=(depth, n // tn),
        in_specs=[pl.BlockSpec((8, d), lambda l, j: (0, 0)),
                  pl.BlockSpec((None, d, tn), lambda l, j: (l, 0, j)),
                  pl.BlockSpec((None, 1, tn), lambda l, j: (l, 0, j))],
        out_specs=pl.BlockSpec((None, 8, tn), lambda l, j: (l, 0, j)),
        out_shape=jax.ShapeDtypeStruct((depth, 8, n), F32),
        compiler_params=_cparams("arbitrary", "arbitrary"),
        name="ada",
    )(cc, ada_w, ada_b.reshape(depth, 1, n))


_SEG_Z = (0, 1024)
_SEG_XBC = (1024, 2560)
_SEG_Q = (2560, 3072)
_SEG_K = (3072, 3584)
_SEG_V = (3584, 4096)
_SEG_GLU = (4096, 5120)

CONV_HALO = 16


def _dwconv_tile(buf, shifted, w_ref, b_ref, taps, tl, strip, emit):
    if shifted is not None:
        for s in range(1, SUBLANES):
            shifted[s - 1] = buf[s:s + shifted.shape[1], :]

    def window(off):
        if shifted is not None and off % SUBLANES:
            a = off - off % SUBLANES
            return shifted[off % SUBLANES - 1, a:a + strip, :]
        return buf[off:off + strip, :]

    def tap_weight(j):
        wj = w_ref[j * SUBLANES:(j + 1) * SUBLANES, :]
        return jnp.concatenate([wj] * (strip // SUBLANES), axis=0)

    first = CONV_HALO - taps // 2
    for r0 in range(0, tl, strip):
        acc = b_ref[...] + tap_weight(0) * window(r0 + first)
        for j in range(1, taps):
            acc = acc + tap_weight(j) * window(r0 + first + j)
        emit(r0, acc)


def _mixin_body(xp_ref, x_ref, xn_ref, sh_ref, sc_ref, g_ref, w_ref, wdt_ref, wdtt_ref, bl_ref, bs_ref,
                c5w_ref, c5b_ref, c31w_ref, c31b_ref, lng_ref, lnb_ref,
                z_ref, u_ref, q_ref, k_ref, v_ref, cf_ref, dt_ref, dtt_ref, buf5, buf31, sh31, *, tm):
    j = pl.program_id(1)
    hb = CONV_HALO
    norm_mod = lambda t: (_rms(t, g_ref[...]) * (1.0 + sc_ref[0]) + sh_ref[0]).astype(BF16)
    h = norm_mod(x_ref[0])
    hh = norm_mod(jnp.concatenate([xp_ref[0], xn_ref[0]], axis=0))
    has_prev = j > 0
    has_next = j < pl.num_programs(1) - 1

    cw = 2 * LANES

    def glu_chunk(c):
        def run():
            a0 = _SEG_GLU[0] + c * cw
            g0 = a0 + CONF_WIDTH
            glu = lambda t: _dot(t, w_ref[:, a0:a0 + cw]) * _sigmoid(_dot(t, w_ref[:, g0:g0 + cw]))
            halo = glu(hh)
            cols = slice(c * cw, (c + 1) * cw)
            buf31[0:hb, cols] = jnp.where(has_prev, halo[0:hb], 0.0)
            buf31[hb:hb + tm, cols] = glu(h)
            buf31[hb + tm:, cols] = jnp.where(has_next, halo[hb:], 0.0)
        return run

    def proj_chunk(o_ref, seg, c, scale=None):
        def run():
            r = _dot(h, w_ref[:, seg[0] + c * cw:seg[0] + (c + 1) * cw])
            o_ref[0, :, c * cw:(c + 1) * cw] = (r if scale is None else r * scale).astype(o_ref.dtype)
        return run

    def dt_chunks():
        dt_ref[0] = _softplus(_dot(h, wdt_ref[...]) + bl_ref[...])
        dtt_ref[0] = _softplus(_dot_nt(wdtt_ref[...], h) + bs_ref[...])

    during_conv5 = [glu_chunk(c) for c in range(CONF_WIDTH // cw)]
    during_conv31 = ([proj_chunk(z_ref, _SEG_Z, c) for c in range(4)]
                     + [proj_chunk(q_ref, _SEG_Q, c, NA_Q_SCALE) for c in range(2)]
                     + [proj_chunk(k_ref, _SEG_K, c) for c in range(2)]
                     + [proj_chunk(v_ref, _SEG_V, c) for c in range(2)] + [dt_chunks])

    def run_some(pending, n):
        for _ in range(min(n, len(pending))):
            pending.pop(0)()

    halo = _dot(hh, w_ref[:, _SEG_XBC[0]:_SEG_XBC[1]])
    buf5[0:hb, :] = jnp.where(has_prev, halo[0:hb], 0.0)
    buf5[hb:hb + tm, :] = _dot(h, w_ref[:, _SEG_XBC[0]:_SEG_XBC[1]])
    buf5[hb + tm:, :] = jnp.where(has_next, halo[hb:], 0.0)

    def emit_u(r0, acc):
        u_ref[0, r0:r0 + 16, :] = _silu(acc)
        if r0 % 64 == 0:
            run_some(during_conv5, 1)

    _dwconv_tile(buf5, None, c5w_ref, c5b_ref, SSD_CONV, tm, 16, emit_u)
    run_some(during_conv5, len(during_conv5))

    def emit_cf(r0, acc):
        xc = acc - jnp.mean(acc, axis=-1, keepdims=True)
        var = jnp.mean(xc * xc, axis=-1, keepdims=True)
        y = xc * lax.rsqrt(var + NORM_EPS) * lng_ref[...] + lnb_ref[...]
        cf_ref[0, r0:r0 + 32, :] = _silu(y).astype(BF16)
        run_some(during_conv31, 2)

    _dwconv_tile(buf31, sh31, c31w_ref, c31b_ref, CONF_KERNEL, tm, 32, emit_cf)
    run_some(during_conv31, len(during_conv31))


def _mixin(x, shift, scale, g, w_main, w_dt, w_dtt, b_lane, b_sub, c5w, c5b, c31w, c31b, ln_g, ln_b, tm):
    b, l, d = x.shape
    hb = CONV_HALO
    nh = tm // hb
    last = l // hb - 1
    row = lambda n: pl.BlockSpec((1, tm, n), lambda i, j: (i, j, 0))
    mod = pl.BlockSpec((1, 1, d), lambda i, j: (i, 0, 0))
    rep = lambda w: jnp.repeat(w, SUBLANES, axis=0)
    consts = [g, w_main, w_dt, w_dtt, b_lane, b_sub, rep(c5w), c5b.reshape(1, -1), rep(c31w),
              c31b.reshape(1, -1), ln_g.reshape(1, -1), ln_b.reshape(1, -1)]
    return pl.pallas_call(
        functools.partial(_mixin_body, tm=tm),
        grid=(b, l // tm),
        in_specs=[pl.BlockSpec((1, hb, d), lambda i, j: (i, jnp.maximum(j * nh - 1, 0), 0)),
                  row(d),
                  pl.BlockSpec((1, hb, d), lambda i, j: (i, jnp.minimum((j + 1) * nh, last), 0)),
                  mod, mod] + [_const_spec(c.shape) for c in consts],
        out_specs=[row(1024), row(SSD_XBC), row(NA_WIDTH), row(NA_WIDTH), row(NA_WIDTH),
                   row(CONF_WIDTH), row(LANES),
                   pl.BlockSpec((1, 2 * SSD_HEADS, tm), lambda i, j: (i, 0, j))],
        out_shape=[jax.ShapeDtypeStruct((b, l, 1024), F32),
                   jax.ShapeDtypeStruct((b, l, SSD_XBC), F32),
                   jax.ShapeDtypeStruct((b, l, NA_WIDTH), BF16),
                   jax.ShapeDtypeStruct((b, l, NA_WIDTH), BF16),
                   jax.ShapeDtypeStruct((b, l, NA_WIDTH), BF16),
                   jax.ShapeDtypeStruct((b, l, CONF_WIDTH), BF16),
                   jax.ShapeDtypeStruct((b, l, LANES), F32),
                   jax.ShapeDtypeStruct((b, 2 * SSD_HEADS, l), F32)],
        scratch_shapes=[pltpu.VMEM((tm + 2 * hb, SSD_XBC), F32),
                        pltpu.VMEM((tm + 2 * hb, CONF_WIDTH), F32),
                        pltpu.VMEM((SUBLANES - 1, tm + 2 * hb - SUBLANES, CONF_WIDTH), F32)],
        compiler_params=_cparams("parallel", "parallel"),
        name="mixin",
    )(x, x, x, shift, scale, *consts)


def _ssd_dir(u_ref, dtc_ref, dtt_ref, al_ref, as_ref, st_ref, y_ref, d, reverse):
    q = SSD_CHUNK
    li = lax.broadcasted_iota(jnp.int32, (q, q), 0)
    si = lax.broadcasted_iota(jnp.int32, (q, q), 1)
    lane = lax.broadcasted_iota(jnp.int32, (q, LANES), 1)
    lo_half = lane < SSD_HEAD_DIM
    mask = (si >= li) if reverse else (si <= li)
    tri = jnp.where(mask, 1.0, 0.0).astype(BF16)
    tri_t = jnp.where((li >= si) if reverse else (li <= si), 1.0, 0.0).astype(BF16)

    a_lane = -jnp.exp(al_ref[...]) * LOG2_E
    a_sub = -jnp.exp(as_ref[d * SSD_HEADS:(d + 1) * SSD_HEADS, :]) * LOG2_E
    dtc = dtc_ref[0]
    hi, mid, lo = _split3(dtc * a_lane)
    cs = _dot(tri, hi) + _dot(tri, mid) + _dot(tri, lo)
    dtr = dtt_ref[0, d * SSD_HEADS:(d + 1) * SSD_HEADS, :] * a_sub
    hi, mid, lo = _split3(dtr)
    cs_r = _dot(hi, tri_t) + _dot(mid, tri_t) + _dot(lo, tri_t)
    yield

    hpg = SSD_HEADS // SSD_GROUPS
    gw = hpg * SSD_HEAD_DIM
    for g in range(SSD_GROUPS):
        bm = u_ref[0, :, SSD_D_INNER + g * SSD_STATE:SSD_D_INNER + (g + 1) * SSD_STATE]
        cm = u_ref[0, :, SSD_D_INNER + (SSD_GROUPS + g) * SSD_STATE:
                   SSD_D_INNER + (SSD_GROUPS + g + 1) * SSD_STATE].astype(BF16)
        bm_t = bm.T.astype(BF16)
        scores = _dot(cm, bm_t)
        st = st_ref[d, g]
        y_off = _dot(cm, st.astype(BF16))
        yield
        xdd = []
        tots = []
        for pp in range(hpg // 2):
            p = g * (hpg // 2) + pp
            c0 = 16 * d + 2 * p
            bc0 = jnp.broadcast_to(cs[:, c0:c0 + 1], (q, LANES))
            bc1 = jnp.broadcast_to(cs[:, c0 + 1:c0 + 2], (q, LANES))
            csx = jnp.where(lo_half, bc0, bc1)
            dtx = jnp.where(lo_half, jnp.broadcast_to(dtc[:, c0:c0 + 1], (q, LANES)),
                            jnp.broadcast_to(dtc[:, c0 + 1:c0 + 2], (q, LANES)))
            xd = u_ref[0, :, p * LANES:(p + 1) * LANES] * dtx
            xd_b = xd.astype(BF16)
            dec0 = jnp.exp2(jnp.where(mask, bc0 - cs_r[2 * p:2 * p + 1, :], -jnp.inf))
            dec1 = jnp.exp2(jnp.where(mask, bc1 - cs_r[2 * p + 1:2 * p + 2, :], -jnp.inf))
            y0 = _dot((scores * dec0).astype(BF16), xd_b)
            y1 = _dot((scores * dec1).astype(BF16), xd_b)
            y = jnp.where(lo_half, y0, y1) + y_off[:, pp * LANES:(pp + 1) * LANES] * jnp.exp2(csx)
            y_ref[0, :, p * LANES:(p + 1) * LANES] = y
            totx = csx[0:1, :] if reverse else csx[q - 1:q, :]
            xdd.append((xd * jnp.exp2(totx - csx)).astype(BF16))
            tots.append(totx)
            yield
        new_states = _dot(bm_t, jnp.concatenate(xdd, axis=1))
        st_ref[d, g] = st * jnp.exp2(jnp.concatenate(tots, axis=1)) + new_states


def _ssd_body(uf_ref, ub_ref, dcf_ref, dcb_ref, dtf_ref, dtb_ref, al_ref, as_ref, h0_ref,
              yf_ref, yb_ref, ht_ref, st_ref):
    i = pl.program_id(1)

    @pl.when(i == 0)
    def _():
        st_ref[...] = h0_ref[0]

    pending = [_ssd_dir(uf_ref, dcf_ref, dtf_ref, al_ref, as_ref, st_ref, yf_ref, 0, False),
               _ssd_dir(ub_ref, dcb_ref, dtb_ref, al_ref, as_ref, st_ref, yb_ref, 1, True)]
    while pending:
        pending = [g for g in pending if next(g, "done") != "done"]

    @pl.when(i == pl.num_programs(1) - 1)
    def _():
        ht_ref[0] = st_ref[...]


def _ssd(u, dt, dtt, a_lane, a_sub, h0):
    b, l, _ = u.shape
    q = SSD_CHUNK
    nc = l // q
    fwd = lambda n: pl.BlockSpec((1, q, n), lambda i, j: (i, j, 0))
    bwd = lambda n: pl.BlockSpec((1, q, n), lambda i, j: (i, nc - 1 - j, 0))
    st_shape = (2, SSD_GROUPS, SSD_STATE, SSD_D_INNER // SSD_GROUPS)
    st_spec = pl.BlockSpec((1,) + st_shape, lambda i, j: (i, 0, 0, 0, 0))
    return pl.pallas_call(
        _ssd_body,
        grid=(b, nc),
        in_specs=[fwd(SSD_XBC), bwd(SSD_XBC), fwd(LANES), bwd(LANES),
                  pl.BlockSpec((1, 2 * SSD_HEADS, q), lambda i, j: (i, 0, j)),
                  pl.BlockSpec((1, 2 * SSD_HEADS, q), lambda i, j: (i, 0, nc - 1 - j)),
                  _const_spec((1, LANES)), _const_spec((2 * SSD_HEADS, 1)), st_spec],
        out_specs=[fwd(SSD_D_INNER), bwd(SSD_D_INNER), st_spec],
        out_shape=[jax.ShapeDtypeStruct((b, l, SSD_D_INNER), F32),
                   jax.ShapeDtypeStruct((b, l, SSD_D_INNER), F32),
                   jax.ShapeDtypeStruct((b,) + st_shape, F32)],
        scratch_shapes=[pltpu.VMEM(st_shape, F32)],
        compiler_params=_cparams("parallel", "arbitrary"),
        name="ssd",
    )(u, u, dt, dt, dtt, dtt, a_lane, a_sub, h0)


def _na_body(*refs, window, n_win, first_key_row=None):
    if window:
        q_ref, k_ref, v_ref, kc_ref, vc_ref, bias_ref, o_ref = refs
    else:
        q_ref, kc_ref, vc_ref, o_ref = refs
    tq = q_ref.shape[1]
    lane = lax.broadcasted_iota(jnp.int32, (tq, LANES), 1)
    lo_half = lane < NA_HEAD_DIM
    if window:
        start = pl.multiple_of(first_key_row(pl.program_id(1)) * GRID_W, GRID_W)
    scores = []
    for h in range(NA_HEADS):
        sl = slice((h // 2) * LANES, (h // 2 + 1) * LANES)
        q2 = q_ref[0, :, sl]
        qm = jnp.where(lo_half if h % 2 == 0 else jnp.logical_not(lo_half), q2, jnp.zeros_like(q2))
        s = _dot_nt(qm, kc_ref[0, :, sl])
        if window:
            s_w = _dot_nt(qm, k_ref[0, pl.ds(start, n_win), sl]) + bias_ref[0, h]
            s = jnp.concatenate([s_w, s], axis=1)
        scores.append(s)
    s_all = jnp.concatenate(scores, axis=0)
    p_all = jnp.exp2(s_all - jnp.max(s_all, axis=-1, keepdims=True))
    inv = 1.0 / jnp.sum(p_all, axis=-1, keepdims=True)
    p_all = p_all.astype(BF16)
    for hp in range(NA_HEADS // 2):
        sl = slice(hp * LANES, (hp + 1) * LANES)
        outs = []
        for par in range(2):
            h = 2 * hp + par
            p = p_all[h * tq:(h + 1) * tq]
            o = _dot(p[:, n_win:], vc_ref[0, :, sl])
            if window:
                o = o + _dot(p[:, :n_win], v_ref[0, pl.ds(start, n_win), sl])
            outs.append(o * inv[h * tq:(h + 1) * tq])
        o_ref[0, :, sl] = jnp.where(lo_half, outs[0], outs[1]).astype(o_ref.dtype)


def _na_geometry(rows):
    wr = min(NA_WIN_ROWS, rows)
    qrows = 2 if rows % 2 == 0 else 1
    per_vreg = LANES // GRID_W
    uw = -(-(wr + qrows - 1) // per_vreg) * per_vreg
    return qrows, wr, min(rows, uw)


def _na_bias_table(rpb, rows):
    qrows, wr, uw = _na_geometry(rows)
    steps = rows // qrows
    r_all = np.arange(rows).reshape(steps, qrows)
    rs_all = np.clip(r_all - wr // 2, 0, rows - wr)
    ks_all = np.clip(np.arange(steps) * qrows - wr // 2, 0, rows - uw)
    krow = ks_all[:, None, None] + np.arange(uw)[None, None, :]
    row_ok = (krow >= rs_all[:, :, None]) & (krow < rs_all[:, :, None] + wr)
    drow_all = np.where(row_ok, krow - r_all[:, :, None] + NA_WIN_ROWS - 1, -1)
    uniq, var_of_step = np.unique(drow_all.reshape(steps, -1), axis=0, return_inverse=True)
    n_var = uniq.shape[0]
    uniq = uniq.reshape(n_var, qrows, uw)
    qcol = np.arange(GRID_W)
    kcol = np.arange(GRID_W)
    col_start = np.clip(qcol - NA_WIN_COLS // 2, 0, GRID_W - NA_WIN_COLS)
    rel = kcol[None, :] - col_start[:, None]
    col_ok = (rel >= 0) & (rel < NA_WIN_COLS)
    dcol = np.clip(kcol[None, :] - qcol[:, None] + NA_WIN_COLS - 1, 0, 2 * NA_WIN_COLS - 2)
    n_dcol = 2 * NA_WIN_COLS - 1
    onehot = jnp.asarray(dcol[None, :, :] == np.arange(n_dcol)[:, None, None], F32)
    sel = rpb.astype(F32)[:, np.maximum(uniq, 0).reshape(-1), :]
    tab = jnp.einsum("hrc,cqk->hqrk", sel, onehot, precision=lax.Precision.HIGHEST)
    tab = tab.reshape(NA_HEADS, GRID_W, n_var, qrows, uw, GRID_W)
    ok = col_ok[None, :, None, None, None, :] & (uniq >= 0)[None, None, :, :, :, None]
    tab = jnp.where(ok, tab * LOG2_E, NEG_INF)
    tab = jnp.transpose(tab, (2, 0, 3, 1, 4, 5)).reshape(n_var, NA_HEADS, qrows * GRID_W, uw * GRID_W)
    return tab, jnp.asarray(var_of_step.reshape(-1), jnp.int32)


def _na_latent(q, k, v, kc, vc, rpb):
    b, l, w = q.shape
    rows = l // GRID_W
    qrows, wr, uw = _na_geometry(rows)
    tab, var_of_step = _na_bias_table(rpb, rows)
    tq = qrows * GRID_W
    n_win = uw * GRID_W
    lc = kc.shape[1]
    full = lambda n: pl.BlockSpec((1, n, w), lambda i, r, vr: (i, 0, 0))
    grid_spec = pltpu.PrefetchScalarGridSpec(
        num_scalar_prefetch=1,
        grid=(b, rows // qrows),
        in_specs=[pl.BlockSpec((1, tq, w), lambda i, r, vr: (i, r, 0)),
                  full(l), full(l), full(lc), full(lc),
                  pl.BlockSpec((1, NA_HEADS, tq, n_win), lambda i, r, vr: (vr[r], 0, 0, 0))],
        out_specs=pl.BlockSpec((1, tq, w), lambda i, r, vr: (i, r, 0)),
    )

    def body(vr_ref, *refs):
        _na_body(*refs, window=True, n_win=n_win, first_key_row=lambda step: jnp.clip(
            step * qrows - wr // 2, 0, rows - uw))

    return pl.pallas_call(
        body,
        grid_spec=grid_spec,
        out_shape=jax.ShapeDtypeStruct((b, l, w), BF16),
        compiler_params=_cparams("parallel", "arbitrary"),
        name="na_latent",
    )(var_of_step, q, k, v, kc, vc, tab)


def _na_context(q, kc, vc):
    b, lc, w = q.shape
    tq = GRID_W
    full = pl.BlockSpec((1, lc, w), lambda i, j: (i, 0, 0))
    return pl.pallas_call(
        functools.partial(_na_body, window=False, n_win=0),
        grid=(b, lc // tq),
        in_specs=[pl.BlockSpec((1, tq, w), lambda i, j: (i, j, 0)), full, full],
        out_specs=pl.BlockSpec((1, tq, w), lambda i, j: (i, j, 0)),
        out_shape=jax.ShapeDtypeStruct((b, lc, w), BF16),
        compiler_params=_cparams("parallel", "arbitrary"),
        name="na_context",
    )(q, kc, vc)


def _tail_body(x_ref, mod_ref, gmix_ref, gffn_ref, yf_ref, yb_ref, xs_ref, z_ref, na_ref, cf_ref,
               dskip_ref, snorm_ref, wgate_ref, wssd_ref, wna_ref, wcf_ref, wo_ref, *rest, router):
    if router:
        wr_ref, xo_ref, h2_ref, lg_ref = rest
    else:
        xo_ref, h2_ref = rest
    d = x_ref.shape[2]
    x = x_ref[0]
    mod = mod_ref[0]
    h = (_rms(x, gmix_ref[...]) * (1.0 + mod[1:2]) + mod[0:1]).astype(BF16)
    y = (yf_ref[0] + yb_ref[0] + dskip_ref[...] * xs_ref[0]) * _silu(z_ref[0])
    ssd = _dot(_rms(y, snorm_ref[...]).astype(BF16), wssd_ref[...])
    merged = _sigmoid(_dot(h, wgate_ref[:, 0:d])) * ssd
    merged = merged + _sigmoid(_dot(h, wgate_ref[:, d:2 * d])) * _dot(na_ref[0], wna_ref[...])
    merged = merged + _sigmoid(_dot(h, wgate_ref[:, 2 * d:3 * d])) * _dot(cf_ref[0], wcf_ref[...])
    xn = x + mod[2:3] * _dot(merged.astype(BF16), wo_ref[...])
    xo_ref[0] = xn
    h2 = _rms(xn, gffn_ref[...]) * (1.0 + mod[4:5]) + mod[3:4]
    if router:
        _to_token_tiles(h2_ref.at[0], h2)
    else:
        h2_ref[0] = h2.astype(BF16)
    if router:
        hi = h2.astype(BF16)
        mid = (h2 - hi.astype(F32)).astype(BF16)
        r = _dot(hi, wr_ref[...])
        lg_ref[0] = _top2_route(r[:, :LANES] + r[:, LANES:] + _dot(mid, wr_ref[:, :LANES]))


def _tail(x, mod, gmix, gffn, yf, yb, u, z, na, cf, dskip, snorm, wgate, wssd, wna, wcf, wo, tm,
          w_router=None):
    b, l, d = x.shape
    row = lambda n: pl.BlockSpec((1, tm, n), lambda i, j: (i, j, 0))
    ins = [x, mod, gmix, gffn, yf, yb, u, z, na, cf, dskip, snorm, wgate, wssd, wna, wcf, wo]
    specs = [row(d), pl.BlockSpec((1, 8, d), lambda i, j: (i, 0, 0)), _const_spec((1, d)),
             _const_spec((1, d)), row(d), row(d), row(SSD_D_INNER), row(d), row(NA_WIDTH),
             row(CONF_WIDTH), _const_spec((1, d)), _const_spec((1, d)), _const_spec(wgate.shape),
             _const_spec(wssd.shape), _const_spec(wna.shape), _const_spec(wcf.shape),
             _const_spec(wo.shape)]
    out_specs = [row(d), row(d)]
    out_shape = [jax.ShapeDtypeStruct((b, l, d), F32), jax.ShapeDtypeStruct((b, l, d), BF16)]
    if w_router is not None:
        assert d == SUBLANES * LANES
        out_specs[1] = pl.BlockSpec((1, tm * SUBLANES, LANES), lambda i, j: (i, j, 0))
        out_shape[1] = jax.ShapeDtypeStruct((b, l * SUBLANES, LANES), F32)
        ins.append(w_router)
        specs.append(_const_spec(w_router.shape))
        out_specs.append(row(LANES))
        out_shape.append(jax.ShapeDtypeStruct((b, l, LANES), F32))
    return pl.pallas_call(
        functools.partial(_tail_body, router=w_router is not None),
        grid=(b, l // tm),
        in_specs=specs,
        out_specs=out_specs,
        out_shape=out_shape,
        compiler_params=_cparams("parallel", "parallel"),
        name="tail",
    )(*ins)


def _ffn_epilogue(x_ref, mod_ref, fn_ref, acc, o_ref, final):
    xn = x_ref[0] + mod_ref[0][5:6] * acc
    if final:
        xn = _rms(xn, fn_ref[...])
    o_ref[0] = xn


FF_CHUNK = 2 * LANES


def _swiglu(h, wg_ref, wu_ref, wd_ref):
    ff = wg_ref.shape[1]
    acts = []
    for c0 in range(0, ff, FF_CHUNK):
        cs = slice(c0, c0 + FF_CHUNK)
        acts.append((_silu(_dot(h, wg_ref[:, cs])) * _dot(h, wu_ref[:, cs])).astype(BF16))
    return _dot(jnp.concatenate(acts, axis=1), wd_ref[...])


def _ffn_body(h_ref, x_ref, mod_ref, fn_ref, wg_ref, wu_ref, wd_ref, o_ref, *, final):
    _ffn_epilogue(x_ref, mod_ref, fn_ref, _swiglu(h_ref[0], wg_ref, wu_ref, wd_ref), o_ref, final)


def _ffn(h, x, mod, fnorm, wg, wu, wd, tm, final):
    b, l, d = x.shape
    assert wg.shape[1] % FF_CHUNK == 0
    row = pl.BlockSpec((1, tm, d), lambda i, j: (i, j, 0))
    return pl.pallas_call(
        functools.partial(_ffn_body, final=final),
        grid=(b, l // tm),
        in_specs=[row, row, pl.BlockSpec((1, 8, d), lambda i, j: (i, 0, 0)), _const_spec((1, d)),
                  _const_spec(wg.shape), _const_spec(wu.shape), _const_spec(wd.shape)],
        out_specs=row,
        out_shape=jax.ShapeDtypeStruct((b, l, d), F32),
        compiler_params=_cparams("parallel", "parallel"),
        name="ffn",
    )(h, x, mod, fnorm, wg, wu, wd)


def _top2_route(logits):
    lane = lax.broadcasted_iota(jnp.int32, logits.shape, 1)
    lg = jnp.where(lane < N_EXPERTS, logits, -jnp.inf)
    v1 = jnp.max(lg, axis=-1, keepdims=True)
    i1 = jnp.min(jnp.where(lg == v1, lane, LANES), axis=-1, keepdims=True)
    lg2 = jnp.where(lane == i1, -jnp.inf, lg)
    v2 = jnp.max(lg2, axis=-1, keepdims=True)
    i2 = jnp.min(jnp.where(lg2 == v2, lane, LANES), axis=-1, keepdims=True)
    e2 = jnp.exp(v2 - v1)
    w1 = 1.0 / (1.0 + e2)
    w2 = e2 / (1.0 + e2)
    rec = jnp.where(lane == 0, i1.astype(F32), 0.0) + jnp.where(lane == 1, i2.astype(F32), 0.0)
    return rec + jnp.where(lane == 2, w1, 0.0) + jnp.where(lane == 3, w2, 0.0)


def _to_token_tiles(ref, val):
    n = val.shape[0]
    for s in range(SUBLANES):
        ref[pl.ds(s, n, stride=SUBLANES), :] = val[:, s * LANES:(s + 1) * LANES]


def _from_token_tiles(ref, first, n):
    return jnp.concatenate([ref[pl.ds(first * SUBLANES + s, n, stride=SUBLANES), :] for s in range(SUBLANES)],
                           axis=1)


def _token_tile(ref, idx):
    return ref.at[pl.ds(pl.multiple_of(idx * SUBLANES, SUBLANES), SUBLANES), :]


def _dispatch_body(lb_ref, pos_ref, h_ref, xs_hbm, zero_buf, sem, *, tc, tm):
    @pl.when(pl.program_id(0) == 0)
    def _():
        zero_buf[...] = jnp.zeros_like(zero_buf)
        fills = [pltpu.make_async_copy(
            zero_buf, xs_hbm.at[pl.ds(pl.multiple_of(lb_ref[e] * (tm * SUBLANES), tm * SUBLANES),
                                      tm * SUBLANES), :], sem) for e in range(lb_ref.shape[0])]
        for f in fills:
            f.start()
            f.wait()

    def body(r, carry):
        src = _token_tile(h_ref, r)
        for k in range(2):
            pltpu.make_async_copy(src, _token_tile(xs_hbm, pos_ref[0, k, r]), sem).start()
        return carry

    lax.fori_loop(0, tc, body, 0, unroll=4)
    for _ in range(2):
        pltpu.make_async_copy(h_ref, xs_hbm.at[pl.ds(0, tc * SUBLANES), :], sem).wait()


def _dispatch(h_tiles, pos, clear_blocks, n_rows, tc, tm):
    t = h_tiles.shape[0] // SUBLANES
    grid_spec = pltpu.PrefetchScalarGridSpec(
        num_scalar_prefetch=1,
        grid=(t // tc,),
        in_specs=[pl.BlockSpec((1, 2, tc), lambda j, lb: (j, 0, 0), memory_space=pltpu.SMEM),
                  pl.BlockSpec((tc * SUBLANES, LANES), lambda j, lb: (j, 0))],
        out_specs=pl.BlockSpec(memory_space=pl.ANY),
        scratch_shapes=[pltpu.VMEM((tm * SUBLANES, LANES), F32), pltpu.SemaphoreType.DMA(())],
    )
    return pl.pallas_call(
        functools.partial(_dispatch_body, tc=tc, tm=tm),
        grid_spec=grid_spec,
        out_shape=jax.ShapeDtypeStruct((n_rows * SUBLANES, LANES), F32),
        compiler_params=_cparams("arbitrary"),
        name="moe_dispatch",
    )(clear_blocks, pos, h_tiles)


def _gmm_body(be_ref, nu_ref, x_ref, wg_ref, wu_ref, wd_ref, y_ref, *, tm):
    j = pl.program_id(0)

    @pl.when(j < nu_ref[0])
    def _():
        h = _from_token_tiles(x_ref, 0, tm).astype(BF16)
        _to_token_tiles(y_ref, _swiglu(h, wg_ref, wu_ref, wd_ref))

    @pl.when(j >= nu_ref[0])
    def _():
        y_ref[...] = jnp.zeros_like(y_ref)


def _gmm(xs, block_expert, n_used, wg, wu, wd, tm):
    ne, d, ff = wg.shape
    n_blocks = xs.shape[0] // (tm * SUBLANES)
    wspec = lambda r, c: pl.BlockSpec((None, r, c), lambda j, be, nu: (be[j], 0, 0))
    grid_spec = pltpu.PrefetchScalarGridSpec(
        num_scalar_prefetch=2,
        grid=(n_blocks,),
        in_specs=[pl.BlockSpec((tm * SUBLANES, LANES), lambda j, be, nu: (jnp.minimum(j, nu[0] - 1), 0)),
                  wspec(d, ff), wspec(d, ff), wspec(ff, d)],
        out_specs=pl.BlockSpec((tm * SUBLANES, LANES), lambda j, be, nu: (j, 0)),
    )
    return pl.pallas_call(
        functools.partial(_gmm_body, tm=tm),
        grid_spec=grid_spec,
        out_shape=jax.ShapeDtypeStruct(xs.shape, F32),
        compiler_params=_cparams("arbitrary"),
        name="moe_gmm",
    )(block_expert, n_used, xs, wg, wu, wd)


def _combine_body(pos_ref, posn_ref, rt_ref, x_ref, mod_ref, fn_ref, y_hbm, o_ref, buf, sem, *, tc, final):
    nl = pl.num_programs(1)
    j = pl.program_id(0) * nl + pl.program_id(1)
    n = pl.num_programs(0) * nl
    slot = j % 2

    def start(ids_ref, s):
        def body(r, carry):
            for k in range(2):
                pltpu.make_async_copy(_token_tile(y_hbm, ids_ref[0, k, r]),
                                      _token_tile(buf.at[s], k * tc + r), sem.at[s]).start()
            return carry
        lax.fori_loop(0, tc, body, 0, unroll=4)

    @pl.when(j == 0)
    def _():
        start(pos_ref, 0)

    def wait(s):
        pltpu.make_async_copy(y_hbm.at[pl.ds(0, 2 * tc * SUBLANES), :], buf.at[s], sem.at[s]).wait()

    wait(slot)

    rt = rt_ref[0]
    cur = buf.at[slot]
    nxt = buf.at[1 - slot]
    group = tc // SUBLANES
    parts = []
    for s in range(SUBLANES):
        for r in range(s * group, (s + 1) * group):
            for k in range(2):
                pltpu.make_async_copy(_token_tile(y_hbm, posn_ref[0, k, r]),
                                      _token_tile(nxt, k * tc + r), sem.at[1 - slot]).start()
        parts.append(rt[:, 2:3] * cur[pl.ds(s, tc, stride=SUBLANES), :]
                     + rt[:, 3:4] * cur[pl.ds(tc * SUBLANES + s, tc, stride=SUBLANES), :])
    _ffn_epilogue(x_ref, mod_ref, fn_ref, jnp.concatenate(parts, axis=1), o_ref, final)

    @pl.when(j == n - 1)
    def _():
        wait(1 - slot)


def _combine(route, x, mod, fnorm, pos, y_sorted, tc, final):
    b, l, d = x.shape
    nl = l // tc
    n = b * nl
    ids = lambda shift: pl.BlockSpec((1, 2, tc), lambda i, j: (jnp.minimum(i * nl + j + shift, n - 1), 0, 0),
                                     memory_space=pltpu.SMEM)
    row = lambda w: pl.BlockSpec((1, tc, w), lambda i, j: (i, j, 0))
    return pl.pallas_call(
        functools.partial(_combine_body, tc=tc, final=final),
        grid=(b, nl),
        in_specs=[ids(0), ids(1), row(LANES), row(d), pl.BlockSpec((1, 8, d), lambda i, j: (i, 0, 0)),
                  _const_spec((1, d)), pl.BlockSpec(memory_space=pl.ANY)],
        out_specs=row(d),
        out_shape=jax.ShapeDtypeStruct((b, l, d), F32),
        scratch_shapes=[pltpu.VMEM((2, 2 * tc * SUBLANES, LANES), F32), pltpu.SemaphoreType.DMA((2,))],
        compiler_params=_cparams("arbitrary", "arbitrary"),
        name="moe_combine",
    )(pos, pos, route, x, mod, fnorm, y_sorted)


def _moe_sparse(h2, route, x, mod, fnorm, wg, wu, wd, tm, tc, final):
    b, l, d = x.shape
    t = b * l
    ne = wg.shape[0]
    rt = route.reshape(t, LANES)
    experts = jnp.concatenate([rt[:, 0], rt[:, 1]]).astype(jnp.int32)
    onehot = (experts[:, None] == jnp.arange(ne, dtype=jnp.int32)[None, :]).astype(jnp.int32)
    csum = jnp.cumsum(onehot, axis=0)
    rank = jnp.sum((csum - onehot) * onehot, axis=1)
    counts = csum[-1]
    padded = ((counts + tm - 1) // tm) * tm
    ends = jnp.cumsum(padded)
    pos = jnp.sum(onehot * (ends - padded)[None, :], axis=1) + rank
    n_blocks = (2 * t) // tm + ne
    block_start = jnp.arange(n_blocks, dtype=jnp.int32) * tm
    block_expert = jnp.minimum(jnp.sum(block_start[:, None] >= ends[None, :], axis=1), ne - 1).astype(jnp.int32)
    n_used = (ends[-1:] // tm).astype(jnp.int32)
    tiled = lambda n: jnp.transpose(pos.astype(jnp.int32).reshape(2, t // n, n), (1, 0, 2))
    pos2 = tiled(tc)
    td = TILE_DISPATCH if t % TILE_DISPATCH == 0 else tc
    clear = jnp.concatenate([ends // tm - 1, n_used[0] + jnp.arange(ne, dtype=jnp.int32)])
    clear = jnp.clip(clear, 0, n_blocks - 1).astype(jnp.int32)
    xs = _dispatch(h2.reshape(t * SUBLANES, LANES), tiled(td), clear, n_blocks * tm, td, tm)
    y_sorted = _gmm(xs, block_expert, n_used, wg, wu, wd, tm)
    return _combine(route, x, mod, fnorm, pos2, y_sorted, tc, final)


TILE_LATENT = 512
TILE_CONTEXT = 256
TILE_EXPERT = 512
TILE_COMBINE = 256
TILE_DISPATCH = 1024


def _row_tile(l, want):
    return want if l % want == 0 else l


def kernel(x, c, ctx, c_ctx, ada_w, ada_b, norm_mix, norm_ffn, w_in, ssd_conv_w, ssd_conv_b, ssd_a_log, ssd_dt_bias, ssd_d, ssd_norm, ssd_out, na_rpb, na_out, conf_conv_w, conf_conv_b, conf_ln_g, conf_ln_b, conf_out, w_o, ffn_gate, ffn_up, ffn_down, moe_router, moe_gate, moe_up, moe_down, final_norm):
    depth = w_in.shape[0]
    b, l, d = x.shape
    lc = ctx.shape[1]
    nh = SSD_HEADS

    cc = jnp.zeros((8, d), F32).at[:b].set(c).at[b].set(c_ctx)
    mods = _ada(cc, ada_w, ada_b)
    fnorm = final_norm.reshape(1, d)

    offs = np.cumsum((0, SSD_D_INNER, SSD_XBC, nh, nh, NA_WIDTH, NA_WIDTH, NA_WIDTH, 2 * CONF_WIDTH, 3 * d))
    seg = lambda w, i: w[:, offs[i]:offs[i + 1]]

    for layer in range(depth):
        need_ctx = layer < depth - 1
        last = layer == depth - 1
        m6 = mods[layer].reshape(8, 6, d)
        mod_l = jnp.zeros((b, 8, d), F32).at[:, :6].set(m6[:b])
        mod_c = jnp.broadcast_to(jnp.zeros((8, d), F32).at[:6].set(m6[b]), (b, 8, d))

        wl = w_in[layer]
        w_main = jnp.concatenate([seg(wl, i) for i in (0, 1, 4, 5, 6, 7)], axis=1).astype(BF16)
        w_dt = jnp.concatenate([seg(wl, 2), seg(wl, 3)], axis=1)
        w_dt_pad = jnp.zeros((d, LANES), F32).at[:, :2 * nh].set(w_dt).astype(BF16)
        w_dtt = w_dt.T.astype(BF16)
        w_gate = seg(wl, 8).astype(BF16)
        dtb = ssd_dt_bias[layer].reshape(2 * nh)
        b_lane = jnp.zeros((1, LANES), F32).at[0, :2 * nh].set(dtb)
        b_sub = dtb.reshape(2 * nh, 1)
        alog = ssd_a_log[layer].reshape(2 * nh)
        a_lane = jnp.zeros((1, LANES), F32).at[0, :2 * nh].set(alog)
        a_sub = alog.reshape(2 * nh, 1)
        dskip = jnp.repeat(ssd_d[layer], SSD_HEAD_DIM).reshape(1, SSD_D_INNER)
        gmix = norm_mix[layer].reshape(1, d)
        gffn = norm_ffn[layer].reshape(1, d)
        snorm = ssd_norm[layer].reshape(1, SSD_D_INNER)
        wssd = ssd_out[layer].astype(BF16)
        wna = na_out[layer].astype(BF16)
        wcf = conf_out[layer].astype(BF16)
        wo = w_o[layer].astype(BF16)

        def mixer_inputs(xx, mod, tm):
            return _mixin(xx, mod[:, 0:1], mod[:, 1:2], gmix, w_main, w_dt_pad, w_dtt, b_lane, b_sub,
                          ssd_conv_w[layer], ssd_conv_b[layer], conf_conv_w[layer], conf_conv_b[layer],
                          conf_ln_g[layer], conf_ln_b[layer], tm)

        z_c, u_c, q_c, k_c, v_c, cf_c, dt_c, dtt_c = mixer_inputs(ctx, mod_c, _row_tile(lc, TILE_CONTEXT))
        z_l, u_l, q_l, k_l, v_l, cf_l, dt_l, dtt_l = mixer_inputs(x, mod_l, _row_tile(l, TILE_LATENT))

        h0 = jnp.zeros((b, 2, SSD_GROUPS, SSD_STATE, SSD_D_INNER // SSD_GROUPS), F32)
        yf_c, yb_c, st = _ssd(u_c, dt_c, dtt_c, a_lane, a_sub, h0)
        yf_l, yb_l, _ = _ssd(u_l, dt_l, dtt_l, a_lane, a_sub, st)

        na_l = _na_latent(q_l, k_l, v_l, k_c, v_c, na_rpb[layer])

        is_moe = layer % 2 == 1
        w_router = None
        if is_moe:
            wr_f = jnp.zeros((d, LANES), F32).at[:, :N_EXPERTS].set(moe_router[layer // 2])
            wr_hi = wr_f.astype(BF16)
            w_router = jnp.concatenate([wr_hi, (wr_f - wr_hi.astype(F32)).astype(BF16)], axis=1)
        tail = functools.partial(_tail, gmix=gmix, gffn=gffn, dskip=dskip, snorm=snorm, wgate=w_gate,
                                 wssd=wssd, wna=wna, wcf=wcf, wo=wo)
        outs = tail(x, mod_l, yf=yf_l, yb=yb_l, u=u_l, z=z_l, na=na_l, cf=cf_l, tm=_row_tile(l, TILE_LATENT),
                    w_router=w_router)
        x_mid, h2_l = outs[0], outs[1]
        if need_ctx:
            na_c = _na_context(q_c, k_c, v_c)
            outs_c = tail(ctx, mod_c, yf=yf_c, yb=yb_c, u=u_c, z=z_c, na=na_c, cf=cf_c,
                          tm=_row_tile(lc, TILE_CONTEXT), w_router=w_router)
            ctx_mid, h2_c = outs_c[0], outs_c[1]

        i = layer // 2
        if not is_moe:
            wg, wu, wd = (t[i].astype(BF16) for t in (ffn_gate, ffn_up, ffn_down))
            x = _ffn(h2_l, x_mid, mod_l, fnorm, wg, wu, wd, _row_tile(l, TILE_LATENT), last)
            if need_ctx:
                ctx = _ffn(h2_c, ctx_mid, mod_c, fnorm, wg, wu, wd, _row_tile(lc, TILE_CONTEXT), False)
        else:
            wg, wu, wd = (t[i].astype(BF16) for t in (moe_gate, moe_up, moe_down))
            x = _moe_sparse(h2_l, outs[2], x_mid, mod_l, fnorm, wg, wu, wd, TILE_EXPERT,
                            _row_tile(l, TILE_COMBINE), last)
            if need_ctx:
                ctx = _moe_sparse(h2_c, outs_c[2], ctx_mid, mod_c, fnorm, wg, wu, wd, TILE_EXPERT,
                                  _row_tile(lc, TILE_COMBINE), False)
    return x
```

```python
import functools

import numpy as np
import jax
import jax.numpy as jnp
from jax import lax
from jax.experimental import pallas as pl
from jax.experimental.pallas import tpu as pltpu

F32 = jnp.float32
BF16 = jnp.bfloat16

NORM_EPS = 1e-6
NEG_INF = -1e30
GRID_W = 64

SSD_HEADS = 16
SSD_HEAD_DIM = 64
SSD_GROUPS = 2
SSD_STATE = 128
SSD_CHUNK = 128
SSD_D_INNER = SSD_HEADS * SSD_HEAD_DIM
SSD_XBC = SSD_D_INNER + 2 * SSD_GROUPS * SSD_STATE
SSD_CONV = 5

NA_HEADS = 8
NA_HEAD_DIM = 64
NA_WIDTH = NA_HEADS * NA_HEAD_DIM
NA_WIN_ROWS = 8
NA_WIN_COLS = 16
LOG2_E = 1.4426950408889634
NA_Q_SCALE = NA_HEAD_DIM ** -0.5 * LOG2_E

CONF_WIDTH = 512
CONF_KERNEL = 31
N_EXPERTS = 8

LANES = 128
SUBLANES = 8
VMEM_LIMIT = 56 << 20


def _cparams(*sem):
    return pltpu.CompilerParams(dimension_semantics=sem, vmem_limit_bytes=VMEM_LIMIT)


def _sigmoid(x):
    return 1.0 / (1.0 + jnp.exp(-x))


def _silu(x):
    return x * _sigmoid(x)


def _softplus(x):
    return jnp.maximum(x, 0.0) + jnp.log1p(jnp.exp(-jnp.abs(x)))


def _rms(x, g):
    return x * lax.rsqrt(jnp.mean(x * x, axis=-1, keepdims=True) + NORM_EPS) * g


def _dot(a, b):
    return jnp.dot(a, b, preferred_element_type=F32)


def _dot_nt(a, b):
    return lax.dot_general(a, b, (((1,), (1,)), ((), ())), preferred_element_type=F32)


def _split3(a):
    hi = a.astype(BF16)
    r1 = a - hi.astype(F32)
    mid = r1.astype(BF16)
    lo = (r1 - mid.astype(F32)).astype(BF16)
    return hi, mid, lo


def _const_spec(shape):
    nd = len(shape)
    return pl.BlockSpec(shape, lambda *_: (0,) * nd, pipeline_mode=pl.Buffered(1))


def _ada_body(c_ref, w_ref, b_ref, o_ref):
    s = _silu(c_ref[...]).astype(BF16)
    o_ref[...] = _dot(s, w_ref[...].astype(BF16)) + b_ref[...]


def _ada(cc, ada_w, ada_b):
    depth, d, n = ada_w.shape
    tn = 1536
    return pl.pallas_call(
        _ada_body,
        grid=(depth, n // tn),
        in_specs=[pl.BlockSpec((8, d), lambda l, j: (0, 0)),
                  pl.BlockSpec((None, d, tn), lambda l, j: (l, 0, j)),
                  pl.BlockSpec((None, 1, tn), lambda l, j: (l, 0, j))],
        out_specs=pl.BlockSpec((None, 8, tn), lambda l, j: (l, 0, j)),
        out_shape=jax.ShapeDtypeStruct((depth, 8, n), F32),
        compiler_params=_cparams("arbitrary", "arbitrary"),
        name="ada",
    )(cc, ada_w, ada_b.reshape(depth, 1, n))


_SEG_Z = (0, 1024)
_SEG_XBC = (1024, 2560)
_SEG_Q = (2560, 3072)
_SEG_K = (3072, 3584)
_SEG_V = (3584, 4096)
_SEG_GLU = (4096, 5120)

CONV_HALO = 16


def _dwconv_tile(buf, shifted, w_ref, b_ref, taps, tl, strip, emit):
    if shifted is not None:
        for s in range(1, SUBLANES):
            shifted[s - 1] = buf[s:s + shifted.shape[1], :]

    def window(off):
        if shifted is not None and off % SUBLANES:
            a = off - off % SUBLANES
            return shifted[off % SUBLANES - 1, a:a + strip, :]
        return buf[off:off + strip, :]

    def tap_weight(j):
        wj = w_ref[j * SUBLANES:(j + 1) * SUBLANES, :]
        return jnp.concatenate([wj] * (strip // SUBLANES), axis=0)

    first = CONV_HALO - taps // 2
    for r0 in range(0, tl, strip):
        acc = b_ref[...] + tap_weight(0) * window(r0 + first)
        for j in range(1, taps):
            acc = acc + tap_weight(j) * window(r0 + first + j)
        emit(r0, acc)


def _mixin_body(xp_ref, x_ref, xn_ref, sh_ref, sc_ref, g_ref, w_ref, wdt_ref, wdtt_ref, bl_ref, bs_ref,
                c5w_ref, c5b_ref, c31w_ref, c31b_ref, lng_ref, lnb_ref,
                z_ref, u_ref, q_ref, k_ref, v_ref, cf_ref, dt_ref, dtt_ref, buf5, buf31, sh31, *, tm):
    j = pl.program_id(1)
    hb = CONV_HALO
    norm_mod = lambda t: (_rms(t, g_ref[...]) * (1.0 + sc_ref[0]) + sh_ref[0]).astype(BF16)
    h = norm_mod(x_ref[0])
    hh = norm_mod(jnp.concatenate([xp_ref[0], xn_ref[0]], axis=0))
    has_prev = j > 0
    has_next = j < pl.num_programs(1) - 1

    cw = 2 * LANES

    def glu_chunk(c):
        def run():
            a0 = _SEG_GLU[0] + c * cw
            g0 = a0 + CONF_WIDTH
            glu = lambda t: _dot(t, w_ref[:, a0:a0 + cw]) * _sigmoid(_dot(t, w_ref[:, g0:g0 + cw]))
            halo = glu(hh)
            cols = slice(c * cw, (c + 1) * cw)
            buf31[0:hb, cols] = jnp.where(has_prev, halo[0:hb], 0.0)
            buf31[hb:hb + tm, cols] = glu(h)
            buf31[hb + tm:, cols] = jnp.where(has_next, halo[hb:], 0.0)
        return run

    def proj_chunk(o_ref, seg, c, scale=None):
        def run():
            r = _dot(h, w_ref[:, seg[0] + c * cw:seg[0] + (c + 1) * cw])
            o_ref[0, :, c * cw:(c + 1) * cw] = (r if scale is None else r * scale).astype(o_ref.dtype)
        return run

    def dt_chunks():
        dt_ref[0] = _softplus(_dot(h, wdt_ref[...]) + bl_ref[...])
        dtt_ref[0] = _softplus(_dot_nt(wdtt_ref[...], h) + bs_ref[...])

    during_conv5 = [glu_chunk(c) for c in range(CONF_WIDTH // cw)]
    during_conv31 = ([proj_chunk(z_ref, _SEG_Z, c) for c in range(4)]
                     + [proj_chunk(q_ref, _SEG_Q, c, NA_Q_SCALE) for c in range(2)]
                     + [proj_chunk(k_ref, _SEG_K, c) for c in range(2)]
                     + [proj_chunk(v_ref, _SEG_V, c) for c in range(2)] + [dt_chunks])

    def run_some(pending, n):
        for _ in range(min(n, len(pending))):
            pending.pop(0)()

    halo = _dot(hh, w_ref[:, _SEG_XBC[0]:_SEG_XBC[1]])
    buf5[0:hb, :] = jnp.where(has_prev, halo[0:hb], 0.0)
    buf5[hb:hb + tm, :] = _dot(h, w_ref[:, _SEG_XBC[0]:_SEG_XBC[1]])
    buf5[hb + tm:, :] = jnp.where(has_next, halo[hb:], 0.0)

    def emit_u(r0, acc):
        u_ref[0, r0:r0 + 16, :] = _silu(acc)
        if r0 % 64 == 0:
            run_some(during_conv5, 1)

    _dwconv_tile(buf5, None, c5w_ref, c5b_ref, SSD_CONV, tm, 16, emit_u)
    run_some(during_conv5, len(during_conv5))

    def emit_cf(r0, acc):
        xc = acc - jnp.mean(acc, axis=-1, keepdims=True)
        var = jnp.mean(xc * xc, axis=-1, keepdims=True)
        y = xc * lax.rsqrt(var + NORM_EPS) * lng_ref[...] + lnb_ref[...]
        cf_ref[0, r0:r0 + 32, :] = _silu(y).astype(BF16)
        run_some(during_conv31, 2)

    _dwconv_tile(buf31, sh31, c31w_ref, c31b_ref, CONF_KERNEL, tm, 32, emit_cf)
    run_some(during_conv31, len(during_conv31))


def _mixin(x, shift, scale, g, w_main, w_dt, w_dtt, b_lane, b_sub, c5w, c5b, c31w, c31b, ln_g, ln_b, tm):
    b, l, d = x.shape
    hb = CONV_HALO
    nh = tm // hb
    last = l // hb - 1
    row = lambda n: pl.BlockSpec((1, tm, n), lambda i, j: (i, j, 0))
    mod = pl.BlockSpec((1, 1, d), lambda i, j: (i, 0, 0))
    rep = lambda w: jnp.repeat(w, SUBLANES, axis=0)
    consts = [g, w_main, w_dt, w_dtt, b_lane, b_sub, rep(c5w), c5b.reshape(1, -1), rep(c31w),
              c31b.reshape(1, -1), ln_g.reshape(1, -1), ln_b.reshape(1, -1)]
    return pl.pallas_call(
        functools.partial(_mixin_body, tm=tm),
        grid=(b, l // tm),
        in_specs=[pl.BlockSpec((1, hb, d), lambda i, j: (i, jnp.maximum(j * nh - 1, 0), 0)),
                  row(d),
                  pl.BlockSpec((1, hb, d), lambda i, j: (i, jnp.minimum((j + 1) * nh, last), 0)),
                  mod, mod] + [_const_spec(c.shape) for c in consts],
        out_specs=[row(1024), row(SSD_XBC), row(NA_WIDTH), row(NA_WIDTH), row(NA_WIDTH),
                   row(CONF_WIDTH), row(LANES),
                   pl.BlockSpec((1, 2 * SSD_HEADS, tm), lambda i, j: (i, 0, j))],
        out_shape=[jax.ShapeDtypeStruct((b, l, 1024), F32),
                   jax.ShapeDtypeStruct((b, l, SSD_XBC), F32),
                   jax.ShapeDtypeStruct((b, l, NA_WIDTH), BF16),
                   jax.ShapeDtypeStruct((b, l, NA_WIDTH), BF16),
                   jax.ShapeDtypeStruct((b, l, NA_WIDTH), BF16),
                   jax.ShapeDtypeStruct((b, l, CONF_WIDTH), BF16),
                   jax.ShapeDtypeStruct((b, l, LANES), F32),
                   jax.ShapeDtypeStruct((b, 2 * SSD_HEADS, l), F32)],
        scratch_shapes=[pltpu.VMEM((tm + 2 * hb, SSD_XBC), F32),
                        pltpu.VMEM((tm + 2 * hb, CONF_WIDTH), F32),
                        pltpu.VMEM((SUBLANES - 1, tm + 2 * hb - SUBLANES, CONF_WIDTH), F32)],
        compiler_params=_cparams("parallel", "parallel"),
        name="mixin",
    )(x, x, x, shift, scale, *consts)


def _ssd_dir(u_ref, dtc_ref, dtt_ref, al_ref, as_ref, st_ref, y_ref, d, reverse):
    q = SSD_CHUNK
    li = lax.broadcasted_iota(jnp.int32, (q, q), 0)
    si = lax.broadcasted_iota(jnp.int32, (q, q), 1)
    lane = lax.broadcasted_iota(jnp.int32, (q, LANES), 1)
    lo_half = lane < SSD_HEAD_DIM
    mask = (si >= li) if reverse else (si <= li)
    tri = jnp.where(mask, 1.0, 0.0).astype(BF16)
    tri_t = jnp.where((li >= si) if reverse else (li <= si), 1.0, 0.0).astype(BF16)

    a_lane = -jnp.exp(al_ref[...]) * LOG2_E
    a_sub = -jnp.exp(as_ref[d * SSD_HEADS:(d + 1) * SSD_HEADS, :]) * LOG2_E
    dtc = dtc_ref[0]
    hi, mid, lo = _split3(dtc * a_lane)
    cs = _dot(tri, hi) + _dot(tri, mid) + _dot(tri, lo)
    dtr = dtt_ref[0, d * SSD_HEADS:(d + 1) * SSD_HEADS, :] * a_sub
    hi, mid, lo = _split3(dtr)
    cs_r = _dot(hi, tri_t) + _dot(mid, tri_t) + _dot(lo, tri_t)
    yield

    hpg = SSD_HEADS // SSD_GROUPS
    for g in range(SSD_GROUPS):
        bm = u_ref[0, :, SSD_D_INNER + g * SSD_STATE:SSD_D_INNER + (g + 1) * SSD_STATE]
        cm = u_ref[0, :, SSD_D_INNER + (SSD_GROUPS + g) * SSD_STATE:
                   SSD_D_INNER + (SSD_GROUPS + g + 1) * SSD_STATE].astype(BF16)
        bm_t = bm.T.astype(BF16)
        scores = _dot(cm, bm_t)
        st = st_ref[d, g]
        y_off = _dot(cm, st.astype(BF16))
        yield
        xdd = []
        tots = []
        for pp in range(hpg // 2):
            p = g * (hpg // 2) + pp
            c0 = 16 * d + 2 * p
            bc0 = jnp.broadcast_to(cs[:, c0:c0 + 1], (q, LANES))
            bc1 = jnp.broadcast_to(cs[:, c0 + 1:c0 + 2], (q, LANES))
            csx = jnp.where(lo_half, bc0, bc1)
            dtx = jnp.where(lo_half, jnp.broadcast_to(dtc[:, c0:c0 + 1], (q, LANES)),
                            jnp.broadcast_to(dtc[:, c0 + 1:c0 + 2], (q, LANES)))
            xd = u_ref[0, :, p * LANES:(p + 1) * LANES] * dtx
            xd_b = xd.astype(BF16)
            dec0 = jnp.exp2(jnp.where(mask, bc0 - cs_r[2 * p:2 * p + 1, :], -jnp.inf))
            dec1 = jnp.exp2(jnp.where(mask, bc1 - cs_r[2 * p + 1:2 * p + 2, :], -jnp.inf))
            y0 = _dot((scores * dec0).astype(BF16), xd_b)
            y1 = _dot((scores * dec1).astype(BF16), xd_b)
            y = jnp.where(lo_half, y0, y1) + y_off[:, pp * LANES:(pp + 1) * LANES] * jnp.exp2(csx)
            y_ref[0, :, p * LANES:(p + 1) * LANES] = y
            totx = csx[0:1, :] if reverse else csx[q - 1:q, :]
            xdd.append((xd * jnp.exp2(totx - csx)).astype(BF16))
            tots.append(totx)
            yield
        new_states = _dot(bm_t, jnp.concatenate(xdd, axis=1))
        st_ref[d, g] = st * jnp.exp2(jnp.concatenate(tots, axis=1)) + new_states


def _ssd_body(uf_ref, ub_ref, dcf_ref, dcb_ref, dtf_ref, dtb_ref, al_ref, as_ref, h0_ref,
              yf_ref, yb_ref, ht_ref, st_ref):
    i = pl.program_id(1)

    @pl.when(i == 0)
    def _():
        st_ref[...] = h0_ref[0]

    pending = [_ssd_dir(uf_ref, dcf_ref, dtf_ref, al_ref, as_ref, st_ref, yf_ref, 0, False),
               _ssd_dir(ub_ref, dcb_ref, dtb_ref, al_ref, as_ref, st_ref, yb_ref, 1, True)]
    while pending:
        pending = [g for g in pending if next(g, "done") != "done"]

    @pl.when(i == pl.num_programs(1) - 1)
    def _():
        ht_ref[0] = st_ref[...]


def _ssd(u, dt, dtt, a_lane, a_sub, h0):
    b, l, _ = u.shape
    q = SSD_CHUNK
    nc = l // q
    fwd = lambda n: pl.BlockSpec((1, q, n), lambda i, j: (i, j, 0))
    bwd = lambda n: pl.BlockSpec((1, q, n), lambda i, j: (i, nc - 1 - j, 0))
    st_shape = (2, SSD_GROUPS, SSD_STATE, SSD_D_INNER // SSD_GROUPS)
    st_spec = pl.BlockSpec((1,) + st_shape, lambda i, j: (i, 0, 0, 0, 0))
    return pl.pallas_call(
        _ssd_body,
        grid=(b, nc),
        in_specs=[fwd(SSD_XBC), bwd(SSD_XBC), fwd(LANES), bwd(LANES),
                  pl.BlockSpec((1, 2 * SSD_HEADS, q), lambda i, j: (i, 0, j)),
                  pl.BlockSpec((1, 2 * SSD_HEADS, q), lambda i, j: (i, 0, nc - 1 - j)),
                  _const_spec((1, LANES)), _const_spec((2 * SSD_HEADS, 1)), st_spec],
        out_specs=[fwd(SSD_D_INNER), bwd(SSD_D_INNER), st_spec],
        out_shape=[jax.ShapeDtypeStruct((b, l, SSD_D_INNER), F32),
                   jax.ShapeDtypeStruct((b, l, SSD_D_INNER), F32),
                   jax.ShapeDtypeStruct((b,) + st_shape, F32)],
        scratch_shapes=[pltpu.VMEM(st_shape, F32)],
        compiler_params=_cparams("parallel", "arbitrary"),
        name="ssd",
    )(u, u, dt, dt, dtt, dtt, a_lane, a_sub, h0)


def _na_body(*refs, window, n_win, first_key_row=None):
    if window:
        q_ref, k_ref, v_ref, kc_ref, vc_ref, bias_ref, o_ref = refs
    else:
        q_ref, kc_ref, vc_ref, o_ref = refs
    tq = q_ref.shape[1]
    lane = lax.broadcasted_iota(jnp.int32, (tq, LANES), 1)
    lo_half = lane < NA_HEAD_DIM
    if window:
        start = pl.multiple_of(first_key_row(pl.program_id(1)) * GRID_W, GRID_W)
    scores = []
    for h in range(NA_HEADS):
        sl = slice((h // 2) * LANES, (h // 2 + 1) * LANES)
        q2 = q_ref[0, :, sl]
        qm = jnp.where(lo_half if h % 2 == 0 else jnp.logical_not(lo_half), q2, jnp.zeros_like(q2))
        s = _dot_nt(qm, kc_ref[0, :, sl])
        if window:
            s_w = _dot_nt(qm, k_ref[0, pl.ds(start, n_win), sl]) + bias_ref[0, h]
            s = jnp.concatenate([s_w, s], axis=1)
        scores.append(s)
    s_all = jnp.concatenate(scores, axis=0)
    p_all = jnp.exp2(s_all - jnp.max(s_all, axis=-1, keepdims=True))
    inv = 1.0 / jnp.sum(p_all, axis=-1, keepdims=True)
    p_all = p_all.astype(BF16)
    for hp in range(NA_HEADS // 2):
        sl = slice(hp * LANES, (hp + 1) * LANES)
        outs = []
        for par in range(2):
            h = 2 * hp + par
            p = p_all[h * tq:(h + 1) * tq]
            o = _dot(p[:, n_win:], vc_ref[0, :, sl])
            if window:
                o = o + _dot(p[:, :n_win], v_ref[0, pl.ds(start, n_win), sl])
            outs.append(o * inv[h * tq:(h + 1) * tq])
        o_ref[0, :, sl] = jnp.where(lo_half, outs[0], outs[1]).astype(o_ref.dtype)


def _na_geometry(rows):
    wr = min(NA_WIN_ROWS, rows)
    qrows = 2 if rows % 2 == 0 else 1
    per_vreg = LANES // GRID_W
    uw = -(-(wr + qrows - 1) // per_vreg) * per_vreg
    return qrows, wr, min(rows, uw)


def _na_bias_table(rpb, rows):
    qrows, wr, uw = _na_geometry(rows)
    steps = rows // qrows
    r_all = np.arange(rows).reshape(steps, qrows)
    rs_all = np.clip(r_all - wr // 2, 0, rows - wr)
    ks_all = np.clip(np.arange(steps) * qrows - wr // 2, 0, rows - uw)
    krow = ks_all[:, None, None] + np.arange(uw)[None, None, :]
    row_ok = (krow >= rs_all[:, :, None]) & (krow < rs_all[:, :, None] + wr)
    drow_all = np.where(row_ok, krow - r_all[:, :, None] + NA_WIN_ROWS - 1, -1)
    uniq, var_of_step = np.unique(drow_all.reshape(steps, -1), axis=0, return_inverse=True)
    n_var = uniq.shape[0]
    uniq = uniq.reshape(n_var, qrows, uw)
    qcol = np.arange(GRID_W)
    kcol = np.arange(GRID_W)
    col_start = np.clip(qcol - NA_WIN_COLS // 2, 0, GRID_W - NA_WIN_COLS)
    rel = kcol[None, :] - col_start[:, None]
    col_ok = (rel >= 0) & (rel < NA_WIN_COLS)
    dcol = np.clip(kcol[None, :] - qcol[:, None] + NA_WIN_COLS - 1, 0, 2 * NA_WIN_COLS - 2)
    n_dcol = 2 * NA_WIN_COLS - 1
    onehot = jnp.asarray(dcol[None, :, :] == np.arange(n_dcol)[:, None, None], F32)
    sel = rpb.astype(F32)[:, np.maximum(uniq, 0).reshape(-1), :]
    tab = jnp.einsum("hrc,cqk->hqrk", sel, onehot, precision=lax.Precision.HIGHEST)
    tab = tab.reshape(NA_HEADS, GRID_W, n_var, qrows, uw, GRID_W)
    ok = col_ok[None, :, None, None, None, :] & (uniq >= 0)[None, None, :, :, :, None]
    tab = jnp.where(ok, tab * LOG2_E, NEG_INF)
    tab = jnp.transpose(tab, (2, 0, 3, 1, 4, 5)).reshape(n_var, NA_HEADS, qrows * GRID_W, uw * GRID_W)
    return tab, jnp.asarray(var_of_step.reshape(-1), jnp.int32)


def _na_latent(q, k, v, kc, vc, rpb):
    b, l, w = q.shape
    rows = l // GRID_W
    qrows, wr, uw = _na_geometry(rows)
    tab, var_of_step = _na_bias_table(rpb, rows)
    tq = qrows * GRID_W
    n_win = uw * GRID_W
    lc = kc.shape[1]
    full = lambda n: pl.BlockSpec((1, n, w), lambda i, r, vr: (i, 0, 0))
    grid_spec = pltpu.PrefetchScalarGridSpec(
        num_scalar_prefetch=1,
        grid=(b, rows // qrows),
        in_specs=[pl.BlockSpec((1, tq, w), lambda i, r, vr: (i, r, 0)),
                  full(l), full(l), full(lc), full(lc),
                  pl.BlockSpec((1, NA_HEADS, tq, n_win), lambda i, r, vr: (vr[r], 0, 0, 0))],
        out_specs=pl.BlockSpec((1, tq, w), lambda i, r, vr: (i, r, 0)),
    )

    def body(vr_ref, *refs):
        _na_body(*refs, window=True, n_win=n_win, first_key_row=lambda step: jnp.clip(
            step * qrows - wr // 2, 0, rows - uw))

    return pl.pallas_call(
        body,
        grid_spec=grid_spec,
        out_shape=jax.ShapeDtypeStruct((b, l, w), BF16),
        compiler_params=_cparams("parallel", "arbitrary"),
        name="na_latent",
    )(var_of_step, q, k, v, kc, vc, tab)


def _na_context(q, kc, vc):
    b, lc, w = q.shape
    tq = GRID_W
    full = pl.BlockSpec((1, lc, w), lambda i, j: (i, 0, 0))
    return pl.pallas_call(
        functools.partial(_na_body, window=False, n_win=0),
        grid=(b, lc // tq),
        in_specs=[pl.BlockSpec((1, tq, w), lambda i, j: (i, j, 0)), full, full],
        out_specs=pl.BlockSpec((1, tq, w), lambda i, j: (i, j, 0)),
        out_shape=jax.ShapeDtypeStruct((b, lc, w), BF16),
        compiler_params=_cparams("parallel", "arbitrary"),
        name="na_context",
    )(q, kc, vc)


def _tail_body(x_ref, mod_ref, gmix_ref, gffn_ref, yf_ref, yb_ref, xs_ref, z_ref, na_ref, cf_ref,
               dskip_ref, snorm_ref, wgate_ref, wssd_ref, wna_ref, wcf_ref, wo_ref, *rest, router):
    if router:
        wr_ref, xo_ref, h2_ref, lg_ref = rest
    else:
        xo_ref, h2_ref = rest
    d = x_ref.shape[2]
    x = x_ref[0]
    mod = mod_ref[0]
    h = (_rms(x, gmix_ref[...]) * (1.0 + mod[1:2]) + mod[0:1]).astype(BF16)
    y = (yf_ref[0] + yb_ref[0] + dskip_ref[...] * xs_ref[0]) * _silu(z_ref[0])
    ssd = _dot(_rms(y, snorm_ref[...]).astype(BF16), wssd_ref[...])
    merged = _sigmoid(_dot(h, wgate_ref[:, 0:d])) * ssd
    merged = merged + _sigmoid(_dot(h, wgate_ref[:, d:2 * d])) * _dot(na_ref[0], wna_ref[...])
    merged = merged + _sigmoid(_dot(h, wgate_ref[:, 2 * d:3 * d])) * _dot(cf_ref[0], wcf_ref[...])
    xn = x + mod[2:3] * _dot(merged.astype(BF16), wo_ref[...])
    xo_ref[0] = xn
    h2 = _rms(xn, gffn_ref[...]) * (1.0 + mod[4:5]) + mod[3:4]
    if router:
        _to_token_tiles(h2_ref.at[0], h2)
    else:
        h2_ref[0] = h2.astype(BF16)
    if router:
        hi = h2.astype(BF16)
        mid = (h2 - hi.astype(F32)).astype(BF16)
        r = _dot(hi, wr_ref[...])
        lg_ref[0] = _top2_route(r[:, :LANES] + r[:, LANES:] + _dot(mid, wr_ref[:, :LANES]))


def _tail(x, mod, gmix, gffn, yf, yb, u, z, na, cf, dskip, snorm, wgate, wssd, wna, wcf, wo, tm,
          w_router=None):
    b, l, d = x.shape
    row = lambda n: pl.BlockSpec((1, tm, n), lambda i, j: (i, j, 0))
    ins = [x, mod, gmix, gffn, yf, yb, u, z, na, cf, dskip, snorm, wgate, wssd, wna, wcf, wo]
    specs = [row(d), pl.BlockSpec((1, 8, d), lambda i, j: (i, 0, 0)), _const_spec((1, d)),
             _const_spec((1, d)), row(d), row(d), row(SSD_D_INNER), row(d), row(NA_WIDTH),
             row(CONF_WIDTH), _const_spec((1, d)), _const_spec((1, d)), _const_spec(wgate.shape),
             _const_spec(wssd.shape), _const_spec(wna.shape), _const_spec(wcf.shape),
             _const_spec(wo.shape)]
    out_specs = [row(d), row(d)]
    out_shape = [jax.ShapeDtypeStruct((b, l, d), F32), jax.ShapeDtypeStruct((b, l, d), BF16)]
    if w_router is not None:
        assert d == SUBLANES * LANES
        out_specs[1] = pl.BlockSpec((1, tm * SUBLANES, LANES), lambda i, j: (i, j, 0))
        out_shape[1] = jax.ShapeDtypeStruct((b, l * SUBLANES, LANES), F32)
        ins.append(w_router)
        specs.append(_const_spec(w_router.shape))
        out_specs.append(row(LANES))
        out_shape.append(jax.ShapeDtypeStruct((b, l, LANES), F32))
    return pl.pallas_call(
        functools.partial(_tail_body, router=w_router is not None),
        grid=(b, l // tm),
        in_specs=specs,
        out_specs=out_specs,
        out_shape=out_shape,
        compiler_params=_cparams("parallel", "parallel"),
        name="tail",
    )(*ins)


def _ffn_epilogue(x_ref, mod_ref, fn_ref, acc, o_ref, final):
    xn = x_ref[0] + mod_ref[0][5:6] * acc
    if final:
        xn = _rms(xn, fn_ref[...])
    o_ref[0] = xn


FF_CHUNK = 2 * LANES


def _swiglu(h, wg_ref, wu_ref, wd_ref):
    ff = wg_ref.shape[1]
    acts = []
    for c0 in range(0, ff, FF_CHUNK):
        cs = slice(c0, c0 + FF_CHUNK)
        acts.append((_silu(_dot(h, wg_ref[:, cs])) * _dot(h, wu_ref[:, cs])).astype(BF16))
    return _dot(jnp.concatenate(acts, axis=1), wd_ref[...])


def _ffn_body(h_ref, x_ref, mod_ref, fn_ref, wg_ref, wu_ref, wd_ref, o_ref, *, final):
    _ffn_epilogue(x_ref, mod_ref, fn_ref, _swiglu(h_ref[0], wg_ref, wu_ref, wd_ref), o_ref, final)


def _ffn(h, x, mod, fnorm, wg, wu, wd, tm, final):
    b, l, d = x.shape
    assert wg.shape[1] % FF_CHUNK == 0
    row = pl.BlockSpec((1, tm, d), lambda i, j: (i, j, 0))
    return pl.pallas_call(
        functools.partial(_ffn_body, final=final),
        grid=(b, l // tm),
        in_specs=[row, row, pl.BlockSpec((1, 8, d), lambda i, j: (i, 0, 0)), _const_spec((1, d)),
                  _const_spec(wg.shape), _const_spec(wu.shape), _const_spec(wd.shape)],
        out_specs=row,
        out_shape=jax.ShapeDtypeStruct((b, l, d), F32),
        compiler_params=_cparams("parallel", "parallel"),
        name="ffn",
    )(h, x, mod, fnorm, wg, wu, wd)


def _top2_route(logits):
    lane = lax.broadcasted_iota(jnp.int32, logits.shape, 1)
    lg = jnp.where(lane < N_EXPERTS, logits, -jnp.inf)
    v1 = jnp.max(lg, axis=-1, keepdims=True)
    i1 = jnp.min(jnp.where(lg == v1, lane, LANES), axis=-1, keepdims=True)
    lg2 = jnp.where(lane == i1, -jnp.inf, lg)
    v2 = jnp.max(lg2, axis=-1, keepdims=True)
    i2 = jnp.min(jnp.where(lg2 == v2, lane, LANES), axis=-1, keepdims=True)
    e2 = jnp.exp(v2 - v1)
    w1 = 1.0 / (1.0 + e2)
    w2 = e2 / (1.0 + e2)
    rec = jnp.where(lane == 0, i1.astype(F32), 0.0) + jnp.where(lane == 1, i2.astype(F32), 0.0)
    return rec + jnp.where(lane == 2, w1, 0.0) + jnp.where(lane == 3, w2, 0.0)


def _to_token_tiles(ref, val):
    n = val.shape[0]
    for s in range(SUBLANES):
        ref[pl.ds(s, n, stride=SUBLANES), :] = val[:, s * LANES:(s + 1) * LANES]


def _from_token_tiles(ref, first, n):
    return jnp.concatenate([ref[pl.ds(first * SUBLANES + s, n, stride=SUBLANES), :] for s in range(SUBLANES)],
                           axis=1)


def _token_tile(ref, idx):
    return ref.at[pl.ds(pl.multiple_of(idx * SUBLANES, SUBLANES), SUBLANES), :]


def _dispatch_body(lb_ref, pos_ref, h_ref, xs_hbm, zero_buf, sem, *, tc, tm):
    @pl.when(pl.program_id(0) == 0)
    def _():
        zero_buf[...] = jnp.zeros_like(zero_buf)
        fills = [pltpu.make_async_copy(
            zero_buf, xs_hbm.at[pl.ds(pl.multiple_of(lb_ref[e] * (tm * SUBLANES), tm * SUBLANES),
                                      tm * SUBLANES), :], sem) for e in range(lb_ref.shape[0])]
        for f in fills:
            f.start()
            f.wait()

    def body(r, carry):
        src = _token_tile(h_ref, r)
        for k in range(2):
            pltpu.make_async_copy(src, _token_tile(xs_hbm, pos_ref[0, k, r]), sem).start()
        return carry

    lax.fori_loop(0, tc, body, 0, unroll=4)
    for _ in range(2):
        pltpu.make_async_copy(h_ref, xs_hbm.at[pl.ds(0, tc * SUBLANES), :], sem).wait()


def _dispatch(h_tiles, pos, clear_blocks, n_rows, tc, tm):
    t = h_tiles.shape[0] // SUBLANES
    grid_spec = pltpu.PrefetchScalarGridSpec(
        num_scalar_prefetch=1,
        grid=(t // tc,),
        in_specs=[pl.BlockSpec((1, 2, tc), lambda j, lb: (j, 0, 0), memory_space=pltpu.SMEM),
                  pl.BlockSpec((tc * SUBLANES, LANES), lambda j, lb: (j, 0))],
        out_specs=pl.BlockSpec(memory_space=pl.ANY),
        scratch_shapes=[pltpu.VMEM((tm * SUBLANES, LANES), F32), pltpu.SemaphoreType.DMA(())],
    )
    return pl.pallas_call(
        functools.partial(_dispatch_body, tc=tc, tm=tm),
        grid_spec=grid_spec,
        out_shape=jax.ShapeDtypeStruct((n_rows * SUBLANES, LANES), F32),
        compiler_params=_cparams("arbitrary"),
        name="moe_dispatch",
    )(clear_blocks, pos, h_tiles)


def _gmm_body(be_ref, nu_ref, x_ref, wg_ref, wu_ref, wd_ref, y_ref, *, tm):
    j = pl.program_id(0)

    @pl.when(j < nu_ref[0])
    def _():
        h = _from_token_tiles(x_ref, 0, tm).astype(BF16)
        _to_token_tiles(y_ref, _swiglu(h, wg_ref, wu_ref, wd_ref))

    @pl.when(j >= nu_ref[0])
    def _():
        y_ref[...] = jnp.zeros_like(y_ref)


def _gmm(xs, block_expert, n_used, wg, wu, wd, tm):
    ne, d, ff = wg.shape
    n_blocks = xs.shape[0] // (tm * SUBLANES)
    wspec = lambda r, c: pl.BlockSpec((None, r, c), lambda j, be, nu: (be[j], 0, 0))
    grid_spec = pltpu.PrefetchScalarGridSpec(
        num_scalar_prefetch=2,
        grid=(n_blocks,),
        in_specs=[pl.BlockSpec((tm * SUBLANES, LANES), lambda j, be, nu: (jnp.minimum(j, nu[0] - 1), 0)),
                  wspec(d, ff), wspec(d, ff), wspec(ff, d)],
        out_specs=pl.BlockSpec((tm * SUBLANES, LANES), lambda j, be, nu: (j, 0)),
    )
    return pl.pallas_call(
        functools.partial(_gmm_body, tm=tm),
        grid_spec=grid_spec,
        out_shape=jax.ShapeDtypeStruct(xs.shape, F32),
        compiler_params=_cparams("arbitrary"),
        name="moe_gmm",
    )(block_expert, n_used, xs, wg, wu, wd)


def _combine_body(pos_ref, posn_ref, rt_ref, x_ref, mod_ref, fn_ref, y_hbm, o_ref, buf, sem, *, tc, final):
    nl = pl.num_programs(1)
    j = pl.program_id(0) * nl + pl.program_id(1)
    n = pl.num_programs(0) * nl
    slot = j % 2

    def start(ids_ref, s):
        def body(r, carry):
            for k in range(2):
                pltpu.make_async_copy(_token_tile(y_hbm, ids_ref[0, k, r]),
                                      _token_tile(buf.at[s], k * tc + r), sem.at[s]).start()
            return carry
        lax.fori_loop(0, tc, body, 0, unroll=4)

    @pl.when(j == 0)
    def _():
        start(pos_ref, 0)

    def wait(s):
        pltpu.make_async_copy(y_hbm.at[pl.ds(0, 2 * tc * SUBLANES), :], buf.at[s], sem.at[s]).wait()

    wait(slot)

    @pl.when(j + 1 < n)
    def _():
        start(posn_ref, 1 - slot)

    rt = rt_ref[0]
    cur = buf.at[slot]
    f = rt[:, 2:3] * _from_token_tiles(cur, 0, tc) + rt[:, 3:4] * _from_token_tiles(cur, tc, tc)
    _ffn_epilogue(x_ref, mod_ref, fn_ref, f, o_ref, final)


def _combine(route, x, mod, fnorm, pos, y_sorted, tc, final):
    b, l, d = x.shape
    nl = l // tc
    n = b * nl
    ids = lambda shift: pl.BlockSpec((1, 2, tc), lambda i, j: (jnp.minimum(i * nl + j + shift, n - 1), 0, 0),
                                     memory_space=pltpu.SMEM)
    row = lambda w: pl.BlockSpec((1, tc, w), lambda i, j: (i, j, 0))
    return pl.pallas_call(
        functools.partial(_combine_body, tc=tc, final=final),
        grid=(b, nl),
        in_specs=[ids(0), ids(1), row(LANES), row(d), pl.BlockSpec((1, 8, d), lambda i, j: (i, 0, 0)),
                  _const_spec((1, d)), pl.BlockSpec(memory_space=pl.ANY)],
        out_specs=row(d),
        out_shape=jax.ShapeDtypeStruct((b, l, d), F32),
        scratch_shapes=[pltpu.VMEM((2, 2 * tc * SUBLANES, LANES), F32), pltpu.SemaphoreType.DMA((2,))],
        compiler_params=_cparams("arbitrary", "arbitrary"),
        name="moe_combine",
    )(pos, pos, route, x, mod, fnorm, y_sorted)


def _moe_sparse(h2, route, x, mod, fnorm, wg, wu, wd, tm, tc, final):
    b, l, d = x.shape
    t = b * l
    ne = wg.shape[0]
    rt = route.reshape(t, LANES)
    experts = jnp.concatenate([rt[:, 0], rt[:, 1]]).astype(jnp.int32)
    onehot = (experts[:, None] == jnp.arange(ne, dtype=jnp.int32)[None, :]).astype(jnp.int32)
    csum = jnp.cumsum(onehot, axis=0)
    rank = jnp.sum((csum - onehot) * onehot, axis=1)
    counts = csum[-1]
    padded = ((counts + tm - 1) // tm) * tm
    ends = jnp.cumsum(padded)
    pos = jnp.sum(onehot * (ends - padded)[None, :], axis=1) + rank
    n_blocks = (2 * t) // tm + ne
    block_start = jnp.arange(n_blocks, dtype=jnp.int32) * tm
    block_expert = jnp.minimum(jnp.sum(block_start[:, None] >= ends[None, :], axis=1), ne - 1).astype(jnp.int32)
    n_used = (ends[-1:] // tm).astype(jnp.int32)
    tiled = lambda n: jnp.transpose(pos.astype(jnp.int32).reshape(2, t // n, n), (1, 0, 2))
    pos2 = tiled(tc)
    td = TILE_DISPATCH if t % TILE_DISPATCH == 0 else tc
    clear = jnp.concatenate([ends // tm - 1, n_used[0] + jnp.arange(ne, dtype=jnp.int32)])
    clear = jnp.clip(clear, 0, n_blocks - 1).astype(jnp.int32)
    xs = _dispatch(h2.reshape(t * SUBLANES, LANES), tiled(td), clear, n_blocks * tm, td, tm)
    y_sorted = _gmm(xs, block_expert, n_used, wg, wu, wd, tm)
    return _combine(route, x, mod, fnorm, pos2, y_sorted, tc, final)


TILE_LATENT = 512
TILE_CONTEXT = 256
TILE_EXPERT = 512
TILE_COMBINE = 256
TILE_DISPATCH = 1024


def _row_tile(l, want):
    return want if l % want == 0 else l


def kernel(x, c, ctx, c_ctx, ada_w, ada_b, norm_mix, norm_ffn, w_in, ssd_conv_w, ssd_conv_b, ssd_a_log, ssd_dt_bias, ssd_d, ssd_norm, ssd_out, na_rpb, na_out, conf_conv_w, conf_conv_b, conf_ln_g, conf_ln_b, conf_out, w_o, ffn_gate, ffn_up, ffn_down, moe_router, moe_gate, moe_up, moe_down, final_norm):
    depth = w_in.shape[0]
    b, l, d = x.shape
    lc = ctx.shape[1]
    nh = SSD_HEADS

    cc = jnp.zeros((8, d), F32).at[:b].set(c).at[b].set(c_ctx)
    mods = _ada(cc, ada_w, ada_b)
    fnorm = final_norm.reshape(1, d)

    offs = np.cumsum((0, SSD_D_INNER, SSD_XBC, nh, nh, NA_WIDTH, NA_WIDTH, NA_WIDTH, 2 * CONF_WIDTH, 3 * d))
    seg = lambda w, i: w[:, offs[i]:offs[i + 1]]

    for layer in range(depth):
        need_ctx = layer < depth - 1
        last = layer == depth - 1
        m6 = mods[layer].reshape(8, 6, d)
        mod_l = jnp.zeros((b, 8, d), F32).at[:, :6].set(m6[:b])
        mod_c = jnp.broadcast_to(jnp.zeros((8, d), F32).at[:6].set(m6[b]), (b, 8, d))

        wl = w_in[layer]
        w_main = jnp.concatenate([seg(wl, i) for i in (0, 1, 4, 5, 6, 7)], axis=1).astype(BF16)
        w_dt = jnp.concatenate([seg(wl, 2), seg(wl, 3)], axis=1)
        w_dt_pad = jnp.zeros((d, LANES), F32).at[:, :2 * nh].set(w_dt).astype(BF16)
        w_dtt = w_dt.T.astype(BF16)
        w_gate = seg(wl, 8).astype(BF16)
        dtb = ssd_dt_bias[layer].reshape(2 * nh)
        b_lane = jnp.zeros((1, LANES), F32).at[0, :2 * nh].set(dtb)
        b_sub = dtb.reshape(2 * nh, 1)
        alog = ssd_a_log[layer].reshape(2 * nh)
        a_lane = jnp.zeros((1, LANES), F32).at[0, :2 * nh].set(alog)
        a_sub = alog.reshape(2 * nh, 1)
        dskip = jnp.repeat(ssd_d[layer], SSD_HEAD_DIM).reshape(1, SSD_D_INNER)
        gmix = norm_mix[layer].reshape(1, d)
        gffn = norm_ffn[layer].reshape(1, d)
        snorm = ssd_norm[layer].reshape(1, SSD_D_INNER)
        wssd = ssd_out[layer].astype(BF16)
        wna = na_out[layer].astype(BF16)
        wcf = conf_out[layer].astype(BF16)
        wo = w_o[layer].astype(BF16)

        def mixer_inputs(xx, mod, tm):
            return _mixin(xx, mod[:, 0:1], mod[:, 1:2], gmix, w_main, w_dt_pad, w_dtt, b_lane, b_sub,
                          ssd_conv_w[layer], ssd_conv_b[layer], conf_conv_w[layer], conf_conv_b[layer],
                          conf_ln_g[layer], conf_ln_b[layer], tm)

        z_c, u_c, q_c, k_c, v_c, cf_c, dt_c, dtt_c = mixer_inputs(ctx, mod_c, _row_tile(lc, TILE_CONTEXT))
        z_l, u_l, q_l, k_l, v_l, cf_l, dt_l, dtt_l = mixer_inputs(x, mod_l, _row_tile(l, TILE_LATENT))

        h0 = jnp.zeros((b, 2, SSD_GROUPS, SSD_STATE, SSD_D_INNER // SSD_GROUPS), F32)
        yf_c, yb_c, st = _ssd(u_c, dt_c, dtt_c, a_lane, a_sub, h0)
        yf_l, yb_l, _ = _ssd(u_l, dt_l, dtt_l, a_lane, a_sub, st)

        na_l = _na_latent(q_l, k_l, v_l, k_c, v_c, na_rpb[layer])

        is_moe = layer % 2 == 1
        w_router = None
        if is_moe:
            wr_f = jnp.zeros((d, LANES), F32).at[:, :N_EXPERTS].set(moe_router[layer // 2])
            wr_hi = wr_f.astype(BF16)
            w_router = jnp.concatenate([wr_hi, (wr_f - wr_hi.astype(F32)).astype(BF16)], axis=1)
        tail = functools.partial(_tail, gmix=gmix, gffn=gffn, dskip=dskip, snorm=snorm, wgate=w_gate,
                                 wssd=wssd, wna=wna, wcf=wcf, wo=wo)
        outs = tail(x, mod_l, yf=yf_l, yb=yb_l, u=u_l, z=z_l, na=na_l, cf=cf_l, tm=_row_tile(l, TILE_LATENT),
                    w_router=w_router)
        x_mid, h2_l = outs[0], outs[1]
        if need_ctx:
            na_c = _na_context(q_c, k_c, v_c)
            outs_c = tail(ctx, mod_c, yf=yf_c, yb=yb_c, u=u_c, z=z_c, na=na_c, cf=cf_c,
                          tm=_row_tile(lc, TILE_CONTEXT), w_router=w_router)
            ctx_mid, h2_c = outs_c[0], outs_c[1]

        i = layer // 2
        if not is_moe:
            wg, wu, wd = (t[i].astype(BF16) for t in (ffn_gate, ffn_up, ffn_down))
            x = _ffn(h2_l, x_mid, mod_l, fnorm, wg, wu, wd, _row_tile(l, TILE_LATENT), last)
            if need_ctx:
                ctx = _ffn(h2_c, ctx_mid, mod_c, fnorm, wg, wu, wd, _row_tile(lc, TILE_CONTEXT), False)
        else:
            wg, wu, wd = (t[i].astype(BF16) for t in (moe_gate, moe_up, moe_down))
            x = _moe_sparse(h2_l, outs[2], x_mid, mod_l, fnorm, wg, wu, wd, TILE_EXPERT,
                            _row_tile(l, TILE_COMBINE), last)
            if need_ctx:
                ctx = _moe_sparse(h2_c, outs_c[2], ctx_mid, mod_c, fnorm, wg, wu, wd, TILE_EXPERT,
                                  _row_tile(lc, TILE_COMBINE), False)
    return x
```
